```python
import jax, jax.numpy as jnp
from jax import lax
import numpy as np

D_MODEL = 4096
BATCH = 4
SEQ = 2048
DEPTH = 1

HEAD_DIM = 128
MIX_WIDTH = D_MODEL
ATT_HEADS = MIX_WIDTH // (2 * HEAD_DIM)
ATT_WIDTH = ATT_HEADS * HEAD_DIM
CONV_WIDTH = MIX_WIDTH - ATT_WIDTH
CONV_GROUPS = CONV_WIDTH // HEAD_DIM
CONV_K = 3
Q_BLOCK = 128
IN_WIDTH = 3 * ATT_WIDTH + ATT_HEADS + 3 * CONV_WIDTH
N_EXPERTS = 32
TOP_K = 4
D_EXPERT = D_MODEL // 2
SWIGLU_ALPHA = 1.702
SWIGLU_LIMIT = 7.0
EXPERT_BLOCK = 128
RMS_EPS = 1e-6

kernel_name = "hymba_fox_shortconv_moe_block"


def rmsnorm(x, g):
    xf = x.astype(jnp.float32)
    y = xf * lax.rsqrt(jnp.mean(xf * xf, axis=-1, keepdims=True) + RMS_EPS)
    return (y * g.astype(jnp.float32)).astype(x.dtype)


def short_conv(u, w):
    return lax.conv_general_dilated(
        u, w[:, None, :], window_strides=(1,), padding=[(CONV_K - 1, 0)],
        dimension_numbers=("NWC", "WIO", "NWC"), feature_group_count=u.shape[-1])


def fox_attention(q, k, v, logf):
    b, h, s, d = q.shape
    c = jnp.cumsum(logf, axis=-1)
    nq = s // Q_BLOCK
    qb = q.reshape(b, h, nq, Q_BLOCK, d).transpose(2, 0, 1, 3, 4)
    cqb = c.reshape(b, h, nq, Q_BLOCK).transpose(2, 0, 1, 3)
    k_pos = jnp.arange(s)
    scale = 1.0 / np.sqrt(d)

    def block(args):
        qi, cq, i = args
        logits = jnp.einsum("bhqd,bhkd->bhqk", qi, k).astype(jnp.float32) * scale
        logits = logits + cq[..., :, None] - c[:, :, None, :]
        q_pos = i * Q_BLOCK + jnp.arange(Q_BLOCK)
        mask = k_pos[None, :] <= q_pos[:, None]
        p = jax.nn.softmax(jnp.where(mask, logits, -jnp.inf), axis=-1)
        return jnp.einsum("bhqk,bhkd->bhqd", p.astype(v.dtype), v)

    o = lax.map(block, (qb, cqb, jnp.arange(nq)))
    return o.transpose(1, 2, 0, 3, 4).reshape(b, h, s, d)


def mixer(h, w_in, b_forget, g_q, g_k, w_conv, g_attn_out, g_conv_out, w_out):
    b, s, _ = h.shape
    proj = h @ w_in
    cuts = list(np.cumsum([ATT_WIDTH, ATT_WIDTH, ATT_WIDTH, ATT_HEADS, CONV_WIDTH, CONV_WIDTH]))
    q, k, v, f_logit, gb, gc, xc = jnp.split(proj, cuts, axis=-1)

    q = rmsnorm(q.reshape(b, s, ATT_HEADS, HEAD_DIM), g_q).transpose(0, 2, 1, 3)
    k = rmsnorm(k.reshape(b, s, ATT_HEADS, HEAD_DIM), g_k).transpose(0, 2, 1, 3)
    v = v.reshape(b, s, ATT_HEADS, HEAD_DIM).transpose(0, 2, 1, 3)
    logf = jax.nn.log_sigmoid(f_logit.astype(jnp.float32) + b_forget.astype(jnp.float32))
    att = fox_attention(q, k, v, logf.transpose(0, 2, 1))
    att = att.transpose(0, 2, 1, 3).reshape(b, s, ATT_WIDTH)

    conv = gb * short_conv(gc * xc, w_conv)

    mixed = jnp.concatenate([rmsnorm(att, g_attn_out), rmsnorm(conv, g_conv_out)], axis=-1)
    return mixed @ w_out


def moe(h, w_router, b_router, w_gate_up, b_gate_up, w_down, b_down):
    t, d = h.shape
    logits = (h @ w_router + b_router).astype(jnp.float32)
    top_vals, top_idx = lax.top_k(logits, TOP_K)
    gates = jax.nn.softmax(top_vals, axis=-1).astype(h.dtype)

    a = t * TOP_K
    flat_e = top_idx.reshape(a)
    flat_tok = jnp.repeat(jnp.arange(t, dtype=jnp.int32), TOP_K)
    flat_g = gates.reshape(a)
    order = jnp.argsort(flat_e)
    se, stok, sg = flat_e[order], flat_tok[order], flat_g[order]

    counts = jnp.zeros((N_EXPERTS,), jnp.int32).at[flat_e].add(1)
    starts = jnp.cumsum(counts) - counts
    pcounts = (counts + EXPERT_BLOCK - 1) // EXPERT_BLOCK * EXPERT_BLOCK
    pends = jnp.cumsum(pcounts)
    pstarts = pends - pcounts
    pos = pstarts[se] + (jnp.arange(a, dtype=jnp.int32) - starts[se])

    n_blocks = -(-a // EXPERT_BLOCK) + N_EXPERTS
    p_len = n_blocks * EXPERT_BLOCK
    ptok = jnp.zeros((p_len,), jnp.int32).at[pos].set(stok)
    pg = jnp.zeros((p_len,), h.dtype).at[pos].set(sg)
    block_e = jnp.minimum(
        jnp.searchsorted(pends, jnp.arange(n_blocks, dtype=jnp.int32) * EXPERT_BLOCK, side="right"),
        N_EXPERTS - 1)

    def expert_block(args):
        tok, e = args
        xb = h[tok]
        gu = xb @ w_gate_up[e] + b_gate_up[e]
        gate, up = gu[:, :D_EXPERT], gu[:, D_EXPERT:]
        gate = jnp.minimum(gate, SWIGLU_LIMIT)
        up = jnp.clip(up, -SWIGLU_LIMIT, SWIGLU_LIMIT)
        act = (up + 1.0) * (gate * jax.nn.sigmoid(SWIGLU_ALPHA * gate))
        return act @ w_down[e] + b_down[e]

    ys = lax.map(expert_block, (ptok.reshape(n_blocks, EXPERT_BLOCK), block_e))
    return jnp.zeros_like(h).at[ptok].add(ys.reshape(p_len, d) * pg[:, None])


def setup_inputs(seed: int = 0) -> dict:
    key = jax.random.key(seed)
    ks = jax.random.split(key, 18)
    f32 = jnp.float32
    n = lambda k, shape, s: jax.random.normal(k, shape, f32) * s
    return {
        "x": n(ks[0], (BATCH, SEQ, D_MODEL), 1.0),
        "g_mix": 1.0 + n(ks[1], (DEPTH, D_MODEL), 0.02),
        "w_in": n(ks[2], (DEPTH, D_MODEL, IN_WIDTH), D_MODEL ** -0.5),
        "b_forget": n(ks[3], (DEPTH, ATT_HEADS), 0.1),
        "g_q": 1.0 + n(ks[4], (DEPTH, HEAD_DIM), 0.02),
        "g_k": 1.0 + n(ks[5], (DEPTH, HEAD_DIM), 0.02),
        "w_conv": n(ks[6], (DEPTH, CONV_K, CONV_WIDTH), CONV_K ** -0.5),
        "g_attn_out": 1.0 + n(ks[7], (DEPTH, ATT_WIDTH), 0.02),
        "g_conv_out": 1.0 + n(ks[8], (DEPTH, CONV_WIDTH), 0.02),
        "w_out": n(ks[9], (DEPTH, MIX_WIDTH, D_MODEL), MIX_WIDTH ** -0.5),
        "g_ffn": 1.0 + n(ks[10], (DEPTH, D_MODEL), 0.02),
        "w_router": n(ks[11], (DEPTH, D_MODEL, N_EXPERTS), D_MODEL ** -0.5),
        "b_router": n(ks[12], (DEPTH, N_EXPERTS), 0.01),
        "w_gate_up": n(ks[13], (DEPTH, N_EXPERTS, D_MODEL, 2 * D_EXPERT), D_MODEL ** -0.5),
        "b_gate_up": n(ks[14], (DEPTH, N_EXPERTS, 2 * D_EXPERT), 0.02),
        "w_down": n(ks[15], (DEPTH, N_EXPERTS, D_EXPERT, D_MODEL), D_EXPERT ** -0.5),
        "b_down": n(ks[16], (DEPTH, N_EXPERTS, D_MODEL), 0.02),
    }


def reference(x, g_mix, w_in, b_forget, g_q, g_k, w_conv, g_attn_out, g_conv_out, w_out,
              g_ffn, w_router, b_router, w_gate_up, b_gate_up, w_down, b_down):
    b, s, d = x.shape
    for l in range(DEPTH):
        h = rmsnorm(x, g_mix[l])
        x = x + mixer(h, w_in[l], b_forget[l], g_q[l], g_k[l], w_conv[l],
                      g_attn_out[l], g_conv_out[l], w_out[l])
        h2 = rmsnorm(x, g_ffn[l]).reshape(b * s, d)
        x = x + moe(h2, w_router[l], b_router[l], w_gate_up[l], b_gate_up[l],
                    w_down[l], b_down[l]).reshape(b, s, d)
    return x
```

```python
import functools

import jax
import jax.numpy as jnp
import numpy as np
from jax import lax
from jax.experimental import pallas as pl
from jax.experimental.pallas import tpu as pltpu

f32 = jnp.float32
bf16 = jnp.bfloat16
i32 = jnp.int32

HEAD_DIM = 128
LANES = 128
SUBLANES = 8
TOP_K = 4
RMS_EPS = 1e-6
SWIGLU_ALPHA = 1.702
SWIGLU_LIMIT = 7.0
VMEM_LIMIT_BYTES = 56 * 1024 * 1024

ROW_PAD = 128
EXPERT_CAP = 1280
MAX_ITEMS = 128


def _cparams(sem):
    return pltpu.CompilerParams(dimension_semantics=sem, vmem_limit_bytes=VMEM_LIMIT_BYTES)


def _log_sigmoid(z):
    return jnp.minimum(z, 0.0) - jnp.log1p(jnp.exp(-jnp.abs(z)))


def _split3(a):
    hi = a.astype(bf16)
    r1 = a - hi.astype(f32)
    mid = r1.astype(bf16)
    lo = (r1 - mid.astype(f32)).astype(bf16)
    return hi, mid, lo


def _rms(y, g):
    ms = jnp.mean(y * y, axis=-1, keepdims=True)
    return y * lax.rsqrt(ms + RMS_EPS) * g


def _prenorm_kernel(x_ref, g_ref, wft_ref, wfp_ref, bfc_ref, bfr_ref,
                    h_ref, crow_ref, ccol_ref, carry_r, carry_c, *, tiles_per_seq):
    i = pl.program_id(0)
    hb = _rms(x_ref[...], g_ref[...]).astype(bf16)
    h_ref[...] = hb
    f_row = lax.dot_general(wft_ref[...], hb, (((1,), (1,)), ((), ())), preferred_element_type=f32)
    f_col = jnp.dot(hb, wfp_ref[...], preferred_element_type=f32)
    lf_row = _log_sigmoid(f_row + bfc_ref[...])
    lf_col = _log_sigmoid(f_col + bfr_ref[...])
    tm = hb.shape[0]
    a = lax.broadcasted_iota(i32, (tm, tm), 0)
    b = lax.broadcasted_iota(i32, (tm, tm), 1)
    upper = (a <= b).astype(bf16)
    lower = (b <= a).astype(bf16)
    c_row = sum(jnp.dot(p, upper, preferred_element_type=f32) for p in _split3(lf_row))
    c_col = sum(jnp.dot(lower, p, preferred_element_type=f32) for p in _split3(lf_col))

    @pl.when(i % tiles_per_seq == 0)
    def _():
        carry_r[...] = jnp.zeros_like(carry_r)
        carry_c[...] = jnp.zeros_like(carry_c)

    c_row = c_row + carry_r[...]
    c_col = c_col + carry_c[...]
    crow_ref[...] = c_row
    ccol_ref[...] = c_col
    carry_r[...] = c_row[:, tm - 1:tm]
    carry_c[...] = c_col[tm - 1:tm, :]


def _prenorm(x2, g_mix, w_f, b_f, seq):
    t, d = x2.shape
    nh = w_f.shape[1]
    tm = 512
    wft = w_f.T.astype(bf16)
    wfp = jnp.pad(w_f, ((0, 0), (0, LANES - nh))).astype(bf16)
    bfc = b_f.reshape(nh, 1)
    bfr = jnp.pad(b_f, (0, LANES - nh)).reshape(1, LANES)
    return pl.pallas_call(
        functools.partial(_prenorm_kernel, tiles_per_seq=seq // tm),
        grid=(t // tm,),
        in_specs=[
            pl.BlockSpec((tm, d), lambda i: (i, 0)),
            pl.BlockSpec((1, d), lambda i: (0, 0)),
            pl.BlockSpec((nh, d), lambda i: (0, 0)),
            pl.BlockSpec((d, LANES), lambda i: (0, 0)),
            pl.BlockSpec((nh, 1), lambda i: (0, 0)),
            pl.BlockSpec((1, LANES), lambda i: (0, 0)),
        ],
        out_specs=[
            pl.BlockSpec((tm, d), lambda i: (i, 0)),
            pl.BlockSpec((nh, tm), lambda i: (0, i)),
            pl.BlockSpec((tm, LANES), lambda i: (i, 0)),
        ],
        out_shape=[
            jax.ShapeDtypeStruct((t, d), bf16),
            jax.ShapeDtypeStruct((nh, t), f32),
            jax.ShapeDtypeStruct((t, LANES), f32),
        ],
        scratch_shapes=[pltpu.VMEM((nh, 1), f32), pltpu.VMEM((1, LANES), f32)],
        compiler_params=_cparams(("arbitrary",)),
        name="prenorm",
    )(x2, g_mix.reshape(1, d), wft, wfp, bfc, bfr)


def _qkv_kernel(h_ref, w_ref, g_ref, o_ref, *, n_norm_blocks):
    j = pl.program_id(0)
    y = jnp.dot(h_ref[...], w_ref[...], preferred_element_type=f32)

    @pl.when(j < n_norm_blocks)
    def _():
        for hh in range(y.shape[1] // HEAD_DIM):
            sl = slice(hh * HEAD_DIM, (hh + 1) * HEAD_DIM)
            o_ref[:, sl] = _rms(y[:, sl], g_ref[:, sl]).astype(bf16)

    @pl.when(j >= n_norm_blocks)
    def _():
        o_ref[...] = y.astype(bf16)


def _qkv(h, w_qkv, g_q, g_k, att_width):
    t, d = h.shape
    n = w_qkv.shape[1]
    tm, tn = 1024, 512
    nheads = att_width // HEAD_DIM
    gain = jnp.concatenate([jnp.tile(g_q, nheads), jnp.tile(g_k, nheads),
                            jnp.ones((att_width,), f32)]).reshape(1, n)
    return pl.pallas_call(
        functools.partial(_qkv_kernel, n_norm_blocks=2 * att_width // tn),
        grid=(n // tn, t // tm),
        in_specs=[
            pl.BlockSpec((tm, d), lambda j, i: (i, 0)),
            pl.BlockSpec((d, tn), lambda j, i: (0, j)),
            pl.BlockSpec((1, tn), lambda j, i: (0, j)),
        ],
        out_specs=pl.BlockSpec((tm, tn), lambda j, i: (i, j)),
        out_shape=jax.ShapeDtypeStruct((t, n), bf16),
        compiler_params=_cparams(("arbitrary", "arbitrary")),
        name="qkv",
    )(h, w_qkv, gain)


def _conv_kernel(h_ref, wb_ref, wc_ref, wx_ref, wk_ref, o_ref, carry, *, tiles_per_seq):
    i = pl.program_id(1)
    h = h_ref[...]
    gb = jnp.dot(h, wb_ref[...], preferred_element_type=f32)
    gc = jnp.dot(h, wc_ref[...], preferred_element_type=f32)
    xc = jnp.dot(h, wx_ref[...], preferred_element_type=f32)
    u = gc * xc
    tm = u.shape[0]

    @pl.when(i % tiles_per_seq == 0)
    def _():
        carry[...] = jnp.zeros_like(carry)

    p2 = carry[SUBLANES - 2:SUBLANES - 1, :]
    p1 = carry[SUBLANES - 1:SUBLANES, :]
    row = lax.broadcasted_iota(i32, u.shape, 0)
    u1 = jnp.where(row == 0, p1, pltpu.roll(u, 1, 0))
    u2 = jnp.where(row == 0, p2, jnp.where(row == 1, p1, pltpu.roll(u, 2, 0)))
    y = wk_ref[0:1, :] * u2 + wk_ref[1:2, :] * u1 + wk_ref[2:3, :] * u
    o_ref[...] = gb * y
    carry[...] = u[tm - SUBLANES:tm, :]


def _conv(h, w_cv, w_conv, seq):
    t, d = h.shape
    width = w_cv.shape[1] // 3
    tm, tn = 1024, 256
    nb = width // tn
    return pl.pallas_call(
        functools.partial(_conv_kernel, tiles_per_seq=seq // tm),
        grid=(nb, t // tm),
        in_specs=[
            pl.BlockSpec((tm, d), lambda j, i: (i, 0)),
            pl.BlockSpec((d, tn), lambda j, i: (0, j)),
            pl.BlockSpec((d, tn), lambda j, i: (0, nb + j)),
            pl.BlockSpec((d, tn), lambda j, i: (0, 2 * nb + j)),
            pl.BlockSpec((3, tn), lambda j, i: (0, j)),
        ],
        out_specs=pl.BlockSpec((tm, tn), lambda j, i: (i, j)),
        out_shape=jax.ShapeDtypeStruct((t, width), f32),
        scratch_shapes=[pltpu.VMEM((SUBLANES, tn), f32)],
        compiler_params=_cparams(("arbitrary", "arbitrary")),
        name="conv",
    )(h, w_cv, w_cv, w_cv, w_conv)


def _attn_kernel(q_ref, k_ref, v_ref, crow_ref, ccol_ref, o_ref, *, tq, scale):
    hd = pl.program_id(1)
    qi = pl.program_id(2)
    q = q_ref[...]
    lane = lax.broadcasted_iota(i32, (tq, LANES), 1)
    cq = jnp.sum(jnp.where(lane == hd, ccol_ref[...], 0.0), axis=1, keepdims=True)

    def scores(j):
        rows = pl.ds(pl.multiple_of(j * tq, tq), tq)
        s = lax.dot_general(q, k_ref[rows, :], (((1,), (1,)), ((), ())),
                            preferred_element_type=f32) * scale
        return s + cq - crow_ref[pl.ds(j, 1), :], v_ref[rows, :]

    def update(s, vj, carry):
        m, l, acc = carry
        m_new = jnp.maximum(m, jnp.max(s, axis=1, keepdims=True))
        alpha = jnp.exp(m - m_new)
        p = jnp.exp(s - m_new)
        l = alpha * l + jnp.sum(p, axis=1, keepdims=True)
        acc = alpha * acc + jnp.dot(p.astype(bf16), vj, preferred_element_type=f32)
        return m_new, l, acc

    def body(j, carry):
        s, vj = scores(j)
        return update(s, vj, carry)

    init = (jnp.full((tq, 1), -jnp.inf, f32), jnp.zeros((tq, 1), f32), jnp.zeros((tq, HEAD_DIM), f32))
    carry = lax.fori_loop(0, qi, body, init)
    s, vj = scores(qi)
    r = lax.broadcasted_iota(i32, (tq, tq), 0)
    c = lax.broadcasted_iota(i32, (tq, tq), 1)
    s = jnp.where(c <= r, s, -jnp.inf)
    _, l, acc = update(s, vj, carry)
    o_ref[...] = acc / l


def _attention(qkv, c_row, c_col, batch, seq, nheads):
    t = qkv.shape[0]
    tq = 512
    nq = seq // tq
    crow4 = c_row.reshape(nheads, batch, nq, tq)
    return pl.pallas_call(
        functools.partial(_attn_kernel, tq=tq, scale=1.0 / np.sqrt(HEAD_DIM)),
        grid=(batch, nheads, nq),
        in_specs=[
            pl.BlockSpec((tq, HEAD_DIM), lambda b, h, qi: (b * nq + qi, h)),
            pl.BlockSpec((seq, HEAD_DIM), lambda b, h, qi: (b, nheads + h)),
            pl.BlockSpec((seq, HEAD_DIM), lambda b, h, qi: (b, 2 * nheads + h)),
            pl.BlockSpec((None, None, nq, tq), lambda b, h, qi: (h, b, 0, 0)),
            pl.BlockSpec((tq, LANES), lambda b, h, qi: (b * nq + qi, 0)),
        ],
        out_specs=pl.BlockSpec((tq, HEAD_DIM), lambda b, h, qi: (b * nq + qi, h)),
        out_shape=jax.ShapeDtypeStruct((t, nheads * HEAD_DIM), f32),
        compiler_params=_cparams(("arbitrary", "arbitrary", "arbitrary")),
        name="attention",
    )(qkv, qkv, qkv, crow4, c_col)


def _outproj_kernel(att_ref, conv_ref, ga_ref, gc_ref, w_ref, x_ref, o_ref, mixed):
    j = pl.program_id(1)
    wa = att_ref.shape[1]

    @pl.when(j == 0)
    def _():
        mixed[:, :wa] = _rms(att_ref[...], ga_ref[...]).astype(bf16)
        mixed[:, wa:] = _rms(conv_ref[...], gc_ref[...]).astype(bf16)

    o_ref[...] = x_ref[...] + jnp.dot(mixed[...], w_ref[...], preferred_element_type=f32)


def _outproj(att, conv, g_a, g_c, w_o, x2):
    t, d = x2.shape
    wa, wc = att.shape[1], conv.shape[1]
    tm, tn = 512, 512
    return pl.pallas_call(
        _outproj_kernel,
        grid=(t // tm, d // tn),
        in_specs=[
            pl.BlockSpec((tm, wa), lambda i, j: (i, 0)),
            pl.BlockSpec((tm, wc), lambda i, j: (i, 0)),
            pl.BlockSpec((1, wa), lambda i, j: (0, 0)),
            pl.BlockSpec((1, wc), lambda i, j: (0, 0)),
            pl.BlockSpec((wa + wc, tn), lambda i, j: (0, j)),
            pl.BlockSpec((tm, tn), lambda i, j: (i, j)),
        ],
        out_specs=pl.BlockSpec((tm, tn), lambda i, j: (i, j)),
        out_shape=jax.ShapeDtypeStruct((t, d), f32),
        scratch_shapes=[pltpu.VMEM((tm, wa + wc), bf16)],
        compiler_params=_cparams(("arbitrary", "arbitrary")),
        name="outproj",
    )(att, conv, g_a.reshape(1, wa), g_c.reshape(1, wc), w_o, x2)


def _ffn_norm_kernel(x_ref, g_ref, wrt_ref, br_ref, h_ref, lg_ref):
    h = _rms(x_ref[...], g_ref[...])
    h_ref[...] = h
    lg_ref[...] = lax.dot_general(wrt_ref[...], h, (((1,), (1,)), ((), ())),
                                  precision=lax.Precision.HIGHEST,
                                  preferred_element_type=f32) + br_ref[...]


def _ffn_norm(x1, g_ffn, w_router, b_router):
    t, d = x1.shape
    ne = w_router.shape[1]
    tm = 512
    return pl.pallas_call(
        _ffn_norm_kernel,
        grid=(t // tm,),
        in_specs=[
            pl.BlockSpec((tm, d), lambda i: (i, 0)),
            pl.BlockSpec((1, d), lambda i: (0, 0)),
            pl.BlockSpec((ne, d), lambda i: (0, 0)),
            pl.BlockSpec((ne, 1), lambda i: (0, 0)),
        ],
        out_specs=[
            pl.BlockSpec((tm, d), lambda i: (i, 0)),
            pl.BlockSpec((ne, tm), lambda i: (0, i)),
        ],
        out_shape=[jax.ShapeDtypeStruct((t, d), f32), jax.ShapeDtypeStruct((ne, t), f32)],
        compiler_params=_cparams(("arbitrary",)),
        name="ffn_norm",
    )(x1, g_ffn.reshape(1, d), w_router.T, b_router.reshape(ne, 1))


def _cumsum_sublanes(x):
    n = x.shape[0]
    row = lax.broadcasted_iota(i32, x.shape, 0)
    d = 1
    while d < n:
        x = x + jnp.where(row >= d, pltpu.roll(x, d, 0), 0.0)
        d *= 2
    return x


def _stack_rows(rows, n):
    width = rows[0].shape[1]
    sub = lax.broadcasted_iota(i32, (n, width), 0)
    out = jnp.zeros((n, width), rows[0].dtype)
    for k, r in enumerate(rows):
        out = jnp.where(sub == k, r, out)
    return out


def _route_kernel(lg_ref, pos_ref, gcol_ref, meta_ref, esel, rsel, *, chunk):
    ne, t = lg_ref.shape
    e_iota = lax.broadcasted_iota(i32, (ne, chunk), 0).astype(f32)
    a = lax.broadcasted_iota(i32, (chunk, chunk), 0)
    b = lax.broadcasted_iota(i32, (chunk, chunk), 1)
    before = (a < b).astype(bf16)
    counts = jnp.zeros((ne, 1), f32)
    for c in range(t // chunk):
        sl = slice(c * chunk, (c + 1) * chunk)
        vals = lg_ref[:, sl]
        tops, idxs, hots = [], [], []
        for _ in range(TOP_K):
            m = jnp.max(vals, axis=0, keepdims=True)
            idx = jnp.min(jnp.where(vals == m, e_iota, ne), axis=0, keepdims=True)
            hot = e_iota == idx
            vals = jnp.where(hot, -jnp.inf, vals)
            tops.append(m)
            idxs.append(idx)
            hots.append(hot)
        ex = [jnp.exp(v - tops[0]) for v in tops]
        den = ex[0] + ex[1] + ex[2] + ex[3]
        gates = [e / den for e in ex]
        member = sum(h.astype(f32) for h in hots)
        rank = jnp.dot(member.astype(bf16), before, preferred_element_type=f32) + counts
        counts = counts + jnp.sum(member, axis=1, keepdims=True)
        for k in range(TOP_K):
            rsel[k:k + 1, sl] = jnp.sum(jnp.where(hots[k], rank, 0.0), axis=0, keepdims=True)
            esel[k:k + 1, sl] = idxs[k]
        g8 = _stack_rows(gates, SUBLANES)
        gpad = jnp.concatenate([g8, jnp.zeros((LANES - SUBLANES, chunk), f32)], axis=0)
        gcol_ref[sl, :] = gpad.T

    cnt = jnp.broadcast_to(counts, (ne, LANES))
    pcnt = jnp.ceil(cnt * (1.0 / ROW_PAD)) * ROW_PAD
    pend = _cumsum_sublanes(pcnt)
    pstart = pend - pcnt
    npass = jnp.ceil(pcnt * (1.0 / EXPERT_CAP))
    iend = _cumsum_sublanes(npass)
    istart = iend - npass
    n_items = jnp.max(iend, axis=0, keepdims=True)

    pstart_col = pstart[:, 0:1]
    for c in range(t // chunk):
        sl = slice(c * chunk, (c + 1) * chunk)
        for k in range(TOP_K):
            hot = e_iota == esel[k:k + 1, sl]
            ps = jnp.sum(jnp.where(hot, pstart_col, 0.0), axis=0, keepdims=True)
            pos_ref[k:k + 1, sl] = (rsel[k:k + 1, sl] + ps).astype(i32)

    e_sub = lax.broadcasted_iota(i32, (ne, LANES), 0)
    lane = lax.broadcasted_iota(i32, (ne, LANES), 1)
    lane_f = lane.astype(f32)
    item = lane_f[0:1, :]
    e_of = jnp.minimum(jnp.sum((iend <= lane_f).astype(f32), axis=0, keepdims=True), ne - 1.0)
    hot = e_sub.astype(f32) == e_of
    pick = lambda v: jnp.sum(jnp.where(hot, v, 0.0), axis=0, keepdims=True)
    p_i = item - pick(istart)
    row0 = pick(pstart) + p_i * EXPERT_CAP
    nrows = jnp.clip(pick(pcnt) - p_i * EXPERT_CAP, 0.0, float(EXPERT_CAP))
    nsub = nrows * (1.0 / ROW_PAD)
    diag = e_sub == lane
    zrow = jnp.sum(jnp.where(diag, pend - ROW_PAD, 0.0), axis=0, keepdims=True)
    zval = jnp.sum(jnp.where(diag, (cnt > 0).astype(f32), 0.0), axis=0, keepdims=True)
    total = jnp.max(pend, axis=0, keepdims=True)
    rows = [e_of, row0, nsub, n_items, zrow, zval, total]
    meta_ref[...] = _stack_rows([r.astype(i32) for r in rows], SUBLANES)


def _route(logits_t):
    ne, t = logits_t.shape
    chunk = 512
    return pl.pallas_call(
        functools.partial(_route_kernel, chunk=chunk),
        out_shape=[
            jax.ShapeDtypeStruct((TOP_K, t), i32),
            jax.ShapeDtypeStruct((t, LANES), f32),
            jax.ShapeDtypeStruct((SUBLANES, LANES), i32),
        ],
        scratch_shapes=[pltpu.VMEM((TOP_K, t), f32), pltpu.VMEM((TOP_K, t), f32)],
        compiler_params=pltpu.CompilerParams(vmem_limit_bytes=VMEM_LIMIT_BYTES),
        name="route",
    )(logits_t)


_M_EXPERT, _M_ROW0, _M_NSUB, _M_NITEMS, _M_ZROW, _M_ZVALID, _M_TOTAL = range(7)


def _zero_tail(meta, zbuf, dst_ref, sem):
    total = pl.multiple_of(meta[_M_TOTAL, 0], ROW_PAD)
    n = (dst_ref.shape[0] - total) // ROW_PAD

    def copy(r):
        return pltpu.make_async_copy(zbuf, dst_ref.at[pl.ds(total + r * ROW_PAD, ROW_PAD), :], sem)

    def start(r, c):
        copy(r).start()
        return c

    def wait(r, c):
        copy(r).wait()
        return c

    lax.fori_loop(0, n, start, 0)
    lax.fori_loop(0, n, wait, 0)


def _dispatch_kernel(meta, pos_ref, h_ref, xs_ref, zbuf, sem, zsem, *, tmd, ne):
    i = pl.program_id(0)

    def zero_copy(e):
        start = pl.multiple_of(meta[_M_ZROW, e], ROW_PAD)
        return pltpu.make_async_copy(zbuf, xs_ref.at[pl.ds(start, ROW_PAD), :], zsem)

    @pl.when(i == 0)
    def _():
        zbuf[...] = jnp.zeros_like(zbuf)

        def zstart(e, c):
            @pl.when(meta[_M_ZVALID, e] > 0)
            def _():
                zero_copy(e).start()
            return c

        def zwait(e, c):
            @pl.when(meta[_M_ZVALID, e] > 0)
            def _():
                zero_copy(e).wait()
            return c

        lax.fori_loop(0, ne, zstart, 0)
        lax.fori_loop(0, ne, zwait, 0)
        _zero_tail(meta, zbuf, xs_ref, zsem)

    def row_copy(r, k):
        return pltpu.make_async_copy(h_ref.at[pl.ds(i * tmd + r, 1), :],
                                     xs_ref.at[pl.ds(pos_ref[k, r], 1), :], sem)

    def issue(r, c):
        for k in range(TOP_K):
            row_copy(r, k).start()
        return c

    def drain(r, c):
        for k in range(TOP_K):
            row_copy(r, k).wait()
        return c

    lax.fori_loop(0, tmd, issue, 0)
    lax.fori_loop(0, tmd, drain, 0)


def _dispatch(meta, pos_t, h2, n_rows, ne):
    t, d = h2.shape
    tmd = 256
    return pl.pallas_call(
        functools.partial(_dispatch_kernel, tmd=tmd, ne=ne),
        grid_spec=pltpu.PrefetchScalarGridSpec(
            num_scalar_prefetch=1,
            grid=(t // tmd,),
            in_specs=[
                pl.BlockSpec((TOP_K, tmd), lambda i, m: (0, i), memory_space=pltpu.SMEM),
                pl.BlockSpec(memory_space=pl.ANY),
            ],
            out_specs=pl.BlockSpec(memory_space=pl.ANY),
            scratch_shapes=[pltpu.VMEM((ROW_PAD, d), f32), pltpu.SemaphoreType.DMA, pltpu.SemaphoreType.DMA],
        ),
        out_shape=jax.ShapeDtypeStruct((n_rows, d), f32),
        compiler_params=_cparams(("arbitrary",)),
        name="dispatch",
    )(meta, pos_t, h2)


def _expert_kernel(meta, xs_ref, wg_ref, wu_ref, bg_ref, bu_ref, wd_ref, bd_ref, ys_ref,
                   x_scr, h_scr, wgu_scr, wd_scr, stage, y_scr, ld_sem, st_sem, *, nf, nd, tf, td):
    it = pl.program_id(0)
    s = pl.program_id(1)

    @pl.when((it == 0) & (s == 0))
    def _():
        stage[0] = jnp.zeros(stage.shape[1:], f32)
        _zero_tail(meta, stage.at[0], ys_ref, ld_sem.at[0])

    @pl.when(it < meta[_M_NITEMS, 0])
    def _():
        row0 = pl.multiple_of(meta[_M_ROW0, it], ROW_PAD)
        nsub = meta[_M_NSUB, it]

        def rows_of(r):
            return pl.ds(pl.multiple_of(r * ROW_PAD, ROW_PAD), ROW_PAD)

        def load_copy(r, slot):
            return pltpu.make_async_copy(xs_ref.at[pl.ds(row0 + r * ROW_PAD, ROW_PAD), :],
                                         stage.at[slot], ld_sem.at[slot])

        @pl.when(s == 0)
        def _():
            load_copy(0, 0).start()

            def body(r, c):
                slot = r % 2
                load_copy(r, slot).wait()

                @pl.when(r + 1 < nsub)
                def _():
                    load_copy(r + 1, 1 - slot).start()

                x_scr[rows_of(r), :] = stage[slot].astype(bf16)
                return c

            lax.fori_loop(0, nsub, body, 0)

        @pl.when(s < nf)
        def _():
            wgu_scr[:, :tf] = wg_ref[...].astype(bf16)
            wgu_scr[:, tf:] = wu_ref[...].astype(bf16)

            def body(r, c):
                gu = jnp.dot(x_scr[rows_of(r), :], wgu_scr[...], preferred_element_type=f32)
                g = jnp.minimum(gu[:, :tf] + bg_ref[...], SWIGLU_LIMIT)
                u = jnp.clip(gu[:, tf:] + bu_ref[...], -SWIGLU_LIMIT, SWIGLU_LIMIT)
                act = (u + 1.0) * (g * jax.nn.sigmoid(SWIGLU_ALPHA * g))
                h_scr[s, rows_of(r), :] = act.astype(bf16)
                return c

            lax.fori_loop(0, nsub, body, 0)

        @pl.when(s >= nf)
        def _():
            dcol = s - nf
            slot = dcol % 2
            wd_scr[...] = wd_ref[...].astype(bf16)

            def store_copy(r, sl):
                dst = ys_ref.at[pl.ds(row0 + r * ROW_PAD, ROW_PAD), pl.ds(pl.multiple_of(dcol * td, td), td)]
                return pltpu.make_async_copy(y_scr.at[sl, rows_of(r), :], dst, st_sem.at[sl])

            def wait_slot(sl):
                def w(r, c):
                    store_copy(r, sl).wait()
                    return c
                lax.fori_loop(0, nsub, w, 0)

            @pl.when(dcol >= 2)
            def _():
                wait_slot(slot)

            def body(r, c):
                acc = jnp.broadcast_to(bd_ref[...], (ROW_PAD, td))
                for cf in range(nf):
                    acc = acc + jnp.dot(h_scr[cf, rows_of(r), :], wd_scr[cf * tf:(cf + 1) * tf, :],
                                        preferred_element_type=f32)
                y_scr[slot, rows_of(r), :] = acc
                store_copy(r, slot).start()
                return c

            lax.fori_loop(0, nsub, body, 0)

            @pl.when(dcol == nd - 1)
            def _():
                wait_slot(slot)
                wait_slot(1 - slot)


def _experts(meta, xs, w_gate_up, b_gate_up, w_down, b_down):
    n_rows, d = xs.shape
    ne, _, f2 = w_gate_up.shape
    f = f2 // 2
    tf, td = 256, 256
    nf, nd = f // tf, d // td
    max_items = (n_rows + ne * (EXPERT_CAP - ROW_PAD)) // EXPERT_CAP
    assert max_items <= MAX_ITEMS and nd >= 2
    bgu = b_gate_up.reshape(ne, 1, f2)
    bd = b_down.reshape(ne, 1, d)

    def item(i, m):
        return jnp.minimum(i, m[_M_NITEMS, 0] - 1)

    def up_block(i, s, m):
        return jnp.where(i < m[_M_NITEMS, 0], jnp.minimum(s, nf - 1), nf - 1)

    def down_block(i, s, m):
        return jnp.where(i < m[_M_NITEMS, 0], jnp.clip(s - nf, 0, nd - 1), nd - 1)

    return pl.pallas_call(
        functools.partial(_expert_kernel, nf=nf, nd=nd, tf=tf, td=td),
        grid_spec=pltpu.PrefetchScalarGridSpec(
            num_scalar_prefetch=1,
            grid=(max_items, nf + nd),
            in_specs=[
                pl.BlockSpec(memory_space=pl.ANY),
                pl.BlockSpec((None, d, tf), lambda i, s, m: (m[_M_EXPERT, item(i, m)], 0, up_block(i, s, m))),
                pl.BlockSpec((None, d, tf), lambda i, s, m: (m[_M_EXPERT, item(i, m)], 0, nf + up_block(i, s, m))),
                pl.BlockSpec((None, 1, tf), lambda i, s, m: (m[_M_EXPERT, item(i, m)], 0, up_block(i, s, m))),
                pl.BlockSpec((None, 1, tf), lambda i, s, m: (m[_M_EXPERT, item(i, m)], 0, nf + up_block(i, s, m))),
                pl.BlockSpec((None, f, td), lambda i, s, m: (m[_M_EXPERT, item(i, m)], 0, down_block(i, s, m))),
                pl.BlockSpec((None, 1, td), lambda i, s, m: (m[_M_EXPERT, item(i, m)], 0, down_block(i, s, m))),
            ],
            out_specs=pl.BlockSpec(memory_space=pl.ANY),
            scratch_shapes=[
                pltpu.VMEM((EXPERT_CAP, d), bf16),
                pltpu.VMEM((nf, EXPERT_CAP, tf), bf16),
                pltpu.VMEM((d, 2 * tf), bf16),
                pltpu.VMEM((f, td), bf16),
                pltpu.VMEM((2, ROW_PAD, d), f32),
                pltpu.VMEM((2, EXPERT_CAP, td), f32),
                pltpu.SemaphoreType.DMA((2,)),
                pltpu.SemaphoreType.DMA((2,)),
            ],
        ),
        out_shape=jax.ShapeDtypeStruct((n_rows, d), f32),
        compiler_params=_cparams(("arbitrary", "arbitrary")),
        name="experts",
    )(meta, xs, w_gate_up, w_gate_up, bgu, bgu, w_down, bd)


def _combine_kernel(pos_cur, pos_nxt, gcol_ref, x_ref, ys_ref, o_ref, gbuf, sem, *, tmc, n_tiles):
    i = pl.program_id(0)
    slot = i % 2

    def row_copy(pos_ref, sl, r, k):
        return pltpu.make_async_copy(ys_ref.at[pl.ds(pos_ref[k, r], 1), :],
                                     gbuf.at[sl, k, pl.ds(r, 1), :], sem.at[sl])

    def issue(pos_ref, sl):
        def body(r, c):
            for k in range(TOP_K):
                row_copy(pos_ref, sl, r, k).start()
            return c
        lax.fori_loop(0, tmc, body, 0)

    @pl.when(i == 0)
    def _():
        issue(pos_cur, 0)

    @pl.when(i + 1 < n_tiles)
    def _():
        issue(pos_nxt, 1 - slot)

    def drain(r, c):
        for k in range(TOP_K):
            row_copy(pos_cur, slot, r, k).wait()
        return c

    lax.fori_loop(0, tmc, drain, 0)
    acc = x_ref[...]
    for k in range(TOP_K):
        acc = acc + gcol_ref[:, k:k + 1] * gbuf[slot, k]
    o_ref[...] = acc


def _combine(pos_t, gcol, x1, ys):
    t, d = x1.shape
    tmc = 128
    n_tiles = t // tmc
    return pl.pallas_call(
        functools.partial(_combine_kernel, tmc=tmc, n_tiles=n_tiles),
        grid=(n_tiles,),
        in_specs=[
            pl.BlockSpec((TOP_K, tmc), lambda i: (0, i), memory_space=pltpu.SMEM),
            pl.BlockSpec((TOP_K, tmc), lambda i: (0, jnp.minimum(i + 1, n_tiles - 1)), memory_space=pltpu.SMEM),
            pl.BlockSpec((tmc, LANES), lambda i: (i, 0)),
            pl.BlockSpec((tmc, d), lambda i: (i, 0)),
            pl.BlockSpec(memory_space=pl.ANY),
        ],
        out_specs=pl.BlockSpec((tmc, d), lambda i: (i, 0)),
        out_shape=jax.ShapeDtypeStruct((t, d), f32),
        scratch_shapes=[pltpu.VMEM((2, TOP_K, tmc, d), f32), pltpu.SemaphoreType.DMA((2,))],
        compiler_params=_cparams(("arbitrary",)),
        name="combine",
    )(pos_t, pos_t, gcol, x1, ys)


def _layer(x2, batch, seq, g_mix, w_in, b_forget, g_q, g_k, w_conv, g_attn_out, g_conv_out, w_out,
           g_ffn, w_router, b_router, w_gate_up, b_gate_up, w_down, b_down):
    t, d = x2.shape
    nheads = b_forget.shape[0]
    att_width = nheads * HEAD_DIM
    ne = w_router.shape[1]
    c0 = 3 * att_width
    w_qkv = w_in[:, :c0].astype(bf16)
    w_f = w_in[:, c0:c0 + nheads]
    w_cv = w_in[:, c0 + nheads:].astype(bf16)

    h, c_row, c_col = _prenorm(x2, g_mix, w_f, b_forget, seq)
    qkv = _qkv(h, w_qkv, g_q, g_k, att_width)
    conv = _conv(h, w_cv, w_conv, seq)
    att = _attention(qkv, c_row, c_col, batch, seq, nheads)
    x1 = _outproj(att, conv, g_attn_out, g_conv_out, w_out.astype(bf16), x2)

    h2, logits_t = _ffn_norm(x1, g_ffn, w_router, b_router)
    pos_t, gcol, meta = _route(logits_t)
    n_rows = t * TOP_K + ne * ROW_PAD
    xs = _dispatch(meta, pos_t, h2, n_rows, ne)
    ys = _experts(meta, xs, w_gate_up, b_gate_up, w_down, b_down)
    return _combine(pos_t, gcol, x1, ys)


def kernel(x, g_mix, w_in, b_forget, g_q, g_k, w_conv, g_attn_out, g_conv_out, w_out, g_ffn, w_router,
           b_router, w_gate_up, b_gate_up, w_down, b_down):
    b, s, d = x.shape
    x2 = x.reshape(b * s, d)
    for l in range(g_mix.shape[0]):
        x2 = _layer(x2, b, s, g_mix[l], w_in[l], b_forget[l], g_q[l], g_k[l], w_conv[l], g_attn_out[l],
                    g_conv_out[l], w_out[l], g_ffn[l], w_router[l], b_router[l], w_gate_up[l],
                    b_gate_up[l], w_down[l], b_down[l])
    return x2.reshape(b, s, d)
```

```python
import functools

import jax
import jax.numpy as jnp
import numpy as np
from jax import lax
from jax.experimental import pallas as pl
from jax.experimental.pallas import tpu as pltpu

f32 = jnp.float32
bf16 = jnp.bfloat16
i32 = jnp.int32

HEAD_DIM = 128
LANES = 128
SUBLANES = 8
TOP_K = 4
RMS_EPS = 1e-6
SWIGLU_ALPHA = 1.702
SWIGLU_LIMIT = 7.0
VMEM_LIMIT_BYTES = 56 * 1024 * 1024

ROW_PAD = 128
EXPERT_CAP = 1152
COMPUTE_ROWS = 256
MAX_ITEMS = 128


def _cparams(sem):
    return pltpu.CompilerParams(dimension_semantics=sem, vmem_limit_bytes=VMEM_LIMIT_BYTES)


def _log_sigmoid(z):
    return jnp.minimum(z, 0.0) - jnp.log1p(jnp.exp(-jnp.abs(z)))


def _split3(a):
    hi = a.astype(bf16)
    r1 = a - hi.astype(f32)
    mid = r1.astype(bf16)
    lo = (r1 - mid.astype(f32)).astype(bf16)
    return hi, mid, lo


def _rms(y, g):
    ms = jnp.mean(y * y, axis=-1, keepdims=True)
    return y * lax.rsqrt(ms + RMS_EPS) * g


def _prenorm_kernel(x_ref, g_ref, wft_ref, wfp_ref, bfc_ref, bfr_ref,
                    h_ref, crow_ref, ccol_ref, carry_r, carry_c, *, tiles_per_seq):
    i = pl.program_id(0)
    hb = _rms(x_ref[...], g_ref[...]).astype(bf16)
    h_ref[...] = hb
    f_row = lax.dot_general(wft_ref[...], hb, (((1,), (1,)), ((), ())), preferred_element_type=f32)
    f_col = jnp.dot(hb, wfp_ref[...], preferred_element_type=f32)
    lf_row = _log_sigmoid(f_row + bfc_ref[...])
    lf_col = _log_sigmoid(f_col + bfr_ref[...])
    tm = hb.shape[0]
    a = lax.broadcasted_iota(i32, (tm, tm), 0)
    b = lax.broadcasted_iota(i32, (tm, tm), 1)
    upper = (a <= b).astype(bf16)
    lower = (b <= a).astype(bf16)
    c_row = sum(jnp.dot(p, upper, preferred_element_type=f32) for p in _split3(lf_row))
    c_col = sum(jnp.dot(lower, p, preferred_element_type=f32) for p in _split3(lf_col))

    @pl.when(i % tiles_per_seq == 0)
    def _():
        carry_r[...] = jnp.zeros_like(carry_r)
        carry_c[...] = jnp.zeros_like(carry_c)

    c_row = c_row + carry_r[...]
    c_col = c_col + carry_c[...]
    crow_ref[...] = c_row
    ccol_ref[...] = c_col
    carry_r[...] = c_row[:, tm - 1:tm]
    carry_c[...] = c_col[tm - 1:tm, :]


def _prenorm(x2, g_mix, w_f, b_f, seq):
    t, d = x2.shape
    nh = w_f.shape[1]
    tm = 512
    wft = w_f.T.astype(bf16)
    wfp = jnp.pad(w_f, ((0, 0), (0, LANES - nh))).astype(bf16)
    bfc = b_f.reshape(nh, 1)
    bfr = jnp.pad(b_f, (0, LANES - nh)).reshape(1, LANES)
    return pl.pallas_call(
        functools.partial(_prenorm_kernel, tiles_per_seq=seq // tm),
        grid=(t // tm,),
        in_specs=[
            pl.BlockSpec((tm, d), lambda i: (i, 0)),
            pl.BlockSpec((1, d), lambda i: (0, 0)),
            pl.BlockSpec((nh, d), lambda i: (0, 0)),
            pl.BlockSpec((d, LANES), lambda i: (0, 0)),
            pl.BlockSpec((nh, 1), lambda i: (0, 0)),
            pl.BlockSpec((1, LANES), lambda i: (0, 0)),
        ],
        out_specs=[
            pl.BlockSpec((tm, d), lambda i: (i, 0)),
            pl.BlockSpec((nh, tm), lambda i: (0, i)),
            pl.BlockSpec((tm, LANES), lambda i: (i, 0)),
        ],
        out_shape=[
            jax.ShapeDtypeStruct((t, d), bf16),
            jax.ShapeDtypeStruct((nh, t), f32),
            jax.ShapeDtypeStruct((t, LANES), f32),
        ],
        scratch_shapes=[pltpu.VMEM((nh, 1), f32), pltpu.VMEM((1, LANES), f32)],
        compiler_params=_cparams(("arbitrary",)),
        name="prenorm",
    )(x2, g_mix.reshape(1, d), wft, wfp, bfc, bfr)


def _qkv_kernel(h_ref, w_ref, g_ref, o_ref, *, n_norm_blocks):
    j = pl.program_id(0)
    y = jnp.dot(h_ref[...], w_ref[...], preferred_element_type=f32)

    @pl.when(j < n_norm_blocks)
    def _():
        for hh in range(y.shape[1] // HEAD_DIM):
            sl = slice(hh * HEAD_DIM, (hh + 1) * HEAD_DIM)
            o_ref[:, sl] = _rms(y[:, sl], g_ref[:, sl]).astype(bf16)

    @pl.when(j >= n_norm_blocks)
    def _():
        o_ref[...] = y.astype(bf16)


def _qkv(h, w_qkv, g_q, g_k, att_width):
    t, d = h.shape
    n = w_qkv.shape[1]
    tm, tn = 1024, 512
    nheads = att_width // HEAD_DIM
    gain = jnp.concatenate([jnp.tile(g_q, nheads), jnp.tile(g_k, nheads),
                            jnp.ones((att_width,), f32)]).reshape(1, n)
    return pl.pallas_call(
        functools.partial(_qkv_kernel, n_norm_blocks=2 * att_width // tn),
        grid=(n // tn, t // tm),
        in_specs=[
            pl.BlockSpec((tm, d), lambda j, i: (i, 0)),
            pl.BlockSpec((d, tn), lambda j, i: (0, j)),
            pl.BlockSpec((1, tn), lambda j, i: (0, j)),
        ],
        out_specs=pl.BlockSpec((tm, tn), lambda j, i: (i, j)),
        out_shape=jax.ShapeDtypeStruct((t, n), bf16),
        compiler_params=_cparams(("arbitrary", "arbitrary")),
        name="qkv",
    )(h, w_qkv, gain)


def _conv_kernel(h_ref, wb_ref, wc_ref, wx_ref, wk_ref, o_ref, carry, *, tiles_per_seq):
    i = pl.program_id(1)
    h = h_ref[...]
    gb = jnp.dot(h, wb_ref[...], preferred_element_type=f32)
    gc = jnp.dot(h, wc_ref[...], preferred_element_type=f32)
    xc = jnp.dot(h, wx_ref[...], preferred_element_type=f32)
    u = gc * xc
    tm = u.shape[0]

    @pl.when(i % tiles_per_seq == 0)
    def _():
        carry[...] = jnp.zeros_like(carry)

    p2 = carry[SUBLANES - 2:SUBLANES - 1, :]
    p1 = carry[SUBLANES - 1:SUBLANES, :]
    row = lax.broadcasted_iota(i32, u.shape, 0)
    u1 = jnp.where(row == 0, p1, pltpu.roll(u, 1, 0))
    u2 = jnp.where(row == 0, p2, jnp.where(row == 1, p1, pltpu.roll(u, 2, 0)))
    y = wk_ref[0:1, :] * u2 + wk_ref[1:2, :] * u1 + wk_ref[2:3, :] * u
    o_ref[...] = gb * y
    carry[...] = u[tm - SUBLANES:tm, :]


def _conv(h, w_cv, w_conv, seq):
    t, d = h.shape
    width = w_cv.shape[1] // 3
    tm, tn = 1024, 256
    nb = width // tn
    return pl.pallas_call(
        functools.partial(_conv_kernel, tiles_per_seq=seq // tm),
        grid=(nb, t // tm),
        in_specs=[
            pl.BlockSpec((tm, d), lambda j, i: (i, 0)),
            pl.BlockSpec((d, tn), lambda j, i: (0, j)),
            pl.BlockSpec((d, tn), lambda j, i: (0, nb + j)),
            pl.BlockSpec((d, tn), lambda j, i: (0, 2 * nb + j)),
            pl.BlockSpec((3, tn), lambda j, i: (0, j)),
        ],
        out_specs=pl.BlockSpec((tm, tn), lambda j, i: (i, j)),
        out_shape=jax.ShapeDtypeStruct((t, width), f32),
        scratch_shapes=[pltpu.VMEM((SUBLANES, tn), f32)],
        compiler_params=_cparams(("arbitrary", "arbitrary")),
        name="conv",
    )(h, w_cv, w_cv, w_cv, w_conv)


def _attn_kernel(q_ref, k_ref, v_ref, crow_ref, ccol_ref, o_ref, *, tq, scale):
    hd = pl.program_id(1)
    qi = pl.program_id(2)
    q = q_ref[...]
    lane = lax.broadcasted_iota(i32, (tq, LANES), 1)
    cq = jnp.sum(jnp.where(lane == hd, ccol_ref[...], 0.0), axis=1, keepdims=True)

    def scores(j):
        rows = pl.ds(pl.multiple_of(j * tq, tq), tq)
        s = lax.dot_general(q, k_ref[rows, :], (((1,), (1,)), ((), ())),
                            preferred_element_type=f32) * scale
        return s + cq - crow_ref[pl.ds(j, 1), :], v_ref[rows, :]

    def update(s, vj, carry):
        m, l, acc = carry
        m_new = jnp.maximum(m, jnp.max(s, axis=1, keepdims=True))
        alpha = jnp.exp(m - m_new)
        p = jnp.exp(s - m_new)
        l = alpha * l + jnp.sum(p, axis=1, keepdims=True)
        acc = alpha * acc + jnp.dot(p.astype(bf16), vj, preferred_element_type=f32)
        return m_new, l, acc

    def body(j, carry):
        s, vj = scores(j)
        return update(s, vj, carry)

    init = (jnp.full((tq, 1), -jnp.inf, f32), jnp.zeros((tq, 1), f32), jnp.zeros((tq, HEAD_DIM), f32))
    carry = lax.fori_loop(0, qi, body, init)
    s, vj = scores(qi)
    r = lax.broadcasted_iota(i32, (tq, tq), 0)
    c = lax.broadcasted_iota(i32, (tq, tq), 1)
    s = jnp.where(c <= r, s, -jnp.inf)
    _, l, acc = update(s, vj, carry)
    o_ref[...] = acc / l


def _attention(qkv, c_row, c_col, batch, seq, nheads):
    t = qkv.shape[0]
    tq = 512
    nq = seq // tq
    crow4 = c_row.reshape(nheads, batch, nq, tq)
    return pl.pallas_call(
        functools.partial(_attn_kernel, tq=tq, scale=1.0 / np.sqrt(HEAD_DIM)),
        grid=(batch, nheads, nq),
        in_specs=[
            pl.BlockSpec((tq, HEAD_DIM), lambda b, h, qi: (b * nq + qi, h)),
            pl.BlockSpec((seq, HEAD_DIM), lambda b, h, qi: (b, nheads + h)),
            pl.BlockSpec((seq, HEAD_DIM), lambda b, h, qi: (b, 2 * nheads + h)),
            pl.BlockSpec((None, None, nq, tq), lambda b, h, qi: (h, b, 0, 0)),
            pl.BlockSpec((tq, LANES), lambda b, h, qi: (b * nq + qi, 0)),
        ],
        out_specs=pl.BlockSpec((tq, HEAD_DIM), lambda b, h, qi: (b * nq + qi, h)),
        out_shape=jax.ShapeDtypeStruct((t, nheads * HEAD_DIM), f32),
        compiler_params=_cparams(("arbitrary", "arbitrary", "arbitrary")),
        name="attention",
    )(qkv, qkv, qkv, crow4, c_col)


def _outproj_kernel(att_ref, conv_ref, ga_ref, gc_ref, w_ref, x_ref, o_ref, mixed):
    j = pl.program_id(1)
    wa = att_ref.shape[1]

    @pl.when(j == 0)
    def _():
        mixed[:, :wa] = _rms(att_ref[...], ga_ref[...]).astype(bf16)
        mixed[:, wa:] = _rms(conv_ref[...], gc_ref[...]).astype(bf16)

    o_ref[...] = x_ref[...] + jnp.dot(mixed[...], w_ref[...], preferred_element_type=f32)


def _outproj(att, conv, g_a, g_c, w_o, x2):
    t, d = x2.shape
    wa, wc = att.shape[1], conv.shape[1]
    tm, tn = 512, 512
    return pl.pallas_call(
        _outproj_kernel,
        grid=(t // tm, d // tn),
        in_specs=[
            pl.BlockSpec((tm, wa), lambda i, j: (i, 0)),
            pl.BlockSpec((tm, wc), lambda i, j: (i, 0)),
            pl.BlockSpec((1, wa), lambda i, j: (0, 0)),
            pl.BlockSpec((1, wc), lambda i, j: (0, 0)),
            pl.BlockSpec((wa + wc, tn), lambda i, j: (0, j)),
            pl.BlockSpec((tm, tn), lambda i, j: (i, j)),
        ],
        out_specs=pl.BlockSpec((tm, tn), lambda i, j: (i, j)),
        out_shape=jax.ShapeDtypeStruct((t, d), f32),
        scratch_shapes=[pltpu.VMEM((tm, wa + wc), bf16)],
        compiler_params=_cparams(("arbitrary", "arbitrary")),
        name="outproj",
    )(att, conv, g_a.reshape(1, wa), g_c.reshape(1, wc), w_o, x2)


def _ffn_norm_kernel(x_ref, g_ref, wrt_ref, br_ref, h_ref, lg_ref):
    h = _rms(x_ref[...], g_ref[...])
    h_ref[...] = h
    lg_ref[...] = lax.dot_general(wrt_ref[...], h, (((1,), (1,)), ((), ())),
                                  precision=lax.Precision.HIGHEST,
                                  preferred_element_type=f32) + br_ref[...]


def _ffn_norm(x1, g_ffn, w_router, b_router):
    t, d = x1.shape
    ne = w_router.shape[1]
    tm = 512
    return pl.pallas_call(
        _ffn_norm_kernel,
        grid=(t // tm,),
        in_specs=[
            pl.BlockSpec((tm, d), lambda i: (i, 0)),
            pl.BlockSpec((1, d), lambda i: (0, 0)),
            pl.BlockSpec((ne, d), lambda i: (0, 0)),
            pl.BlockSpec((ne, 1), lambda i: (0, 0)),
        ],
        out_specs=[
            pl.BlockSpec((tm, d), lambda i: (i, 0)),
            pl.BlockSpec((ne, tm), lambda i: (0, i)),
        ],
        out_shape=[jax.ShapeDtypeStruct((t, d), f32), jax.ShapeDtypeStruct((ne, t), f32)],
        compiler_params=_cparams(("arbitrary",)),
        name="ffn_norm",
    )(x1, g_ffn.reshape(1, d), w_router.T, b_router.reshape(ne, 1))


def _cumsum_sublanes(x):
    n = x.shape[0]
    row = lax.broadcasted_iota(i32, x.shape, 0)
    d = 1
    while d < n:
        x = x + jnp.where(row >= d, pltpu.roll(x, d, 0), 0.0)
        d *= 2
    return x


def _stack_rows(rows, n):
    width = rows[0].shape[1]
    sub = lax.broadcasted_iota(i32, (n, width), 0)
    out = jnp.zeros((n, width), rows[0].dtype)
    for k, r in enumerate(rows):
        out = jnp.where(sub == k, r, out)
    return out


def _route_kernel(lg_ref, pos_ref, gcol_ref, meta_ref, esel, rsel, *, chunk):
    ne, t = lg_ref.shape
    e_iota = lax.broadcasted_iota(i32, (ne, chunk), 0).astype(f32)
    a = lax.broadcasted_iota(i32, (chunk, chunk), 0)
    b = lax.broadcasted_iota(i32, (chunk, chunk), 1)
    before = (a < b).astype(bf16)
    counts = jnp.zeros((ne, 1), f32)
    for c in range(t // chunk):
        sl = slice(c * chunk, (c + 1) * chunk)
        vals = lg_ref[:, sl]
        tops, idxs, hots = [], [], []
        for _ in range(TOP_K):
            m = jnp.max(vals, axis=0, keepdims=True)
            idx = jnp.min(jnp.where(vals == m, e_iota, ne), axis=0, keepdims=True)
            hot = e_iota == idx
            vals = jnp.where(hot, -jnp.inf, vals)
            tops.append(m)
            idxs.append(idx)
            hots.append(hot)
        ex = [jnp.exp(v - tops[0]) for v in tops]
        den = ex[0] + ex[1] + ex[2] + ex[3]
        gates = [e / den for e in ex]
        member = sum(h.astype(f32) for h in hots)
        rank = jnp.dot(member.astype(bf16), before, preferred_element_type=f32) + counts
        counts = counts + jnp.sum(member, axis=1, keepdims=True)
        for k in range(TOP_K):
            rsel[k:k + 1, sl] = jnp.sum(jnp.where(hots[k], rank, 0.0), axis=0, keepdims=True)
            esel[k:k + 1, sl] = idxs[k]
        g8 = _stack_rows(gates, SUBLANES)
        gpad = jnp.concatenate([g8, jnp.zeros((LANES - SUBLANES, chunk), f32)], axis=0)
        gcol_ref[sl, :] = gpad.T

    cnt = jnp.broadcast_to(counts, (ne, LANES))
    pcnt = jnp.ceil(cnt * (1.0 / ROW_PAD)) * ROW_PAD
    pend = _cumsum_sublanes(pcnt)
    pstart = pend - pcnt
    npass = jnp.floor((pcnt + (EXPERT_CAP - ROW_PAD)) / EXPERT_CAP + 0.5 * ROW_PAD / EXPERT_CAP)
    iend = _cumsum_sublanes(npass)
    istart = iend - npass
    n_items = jnp.max(iend, axis=0, keepdims=True)

    pstart_col = pstart[:, 0:1]
    for c in range(t // chunk):
        sl = slice(c * chunk, (c + 1) * chunk)
        for k in range(TOP_K):
            hot = e_iota == esel[k:k + 1, sl]
            ps = jnp.sum(jnp.where(hot, pstart_col, 0.0), axis=0, keepdims=True)
            pos_ref[k:k + 1, sl] = (rsel[k:k + 1, sl] + ps).astype(i32)

    e_sub = lax.broadcasted_iota(i32, (ne, LANES), 0)
    lane = lax.broadcasted_iota(i32, (ne, LANES), 1)
    lane_f = lane.astype(f32)
    item = lane_f[0:1, :]
    e_of = jnp.minimum(jnp.sum((iend <= lane_f).astype(f32), axis=0, keepdims=True), ne - 1.0)
    hot = e_sub.astype(f32) == e_of
    pick = lambda v: jnp.sum(jnp.where(hot, v, 0.0), axis=0, keepdims=True)
    p_i = item - pick(istart)
    row0 = pick(pstart) + p_i * EXPERT_CAP
    nrows = jnp.clip(pick(pcnt) - p_i * EXPERT_CAP, 0.0, float(EXPERT_CAP))
    nsub = nrows * (1.0 / ROW_PAD)
    diag = e_sub == lane
    zrow = jnp.sum(jnp.where(diag, pend - ROW_PAD, 0.0), axis=0, keepdims=True)
    zval = jnp.sum(jnp.where(diag, (cnt > 0).astype(f32), 0.0), axis=0, keepdims=True)
    total = jnp.max(pend, axis=0, keepdims=True)
    rows = [e_of, row0, nsub, n_items, zrow, zval, total]
    meta_ref[...] = _stack_rows([r.astype(i32) for r in rows], SUBLANES)


def _route(logits_t):
    ne, t = logits_t.shape
    chunk = 512
    return pl.pallas_call(
        functools.partial(_route_kernel, chunk=chunk),
        out_shape=[
            jax.ShapeDtypeStruct((TOP_K, t), i32),
            jax.ShapeDtypeStruct((t, LANES), f32),
            jax.ShapeDtypeStruct((SUBLANES, LANES), i32),
        ],
        scratch_shapes=[pltpu.VMEM((TOP_K, t), f32), pltpu.VMEM((TOP_K, t), f32)],
        compiler_params=pltpu.CompilerParams(vmem_limit_bytes=VMEM_LIMIT_BYTES),
        name="route",
    )(logits_t)


_M_EXPERT, _M_ROW0, _M_NSUB, _M_NITEMS, _M_ZROW, _M_ZVALID, _M_TOTAL = range(7)


def _zero_tail(meta, zbuf, dst_ref, sem):
    total = pl.multiple_of(meta[_M_TOTAL, 0], ROW_PAD)
    n = (dst_ref.shape[0] - total) // ROW_PAD

    def copy(r):
        return pltpu.make_async_copy(zbuf, dst_ref.at[pl.ds(total + r * ROW_PAD, ROW_PAD), :], sem)

    def start(r, c):
        copy(r).start()
        return c

    def wait(r, c):
        copy(r).wait()
        return c

    lax.fori_loop(0, n, start, 0)
    lax.fori_loop(0, n, wait, 0)


def _dispatch_kernel(meta, pos_ref, h_ref, xs_ref, zbuf, sem, zsem, *, tmd, ne):
    i = pl.program_id(0)

    def zero_copy(e):
        start = pl.multiple_of(meta[_M_ZROW, e], ROW_PAD)
        return pltpu.make_async_copy(zbuf, xs_ref.at[pl.ds(start, ROW_PAD), :], zsem)

    @pl.when(i == 0)
    def _():
        zbuf[...] = jnp.zeros_like(zbuf)

        def zstart(e, c):
            @pl.when(meta[_M_ZVALID, e] > 0)
            def _():
                zero_copy(e).start()
            return c

        def zwait(e, c):
            @pl.when(meta[_M_ZVALID, e] > 0)
            def _():
                zero_copy(e).wait()
            return c

        lax.fori_loop(0, ne, zstart, 0)
        lax.fori_loop(0, ne, zwait, 0)
        _zero_tail(meta, zbuf, xs_ref, zsem)

    def row_copy(r, k):
        return pltpu.make_async_copy(h_ref.at[pl.ds(r, 1), :],
                                     xs_ref.at[pl.ds(pos_ref[k, r], 1), :], sem)

    def issue(r, c):
        for k in range(TOP_K):
            row_copy(r, k).start()
        return c

    def drain(r, c):
        for k in range(TOP_K):
            row_copy(r, k).wait()
        return c

    lax.fori_loop(0, tmd, issue, 0)
    lax.fori_loop(0, tmd, drain, 0)


def _dispatch(meta, pos_t, h2, n_rows, ne):
    t, d = h2.shape
    tmd = 256
    return pl.pallas_call(
        functools.partial(_dispatch_kernel, tmd=tmd, ne=ne),
        grid_spec=pltpu.PrefetchScalarGridSpec(
            num_scalar_prefetch=1,
            grid=(t // tmd,),
            in_specs=[
                pl.BlockSpec((TOP_K, tmd), lambda i, m: (0, i), memory_space=pltpu.SMEM),
                pl.BlockSpec((tmd, d), lambda i, m: (i, 0)),
            ],
            out_specs=pl.BlockSpec(memory_space=pl.ANY),
            scratch_shapes=[pltpu.VMEM((ROW_PAD, d), f32), pltpu.SemaphoreType.DMA, pltpu.SemaphoreType.DMA],
        ),
        out_shape=jax.ShapeDtypeStruct((n_rows, d), f32),
        compiler_params=_cparams(("arbitrary",)),
        name="dispatch",
    )(meta, pos_t, h2)


def _expert_kernel(meta, xs_ref, wg_ref, wu_ref, bg_ref, bu_ref, wd_ref, bd_ref, ys_ref,
                   x_scr, h_scr, wgu_scr, wd_scr, stage, y_scr, ld_sem, st_sem, *, nf, nd, tf, td):
    it = pl.program_id(0)
    s = pl.program_id(1)

    @pl.when((it == 0) & (s == 0))
    def _():
        stage[0] = jnp.zeros(stage.shape[1:], f32)
        _zero_tail(meta, stage.at[0], ys_ref, ld_sem.at[0])

    @pl.when(it < meta[_M_NITEMS, 0])
    def _():
        row0 = pl.multiple_of(meta[_M_ROW0, it], ROW_PAD)
        nsub = meta[_M_NSUB, it]

        def rows_of(r):
            return pl.ds(pl.multiple_of(r * ROW_PAD, ROW_PAD), ROW_PAD)

        def load_copy(r, slot):
            return pltpu.make_async_copy(xs_ref.at[pl.ds(row0 + r * ROW_PAD, ROW_PAD), :],
                                         stage.at[slot], ld_sem.at[slot])

        @pl.when(s == 0)
        def _():
            load_copy(0, 0).start()

            def body(r, c):
                slot = r % 2
                load_copy(r, slot).wait()

                @pl.when(r + 1 < nsub)
                def _():
                    load_copy(r + 1, 1 - slot).start()

                x_scr[rows_of(r), :] = stage[slot].astype(bf16)
                return c

            lax.fori_loop(0, nsub, body, 0)

        def for_row_tiles(fn):
            per = COMPUTE_ROWS // ROW_PAD

            def body(r, c):
                fn(pl.multiple_of(r * COMPUTE_ROWS, COMPUTE_ROWS), COMPUTE_ROWS)
                return c

            lax.fori_loop(0, nsub // per, body, 0)
            for tail in range(1, per):
                @pl.when(nsub % per == tail)
                def _():
                    fn(pl.multiple_of((nsub // per) * COMPUTE_ROWS, COMPUTE_ROWS), tail * ROW_PAD)

        @pl.when(s < nf)
        def _():
            wgu_scr[:, :tf] = wg_ref[...].astype(bf16)
            wgu_scr[:, tf:] = wu_ref[...].astype(bf16)

            def up_tile(first, n):
                rows = pl.ds(first, n)
                gu = jnp.dot(x_scr[rows, :], wgu_scr[...], preferred_element_type=f32)
                g = jnp.minimum(gu[:, :tf] + bg_ref[...], SWIGLU_LIMIT)
                u = jnp.clip(gu[:, tf:] + bu_ref[...], -SWIGLU_LIMIT, SWIGLU_LIMIT)
                act = (u + 1.0) * (g * jax.nn.sigmoid(SWIGLU_ALPHA * g))
                h_scr[s, rows, :] = act.astype(bf16)

            for_row_tiles(up_tile)

        @pl.when(s >= nf)
        def _():
            dcol = s - nf
            slot = dcol % 2
            wd_scr[...] = wd_ref[...].astype(bf16)

            def store_copy(first, sl):
                dst = ys_ref.at[pl.ds(row0 + first, ROW_PAD), pl.ds(pl.multiple_of(dcol * td, td), td)]
                return pltpu.make_async_copy(y_scr.at[sl, pl.ds(first, ROW_PAD), :], dst, st_sem.at[sl])

            def wait_slot(sl):
                def w(r, c):
                    store_copy(pl.multiple_of(r * ROW_PAD, ROW_PAD), sl).wait()
                    return c
                lax.fori_loop(0, nsub, w, 0)

            @pl.when(dcol >= 2)
            def _():
                wait_slot(slot)

            def down_tile(first, n):
                rows = pl.ds(first, n)
                hidden = jnp.concatenate([h_scr[cf, rows, :] for cf in range(nf)], axis=1)
                y_scr[slot, rows, :] = bd_ref[...] + jnp.dot(hidden, wd_scr[...], preferred_element_type=f32)
                for part in range(n // ROW_PAD):
                    store_copy(first + part * ROW_PAD, slot).start()

            for_row_tiles(down_tile)

            @pl.when(dcol == nd - 1)
            def _():
                wait_slot(slot)
                wait_slot(1 - slot)


def _experts(meta, xs, w_gate_up, b_gate_up, w_down, b_down):
    n_rows, d = xs.shape
    ne, _, f2 = w_gate_up.shape
    f = f2 // 2
    tf, td = 256, 512
    nf, nd = f // tf, d // td
    max_items = (n_rows + ne * (EXPERT_CAP - ROW_PAD)) // EXPERT_CAP
    assert max_items <= MAX_ITEMS and nd >= 2
    bgu = b_gate_up.reshape(ne, 1, f2)
    bd = b_down.reshape(ne, 1, d)

    def item(i, m):
        return jnp.minimum(i, m[_M_NITEMS, 0] - 1)

    def up_block(i, s, m):
        return jnp.where(i < m[_M_NITEMS, 0], jnp.minimum(s, nf - 1), nf - 1)

    def down_block(i, s, m):
        return jnp.where(i < m[_M_NITEMS, 0], jnp.clip(s - nf, 0, nd - 1), nd - 1)

    return pl.pallas_call(
        functools.partial(_expert_kernel, nf=nf, nd=nd, tf=tf, td=td),
        grid_spec=pltpu.PrefetchScalarGridSpec(
            num_scalar_prefetch=1,
            grid=(max_items, nf + nd),
            in_specs=[
                pl.BlockSpec(memory_space=pl.ANY),
                pl.BlockSpec((None, d, tf), lambda i, s, m: (m[_M_EXPERT, item(i, m)], 0, up_block(i, s, m))),
                pl.BlockSpec((None, d, tf), lambda i, s, m: (m[_M_EXPERT, item(i, m)], 0, nf + up_block(i, s, m))),
                pl.BlockSpec((None, 1, tf), lambda i, s, m: (m[_M_EXPERT, item(i, m)], 0, up_block(i, s, m))),
                pl.BlockSpec((None, 1, tf), lambda i, s, m: (m[_M_EXPERT, item(i, m)], 0, nf + up_block(i, s, m))),
                pl.BlockSpec((None, f, td), lambda i, s, m: (m[_M_EXPERT, item(i, m)], 0, down_block(i, s, m))),
                pl.BlockSpec((None, 1, td), lambda i, s, m: (m[_M_EXPERT, item(i, m)], 0, down_block(i, s, m))),
            ],
            out_specs=pl.BlockSpec(memory_space=pl.ANY),
            scratch_shapes=[
                pltpu.VMEM((EXPERT_CAP, d), bf16),
                pltpu.VMEM((nf, EXPERT_CAP, tf), bf16),
                pltpu.VMEM((d, 2 * tf), bf16),
                pltpu.VMEM((f, td), bf16),
                pltpu.VMEM((2, ROW_PAD, d), f32),
                pltpu.VMEM((2, EXPERT_CAP, td), f32),
                pltpu.SemaphoreType.DMA((2,)),
                pltpu.SemaphoreType.DMA((2,)),
            ],
        ),
        out_shape=jax.ShapeDtypeStruct((n_rows, d), f32),
        compiler_params=_cparams(("arbitrary", "arbitrary")),
        name="experts",
    )(meta, xs, w_gate_up, w_gate_up, bgu, bgu, w_down, bd)


def _combine_kernel(pos_cur, pos_nxt, gcol_ref, x_ref, ys_ref, o_ref, gbuf, sem, *, tmc, n_tiles):
    i = pl.program_id(0)
    slot = i % 2

    def row_copy(pos_ref, sl, r, k):
        return pltpu.make_async_copy(ys_ref.at[pl.ds(pos_ref[k, r], 1), :],
                                     gbuf.at[sl, k, pl.ds(r, 1), :], sem.at[sl])

    def issue(pos_ref, sl):
        def body(r, c):
            for k in range(TOP_K):
                row_copy(pos_ref, sl, r, k).start()
            return c
        lax.fori_loop(0, tmc, body, 0)

    @pl.when(i == 0)
    def _():
        issue(pos_cur, 0)

    @pl.when(i + 1 < n_tiles)
    def _():
        issue(pos_nxt, 1 - slot)

    def drain(r, c):
        for k in range(TOP_K):
            row_copy(pos_cur, slot, r, k).wait()
        return c

    lax.fori_loop(0, tmc, drain, 0)
    acc = x_ref[...]
    for k in range(TOP_K):
        acc = acc + gcol_ref[:, k:k + 1] * gbuf[slot, k]
    o_ref[...] = acc


def _combine(pos_t, gcol, x1, ys):
    t, d = x1.shape
    tmc = 128
    n_tiles = t // tmc
    return pl.pallas_call(
        functools.partial(_combine_kernel, tmc=tmc, n_tiles=n_tiles),
        grid=(n_tiles,),
        in_specs=[
            pl.BlockSpec((TOP_K, tmc), lambda i: (0, i), memory_space=pltpu.SMEM),
            pl.BlockSpec((TOP_K, tmc), lambda i: (0, jnp.minimum(i + 1, n_tiles - 1)), memory_space=pltpu.SMEM),
            pl.BlockSpec((tmc, LANES), lambda i: (i, 0)),
            pl.BlockSpec((tmc, d), lambda i: (i, 0)),
            pl.BlockSpec(memory_space=pl.ANY),
        ],
        out_specs=pl.BlockSpec((tmc, d), lambda i: (i, 0)),
        out_shape=jax.ShapeDtypeStruct((t, d), f32),
        scratch_shapes=[pltpu.VMEM((2, TOP_K, tmc, d), f32), pltpu.SemaphoreType.DMA((2,))],
        compiler_params=_cparams(("arbitrary",)),
        name="combine",
    )(pos_t, pos_t, gcol, x1, ys)


def _layer(x2, batch, seq, g_mix, w_in, b_forget, g_q, g_k, w_conv, g_attn_out, g_conv_out, w_out,
           g_ffn, w_router, b_router, w_gate_up, b_gate_up, w_down, b_down):
    t, d = x2.shape
    nheads = b_forget.shape[0]
    att_width = nheads * HEAD_DIM
    ne = w_router.shape[1]
    c0 = 3 * att_width
    w_qkv = w_in[:, :c0].astype(bf16)
    w_f = w_in[:, c0:c0 + nheads]
    w_cv = w_in[:, c0 + nheads:].astype(bf16)

    h, c_row, c_col = _prenorm(x2, g_mix, w_f, b_forget, seq)
    qkv = _qkv(h, w_qkv, g_q, g_k, att_width)
    conv = _conv(h, w_cv, w_conv, seq)
    att = _attention(qkv, c_row, c_col, batch, seq, nheads)
    x1 = _outproj(att, conv, g_attn_out, g_conv_out, w_out.astype(bf16), x2)

    h2, logits_t = _ffn_norm(x1, g_ffn, w_router, b_router)
    pos_t, gcol, meta = _route(logits_t)
    n_rows = t * TOP_K + ne * ROW_PAD
    xs = _dispatch(meta, pos_t, h2, n_rows, ne)
    ys = _experts(meta, xs, w_gate_up, b_gate_up, w_down, b_down)
    return _combine(pos_t, gcol, x1, ys)


def kernel(x, g_mix, w_in, b_forget, g_q, g_k, w_conv, g_attn_out, g_conv_out, w_out, g_ffn, w_router,
           b_router, w_gate_up, b_gate_up, w_down, b_down):
    b, s, d = x.shape
    x2 = x.reshape(b * s, d)
    for l in range(g_mix.shape[0]):
        x2 = _layer(x2, b, s, g_mix[l], w_in[l], b_forget[l], g_q[l], g_k[l], w_conv[l], g_attn_out[l],
                    g_conv_out[l], w_out[l], g_ffn[l], w_router[l], b_router[l], w_gate_up[l],
                    b_gate_up[l], w_down[l], b_down[l])
    return x2.reshape(b, s, d)
```

```python
import functools

import jax
import jax.numpy as jnp
import numpy as np
from jax import lax
from jax.experimental import pallas as pl
from jax.experimental.pallas import tpu as pltpu

f32 = jnp.float32
bf16 = jnp.bfloat16
i32 = jnp.int32

HEAD_DIM = 128
LANES = 128
SUBLANES = 8
TOP_K = 4
RMS_EPS = 1e-6
SWIGLU_ALPHA = 1.702
SWIGLU_LIMIT = 7.0
VMEM_LIMIT_BYTES = 56 * 1024 * 1024

ROW_PAD = 128
EXPERT_CAP = 1152
COMPUTE_ROWS = 512
MAX_ITEMS = 128
ATTN_ROW_GROUPS = 1


def _cparams(sem):
    return pltpu.CompilerParams(dimension_semantics=sem, vmem_limit_bytes=VMEM_LIMIT_BYTES)


def _log_sigmoid(z):
    return jnp.minimum(z, 0.0) - jnp.log1p(jnp.exp(-jnp.abs(z)))


def _split3(a):
    hi = a.astype(bf16)
    r1 = a - hi.astype(f32)
    mid = r1.astype(bf16)
    lo = (r1 - mid.astype(f32)).astype(bf16)
    return hi, mid, lo


def _rms(y, g):
    ms = jnp.mean(y * y, axis=-1, keepdims=True)
    return y * lax.rsqrt(ms + RMS_EPS) * g


def _prenorm_kernel(x_ref, g_ref, wf_ref, bf_ref, h_ref, c_ref, carry, *, tiles_per_seq):
    i = pl.program_id(0)
    hb = _rms(x_ref[...], g_ref[...]).astype(bf16)
    h_ref[...] = hb
    logf = _log_sigmoid(jnp.dot(hb, wf_ref[...], preferred_element_type=f32) + bf_ref[...])
    tm = hb.shape[0]
    a = lax.broadcasted_iota(i32, (tm, tm), 0)
    b = lax.broadcasted_iota(i32, (tm, tm), 1)
    lower = (b <= a).astype(bf16)
    c = sum(jnp.dot(lower, p, preferred_element_type=f32) for p in _split3(logf))

    @pl.when(i % tiles_per_seq == 0)
    def _():
        carry[...] = jnp.zeros_like(carry)

    c = c + carry[...]
    c_ref[...] = c
    carry[...] = c[tm - 1:tm, :]


def _prenorm(x2, g_mix, w_f, b_f, seq):
    t, d = x2.shape
    nh = w_f.shape[1]
    tm = 512
    wfp = jnp.pad(w_f, ((0, 0), (0, LANES - nh))).astype(bf16)
    bfr = jnp.pad(b_f, (0, LANES - nh)).reshape(1, LANES)
    return pl.pallas_call(
        functools.partial(_prenorm_kernel, tiles_per_seq=seq // tm),
        grid=(t // tm,),
        in_specs=[
            pl.BlockSpec((tm, d), lambda i: (i, 0)),
            pl.BlockSpec((1, d), lambda i: (0, 0)),
            pl.BlockSpec((d, LANES), lambda i: (0, 0)),
            pl.BlockSpec((1, LANES), lambda i: (0, 0)),
        ],
        out_specs=[
            pl.BlockSpec((tm, d), lambda i: (i, 0)),
            pl.BlockSpec((tm, LANES), lambda i: (i, 0)),
        ],
        out_shape=[
            jax.ShapeDtypeStruct((t, d), bf16),
            jax.ShapeDtypeStruct((t, LANES), f32),
        ],
        scratch_shapes=[pltpu.VMEM((1, LANES), f32)],
        compiler_params=_cparams(("arbitrary",)),
        name="prenorm",
    )(x2, g_mix.reshape(1, d), wfp, bfr)


def _qkv_kernel(h_ref, w_ref, g_ref, post_ref, o_ref, wb, *, n_norm_blocks):
    j = pl.program_id(0)
    i = pl.program_id(1)

    @pl.when(i == 0)
    def _():
        wb[...] = w_ref[...].astype(bf16)

    y = jnp.dot(h_ref[...], wb[...], preferred_element_type=f32)

    @pl.when(j < n_norm_blocks)
    def _():
        for hh in range(y.shape[1] // HEAD_DIM):
            sl = slice(hh * HEAD_DIM, (hh + 1) * HEAD_DIM)
            o_ref[:, sl] = (_rms(y[:, sl], g_ref[:, sl]) * post_ref[:, sl]).astype(bf16)

    @pl.when(j >= n_norm_blocks)
    def _():
        o_ref[...] = y.astype(bf16)


def _qkv(h, w_in, g_q, g_k, att_width):
    t, d = h.shape
    n = 3 * att_width
    tm, tn = 1024, 512
    nheads = att_width // HEAD_DIM
    ones = jnp.ones((att_width,), f32)
    gain = jnp.concatenate([jnp.tile(g_q, nheads), jnp.tile(g_k, nheads), ones]).reshape(1, n)
    post = jnp.concatenate([ones * (1.0 / np.sqrt(HEAD_DIM)), ones, ones]).reshape(1, n)
    return pl.pallas_call(
        functools.partial(_qkv_kernel, n_norm_blocks=2 * att_width // tn),
        grid=(n // tn, t // tm),
        in_specs=[
            pl.BlockSpec((tm, d), lambda j, i: (i, 0)),
            pl.BlockSpec((d, tn), lambda j, i: (0, j)),
            pl.BlockSpec((1, tn), lambda j, i: (0, j)),
            pl.BlockSpec((1, tn), lambda j, i: (0, j)),
        ],
        out_specs=pl.BlockSpec((tm, tn), lambda j, i: (i, j)),
        out_shape=jax.ShapeDtypeStruct((t, n), bf16),
        scratch_shapes=[pltpu.VMEM((d, tn), bf16)],
        compiler_params=_cparams(("arbitrary", "arbitrary")),
        name="qkv",
    )(h, w_in, gain, post)


def _conv_kernel(h_ref, wb_ref, wc_ref, wx_ref, wk_ref, o_ref, carry, *, tiles_per_seq):
    i = pl.program_id(1)
    h = h_ref[...]
    gb = jnp.dot(h, wb_ref[...], preferred_element_type=f32)
    gc = jnp.dot(h, wc_ref[...], preferred_element_type=f32)
    xc = jnp.dot(h, wx_ref[...], preferred_element_type=f32)
    u = gc * xc
    tm = u.shape[0]

    @pl.when(i % tiles_per_seq == 0)
    def _():
        carry[...] = jnp.zeros_like(carry)

    p2 = carry[SUBLANES - 2:SUBLANES - 1, :]
    p1 = carry[SUBLANES - 1:SUBLANES, :]
    row = lax.broadcasted_iota(i32, u.shape, 0)
    u1 = jnp.where(row == 0, p1, pltpu.roll(u, 1, 0))
    u2 = jnp.where(row == 0, p2, jnp.where(row == 1, p1, pltpu.roll(u, 2, 0)))
    y = wk_ref[0:1, :] * u2 + wk_ref[1:2, :] * u1 + wk_ref[2:3, :] * u
    o_ref[...] = gb * y
    carry[...] = u[tm - SUBLANES:tm, :]


def _conv(h, w_cv, w_conv, seq):
    t, d = h.shape
    width = w_cv.shape[1] // 3
    tm, tn = 1024, 256
    nb = width // tn
    return pl.pallas_call(
        functools.partial(_conv_kernel, tiles_per_seq=seq // tm),
        grid=(nb, t // tm),
        in_specs=[
            pl.BlockSpec((tm, d), lambda j, i: (i, 0)),
            pl.BlockSpec((d, tn), lambda j, i: (0, j)),
            pl.BlockSpec((d, tn), lambda j, i: (0, nb + j)),
            pl.BlockSpec((d, tn), lambda j, i: (0, 2 * nb + j)),
            pl.BlockSpec((3, tn), lambda j, i: (0, j)),
        ],
        out_specs=pl.BlockSpec((tm, tn), lambda j, i: (i, j)),
        out_shape=jax.ShapeDtypeStruct((t, width), f32),
        scratch_shapes=[pltpu.VMEM((SUBLANES, tn), f32)],
        compiler_params=_cparams(("arbitrary", "arbitrary")),
        name="conv",
    )(h, w_cv, w_cv, w_cv, w_conv)


def _bias_lanes(c, ones_first):
    n = c.shape[0]
    lane = lax.broadcasted_iota(i32, (n, LANES), 1)
    out = jnp.zeros((n, LANES), f32)
    base_c, base_1 = (3, 0) if ones_first else (0, 3)
    for k, piece in enumerate(_split3(c)):
        out = jnp.where(lane == base_c + k, piece.astype(f32), out)
        out = jnp.where(lane == base_1 + k, 1.0, out)
    return out.astype(bf16)


def _attn_kernel(q_ref, k_ref, v_ref, cq_ref, ck_ref, o_ref, k_aug, *, tq):
    hd = pl.program_id(1)
    qi = pl.program_id(2)

    def head_column(c_ref):
        lane = lax.broadcasted_iota(i32, c_ref.shape, 1)
        return jnp.sum(jnp.where(lane == hd, c_ref[...], 0.0), axis=1, keepdims=True)

    @pl.when(qi == 0)
    def _():
        k_aug[:, :HEAD_DIM] = k_ref[...]
        k_aug[:, HEAD_DIM:] = _bias_lanes(-head_column(ck_ref), ones_first=True)

    q = jnp.concatenate([q_ref[...], _bias_lanes(head_column(cq_ref), ones_first=False)], axis=1)
    rq = tq // ATTN_ROW_GROUPS
    qs = [q[g * rq:(g + 1) * rq, :] for g in range(ATTN_ROW_GROUPS)]

    def update(qg, kj, vj, state, mask=None):
        m, l, acc = state
        s = lax.dot_general(qg, kj, (((1,), (1,)), ((), ())), preferred_element_type=f32)
        if mask is not None:
            s = jnp.where(mask, s, -jnp.inf)
        m_new = jnp.maximum(m, jnp.max(s, axis=1, keepdims=True))
        alpha = jnp.exp(m - m_new)
        p = jnp.exp(s - m_new)
        l = alpha * l + jnp.sum(p, axis=1, keepdims=True)
        acc = alpha * acc + jnp.dot(p.astype(bf16), vj, preferred_element_type=f32)
        return m_new, l, acc

    def body(j, states):
        rows = pl.ds(pl.multiple_of(j * tq, tq), tq)
        kj, vj = k_aug[rows, :], v_ref[rows, :]
        return tuple(update(qs[g], kj, vj, states[g]) for g in range(ATTN_ROW_GROUPS))

    init = (jnp.full((rq, 1), -jnp.inf, f32), jnp.zeros((rq, 1), f32), jnp.zeros((rq, HEAD_DIM), f32))
    states = lax.fori_loop(0, qi, body, (init,) * ATTN_ROW_GROUPS)
    rows = pl.ds(pl.multiple_of(qi * tq, tq), tq)
    kj, vj = k_aug[rows, :], v_ref[rows, :]
    r = lax.broadcasted_iota(i32, (rq, tq), 0)
    c = lax.broadcasted_iota(i32, (rq, tq), 1)
    for g in range(ATTN_ROW_GROUPS):
        _, l, acc = update(qs[g], kj, vj, states[g], mask=c <= r + g * rq)
        o_ref[g * rq:(g + 1) * rq, :] = acc / l


def _attention(qkv, c, batch, seq, nheads):
    t = qkv.shape[0]
    tq = 512
    nq = seq // tq
    return pl.pallas_call(
        functools.partial(_attn_kernel, tq=tq),
        grid=(batch, nheads, nq),
        in_specs=[
            pl.BlockSpec((tq, HEAD_DIM), lambda b, h, qi: (b * nq + qi, h)),
            pl.BlockSpec((seq, HEAD_DIM), lambda b, h, qi: (b, nheads + h)),
            pl.BlockSpec((seq, HEAD_DIM), lambda b, h, qi: (b, 2 * nheads + h)),
            pl.BlockSpec((tq, LANES), lambda b, h, qi: (b * nq + qi, 0)),
            pl.BlockSpec((seq, LANES), lambda b, h, qi: (b, 0)),
        ],
        out_specs=pl.BlockSpec((tq, HEAD_DIM), lambda b, h, qi: (b * nq + qi, h)),
        out_shape=jax.ShapeDtypeStruct((t, nheads * HEAD_DIM), f32),
        scratch_shapes=[pltpu.VMEM((seq, 2 * HEAD_DIM), bf16)],
        compiler_params=_cparams(("arbitrary", "arbitrary", "arbitrary")),
        name="attention",
    )(qkv, qkv, qkv, c, c)


def _outproj_kernel(att_ref, conv_ref, ga_ref, gc_ref, w_ref, x_ref, o_ref, mixed):
    j = pl.program_id(1)
    wa = att_ref.shape[1]

    @pl.when(j == 0)
    def _():
        mixed[:, :wa] = _rms(att_ref[...], ga_ref[...]).astype(bf16)
        mixed[:, wa:] = _rms(conv_ref[...], gc_ref[...]).astype(bf16)

    o_ref[...] = x_ref[...] + jnp.dot(mixed[...], w_ref[...], preferred_element_type=f32)


def _outproj(att, conv, g_a, g_c, w_o, x2):
    t, d = x2.shape
    wa, wc = att.shape[1], conv.shape[1]
    tm, tn = 512, 512
    return pl.pallas_call(
        _outproj_kernel,
        grid=(t // tm, d // tn),
        in_specs=[
            pl.BlockSpec((tm, wa), lambda i, j: (i, 0)),
            pl.BlockSpec((tm, wc), lambda i, j: (i, 0)),
            pl.BlockSpec((1, wa), lambda i, j: (0, 0)),
            pl.BlockSpec((1, wc), lambda i, j: (0, 0)),
            pl.BlockSpec((wa + wc, tn), lambda i, j: (0, j)),
            pl.BlockSpec((tm, tn), lambda i, j: (i, j)),
        ],
        out_specs=pl.BlockSpec((tm, tn), lambda i, j: (i, j)),
        out_shape=jax.ShapeDtypeStruct((t, d), f32),
        scratch_shapes=[pltpu.VMEM((tm, wa + wc), bf16)],
        compiler_params=_cparams(("arbitrary", "arbitrary")),
        name="outproj",
    )(att, conv, g_a.reshape(1, wa), g_c.reshape(1, wc), w_o, x2)


def _ffn_norm_kernel(x_ref, g_ref, wrt_ref, br_ref, h_ref, lg_ref):
    h = _rms(x_ref[...], g_ref[...])
    h_ref[...] = h
    lg_ref[...] = lax.dot_general(wrt_ref[...], h, (((1,), (1,)), ((), ())),
                                  precision=lax.Precision.HIGHEST,
                                  preferred_element_type=f32) + br_ref[...]


def _ffn_norm(x1, g_ffn, w_router, b_router):
    t, d = x1.shape
    ne = w_router.shape[1]
    tm = 512
    return pl.pallas_call(
        _ffn_norm_kernel,
        grid=(t // tm,),
        in_specs=[
            pl.BlockSpec((tm, d), lambda i: (i, 0)),
            pl.BlockSpec((1, d), lambda i: (0, 0)),
            pl.BlockSpec((ne, d), lambda i: (0, 0)),
            pl.BlockSpec((ne, 1), lambda i: (0, 0)),
        ],
        out_specs=[
            pl.BlockSpec((tm, d), lambda i: (i, 0)),
            pl.BlockSpec((ne, tm), lambda i: (0, i)),
        ],
        out_shape=[jax.ShapeDtypeStruct((t, d), f32), jax.ShapeDtypeStruct((ne, t), f32)],
        compiler_params=_cparams(("arbitrary",)),
        name="ffn_norm",
    )(x1, g_ffn.reshape(1, d), w_router.T, b_router.reshape(ne, 1))


def _cumsum_sublanes(x):
    n = x.shape[0]
    row = lax.broadcasted_iota(i32, x.shape, 0)
    d = 1
    while d < n:
        x = x + jnp.where(row >= d, pltpu.roll(x, d, 0), 0.0)
        d *= 2
    return x


def _stack_rows(rows, n):
    width = rows[0].shape[1]
    sub = lax.broadcasted_iota(i32, (n, width), 0)
    out = jnp.zeros((n, width), rows[0].dtype)
    for k, r in enumerate(rows):
        out = jnp.where(sub == k, r, out)
    return out


def _route_kernel(lg_ref, pos_ref, gcol_ref, meta_ref, esel, rsel, *, chunk):
    ne, t = lg_ref.shape
    e_iota = lax.broadcasted_iota(i32, (ne, chunk), 0).astype(f32)
    a = lax.broadcasted_iota(i32, (chunk, chunk), 0)
    b = lax.broadcasted_iota(i32, (chunk, chunk), 1)
    before = (a < b).astype(bf16)
    counts = jnp.zeros((ne, 1), f32)
    for c in range(t // chunk):
        sl = slice(c * chunk, (c + 1) * chunk)
        vals = lg_ref[:, sl]
        tops, idxs, hots = [], [], []
        for _ in range(TOP_K):
            m = jnp.max(vals, axis=0, keepdims=True)
            idx = jnp.min(jnp.where(vals == m, e_iota, ne), axis=0, keepdims=True)
            hot = e_iota == idx
            vals = jnp.where(hot, -jnp.inf, vals)
            tops.append(m)
            idxs.append(idx)
            hots.append(hot)
        ex = [jnp.exp(v - tops[0]) for v in tops]
        den = ex[0] + ex[1] + ex[2] + ex[3]
        gates = [e / den for e in ex]
        member = sum(h.astype(f32) for h in hots)
        rank = jnp.dot(member.astype(bf16), before, preferred_element_type=f32) + counts
        counts = counts + jnp.sum(member, axis=1, keepdims=True)
        for k in range(TOP_K):
            rsel[k:k + 1, sl] = jnp.sum(jnp.where(hots[k], rank, 0.0), axis=0, keepdims=True)
            esel[k:k + 1, sl] = idxs[k]
        g8 = _stack_rows(gates, SUBLANES)
        gpad = jnp.concatenate([g8, jnp.zeros((LANES - SUBLANES, chunk), f32)], axis=0)
        gcol_ref[sl, :] = gpad.T

    cnt = jnp.broadcast_to(counts, (ne, LANES))
    pcnt = jnp.ceil(cnt * (1.0 / ROW_PAD)) * ROW_PAD
    pend = _cumsum_sublanes(pcnt)
    pstart = pend - pcnt
    npass = jnp.floor((pcnt + (EXPERT_CAP - ROW_PAD)) / EXPERT_CAP + 0.5 * ROW_PAD / EXPERT_CAP)
    iend = _cumsum_sublanes(npass)
    istart = iend - npass
    n_items = jnp.max(iend, axis=0, keepdims=True)

    pstart_col = pstart[:, 0:1]
    for c in range(t // chunk):
        sl = slice(c * chunk, (c + 1) * chunk)
        for k in range(TOP_K):
            hot = e_iota == esel[k:k + 1, sl]
            ps = jnp.sum(jnp.where(hot, pstart_col, 0.0), axis=0, keepdims=True)
            pos_ref[k:k + 1, sl] = (rsel[k:k + 1, sl] + ps).astype(i32)

    e_sub = lax.broadcasted_iota(i32, (ne, LANES), 0)
    lane = lax.broadcasted_iota(i32, (ne, LANES), 1)
    lane_f = lane.astype(f32)
    item = lane_f[0:1, :]
    e_of = jnp.minimum(jnp.sum((iend <= lane_f).astype(f32), axis=0, keepdims=True), ne - 1.0)
    hot = e_sub.astype(f32) == e_of
    pick = lambda v: jnp.sum(jnp.where(hot, v, 0.0), axis=0, keepdims=True)
    p_i = item - pick(istart)
    row0 = pick(pstart) + p_i * EXPERT_CAP
    nrows = jnp.clip(pick(pcnt) - p_i * EXPERT_CAP, 0.0, float(EXPERT_CAP))
    nsub = nrows * (1.0 / ROW_PAD)
    diag = e_sub == lane
    zrow = jnp.sum(jnp.where(diag, pend - ROW_PAD, 0.0), axis=0, keepdims=True)
    zval = jnp.sum(jnp.where(diag, (cnt > 0).astype(f32), 0.0), axis=0, keepdims=True)
    total = jnp.max(pend, axis=0, keepdims=True)
    rows = [e_of, row0, nsub, n_items, zrow, zval, total]
    meta_ref[...] = _stack_rows([r.astype(i32) for r in rows], SUBLANES)


def _route(logits_t):
    ne, t = logits_t.shape
    chunk = 512
    return pl.pallas_call(
        functools.partial(_route_kernel, chunk=chunk),
        out_shape=[
            jax.ShapeDtypeStruct((TOP_K, t), i32),
            jax.ShapeDtypeStruct((t, LANES), f32),
            jax.ShapeDtypeStruct((SUBLANES, LANES), i32),
        ],
        scratch_shapes=[pltpu.VMEM((TOP_K, t), f32), pltpu.VMEM((TOP_K, t), f32)],
        compiler_params=pltpu.CompilerParams(vmem_limit_bytes=VMEM_LIMIT_BYTES),
        name="route",
    )(logits_t)


_M_EXPERT, _M_ROW0, _M_NSUB, _M_NITEMS, _M_ZROW, _M_ZVALID, _M_TOTAL = range(7)


def _zero_tail(meta, zbuf, dst_ref, sem):
    total = pl.multiple_of(meta[_M_TOTAL, 0], ROW_PAD)
    n = (dst_ref.shape[0] - total) // ROW_PAD

    def copy(r):
        return pltpu.make_async_copy(zbuf, dst_ref.at[pl.ds(total + r * ROW_PAD, ROW_PAD), :], sem)

    def start(r, c):
        copy(r).start()
        return c

    def wait(r, c):
        copy(r).wait()
        return c

    lax.fori_loop(0, n, start, 0)
    lax.fori_loop(0, n, wait, 0)


def _dispatch_kernel(meta, pos_ref, h_ref, xs_ref, zbuf, sem, zsem, *, tmd, ne):
    i = pl.program_id(0)

    def zero_copy(e):
        start = pl.multiple_of(meta[_M_ZROW, e], ROW_PAD)
        return pltpu.make_async_copy(zbuf, xs_ref.at[pl.ds(start, ROW_PAD), :], zsem)

    @pl.when(i == 0)
    def _():
        zbuf[...] = jnp.zeros_like(zbuf)

        def zstart(e, c):
            @pl.when(meta[_M_ZVALID, e] > 0)
            def _():
                zero_copy(e).start()
            return c

        def zwait(e, c):
            @pl.when(meta[_M_ZVALID, e] > 0)
            def _():
                zero_copy(e).wait()
            return c

        lax.fori_loop(0, ne, zstart, 0)
        lax.fori_loop(0, ne, zwait, 0)
        _zero_tail(meta, zbuf, xs_ref, zsem)

    def row_copy(r, k):
        return pltpu.make_async_copy(h_ref.at[pl.ds(r, 1), :],
                                     xs_ref.at[pl.ds(pos_ref[k, r], 1), :], sem)

    def issue(r, c):
        for k in range(TOP_K):
            row_copy(r, k).start()
        return c

    def drain(r, c):
        for k in range(TOP_K):
            row_copy(r, k).wait()
        return c

    lax.fori_loop(0, tmd, issue, 0)
    lax.fori_loop(0, tmd, drain, 0)


def _dispatch(meta, pos_t, h2, n_rows, ne):
    t, d = h2.shape
    tmd = 256
    return pl.pallas_call(
        functools.partial(_dispatch_kernel, tmd=tmd, ne=ne),
        grid_spec=pltpu.PrefetchScalarGridSpec(
            num_scalar_prefetch=1,
            grid=(t // tmd,),
            in_specs=[
                pl.BlockSpec((TOP_K, tmd), lambda i, m: (0, i), memory_space=pltpu.SMEM),
                pl.BlockSpec((tmd, d), lambda i, m: (i, 0)),
            ],
            out_specs=pl.BlockSpec(memory_space=pl.ANY),
            scratch_shapes=[pltpu.VMEM((ROW_PAD, d), f32), pltpu.SemaphoreType.DMA, pltpu.SemaphoreType.DMA],
        ),
        out_shape=jax.ShapeDtypeStruct((n_rows, d), f32),
        compiler_params=_cparams(("arbitrary",)),
        name="dispatch",
    )(meta, pos_t, h2)


def _expert_kernel(meta, xs_ref, wgu_hbm, wd_hbm, bgu_ref, bd_ref, ys_ref,
                   x_scr, h_scr, wgu_scr, wd_scr, wbuf, stage, y_scr, w_sem, ld_sem, st_sem,
                   *, nf, nd, tf, td):
    it = pl.program_id(0)
    n_items = meta[_M_NITEMS, 0]
    f_dim = wd_hbm.shape[1]

    def up_copies(e, f, slot):
        cols = pl.ds(pl.multiple_of(f * tf, tf), tf)
        cols_up = pl.ds(pl.multiple_of(f_dim + f * tf, tf), tf)
        return (pltpu.make_async_copy(wgu_hbm.at[e, :, cols], wbuf.at[slot, :, pl.ds(0, tf)], w_sem.at[slot]),
                pltpu.make_async_copy(wgu_hbm.at[e, :, cols_up], wbuf.at[slot, :, pl.ds(tf, tf)], w_sem.at[slot]))

    def down_copy(e, dcol, slot):
        cols = pl.ds(pl.multiple_of(dcol * td, td), td)
        return pltpu.make_async_copy(wd_hbm.at[e, :, cols], wbuf.at[slot, pl.ds(0, f_dim), :], w_sem.at[slot])

    def load_copy(first_row, r, slot):
        return pltpu.make_async_copy(xs_ref.at[pl.ds(first_row + r * ROW_PAD, ROW_PAD), :],
                                     stage.at[slot], ld_sem.at[slot])

    def rows_of(r):
        return pl.ds(pl.multiple_of(r * ROW_PAD, ROW_PAD), ROW_PAD)

    @pl.when(it < n_items)
    def _():
        e = meta[_M_EXPERT, it]
        row0 = pl.multiple_of(meta[_M_ROW0, it], ROW_PAD)
        nsub = meta[_M_NSUB, it]
        has_next = it + 1 < n_items
        nxt = jnp.minimum(it + 1, n_items - 1)
        next_row0 = pl.multiple_of(meta[_M_ROW0, nxt], ROW_PAD)
        next_nsub = meta[_M_NSUB, nxt]

        @pl.when(it == 0)
        def _():
            for c in up_copies(e, 0, 0):
                c.start()
            stage[0] = jnp.zeros(stage.shape[1:], f32)
            _zero_tail(meta, stage.at[0], ys_ref, ld_sem.at[0])
            load_copy(row0, 0, 0).start()

            def body(r, c):
                slot = r % 2
                load_copy(row0, r, slot).wait()

                @pl.when(r + 1 < nsub)
                def _():
                    load_copy(row0, r + 1, 1 - slot).start()

                x_scr[rows_of(r), :] = stage[slot].astype(bf16)
                return c

            lax.fori_loop(0, nsub, body, 0)

        def for_row_tiles(fn):
            per = COMPUTE_ROWS // ROW_PAD

            def body(r, c):
                fn(pl.multiple_of(r * COMPUTE_ROWS, COMPUTE_ROWS), COMPUTE_ROWS)
                return c

            lax.fori_loop(0, nsub // per, body, 0)
            for tail in range(1, per):
                @pl.when(nsub % per == tail)
                def _():
                    fn(pl.multiple_of((nsub // per) * COMPUTE_ROWS, COMPUTE_ROWS), tail * ROW_PAD)

        def up_chunk(f, carry):
            slot = f % 2
            for c in up_copies(e, f, slot):
                c.wait()

            @pl.when(f + 1 < nf)
            def _():
                for c in up_copies(e, f + 1, 1 - slot):
                    c.start()

            @pl.when(f + 1 == nf)
            def _():
                down_copy(e, 0, 1 - slot).start()

            wgu_scr[...] = wbuf[slot].astype(bf16)
            bg = bgu_ref[f]
            bu = bgu_ref[nf + f]

            def up_tile(first, n):
                rows = pl.ds(first, n)
                gu = jnp.dot(x_scr[rows, :], wgu_scr[...], preferred_element_type=f32)
                g = jnp.minimum(gu[:, :tf] + bg, SWIGLU_LIMIT)
                u = jnp.clip(gu[:, tf:] + bu, -SWIGLU_LIMIT, SWIGLU_LIMIT)
                act = (u + 1.0) * (g * jax.nn.sigmoid(SWIGLU_ALPHA * g))
                h_scr[f, rows, :] = act.astype(bf16)

            for_row_tiles(up_tile)
            return carry

        lax.fori_loop(0, nf, up_chunk, 0)

        def store_copy(first, dcol, sl):
            dst = ys_ref.at[pl.ds(row0 + first, ROW_PAD), pl.ds(pl.multiple_of(dcol * td, td), td)]
            return pltpu.make_async_copy(y_scr.at[sl, pl.ds(first, ROW_PAD), :], dst, st_sem.at[sl])

        def wait_stores(sl, count):
            def w(r, c):
                store_copy(0, 0, sl).wait()
                return c
            lax.fori_loop(0, count, w, 0)

        def down_chunk(dcol, carry):
            slot = (nf + dcol) % 2
            yslot = dcol % 2
            down_copy(e, dcol, slot).wait()

            @pl.when(dcol + 1 < nd)
            def _():
                down_copy(e, dcol + 1, 1 - slot).start()

            @pl.when((dcol + 1 == nd) & has_next)
            def _():
                for c in up_copies(meta[_M_EXPERT, nxt], 0, 0):
                    c.start()

            wd_scr[...] = wbuf[slot, 0:f_dim, :].astype(bf16)
            bd = bd_ref[dcol]

            for j in range(stage.shape[0]):
                @pl.when(has_next & (dcol + j * nd < next_nsub))
                def _():
                    load_copy(next_row0, dcol + j * nd, j).start()

            @pl.when(dcol >= 2)
            def _():
                wait_stores(yslot, nsub)

            @pl.when((dcol < 2) & (it > 0))
            def _():
                wait_stores(yslot, meta[_M_NSUB, jnp.maximum(it - 1, 0)])

            def down_tile(first, n):
                rows = pl.ds(first, n)
                hidden = jnp.concatenate([h_scr[cf, rows, :] for cf in range(nf)], axis=1)
                y_scr[yslot, rows, :] = bd + jnp.dot(hidden, wd_scr[...], preferred_element_type=f32)
                for part in range(n // ROW_PAD):
                    store_copy(first + part * ROW_PAD, dcol, yslot).start()

            for_row_tiles(down_tile)

            for j in range(stage.shape[0]):
                @pl.when(has_next & (dcol + j * nd < next_nsub))
                def _():
                    load_copy(next_row0, dcol + j * nd, j).wait()
                    x_scr[rows_of(dcol + j * nd), :] = stage[j].astype(bf16)

            return carry

        lax.fori_loop(0, nd, down_chunk, 0)

        @pl.when(jnp.logical_not(has_next))
        def _():
            wait_stores(0, nsub)
            wait_stores(1, nsub)


def _experts(meta, xs, w_gate_up, b_gate_up, w_down, b_down):
    n_rows, d = xs.shape
    ne, _, f2 = w_gate_up.shape
    f = f2 // 2
    tf, td = 256, 512
    nf, nd = f // tf, d // td
    max_items = (n_rows + ne * (EXPERT_CAP - ROW_PAD)) // EXPERT_CAP
    stage_slots = max(2, pl.cdiv(EXPERT_CAP // ROW_PAD, nd))
    assert max_items <= MAX_ITEMS and td == 2 * tf and f <= d
    assert nf % 2 == 0 and nd % 2 == 0

    def expert(i, m):
        return m[_M_EXPERT, jnp.minimum(i, m[_M_NITEMS, 0] - 1)]

    return pl.pallas_call(
        functools.partial(_expert_kernel, nf=nf, nd=nd, tf=tf, td=td),
        grid_spec=pltpu.PrefetchScalarGridSpec(
            num_scalar_prefetch=1,
            grid=(max_items,),
            in_specs=[
                pl.BlockSpec(memory_space=pl.ANY),
                pl.BlockSpec(memory_space=pl.ANY),
                pl.BlockSpec(memory_space=pl.ANY),
                pl.BlockSpec((None, 2 * nf, 1, tf), lambda i, m: (expert(i, m), 0, 0, 0)),
                pl.BlockSpec((None, nd, 1, td), lambda i, m: (expert(i, m), 0, 0, 0)),
            ],
            out_specs=pl.BlockSpec(memory_space=pl.ANY),
            scratch_shapes=[
                pltpu.VMEM((EXPERT_CAP, d), bf16),
                pltpu.VMEM((nf, EXPERT_CAP, tf), bf16),
                pltpu.VMEM((d, 2 * tf), bf16),
                pltpu.VMEM((f, td), bf16),
                pltpu.VMEM((2, d, 2 * tf), f32),
                pltpu.VMEM((stage_slots, ROW_PAD, d), f32),
                pltpu.VMEM((2, EXPERT_CAP, td), f32),
                pltpu.SemaphoreType.DMA((2,)),
                pltpu.SemaphoreType.DMA((stage_slots,)),
                pltpu.SemaphoreType.DMA((2,)),
            ],
        ),
        out_shape=jax.ShapeDtypeStruct((n_rows, d), f32),
        compiler_params=_cparams(("arbitrary",)),
        name="experts",
    )(meta, xs, w_gate_up, w_down, b_gate_up.reshape(ne, 2 * nf, 1, tf), b_down.reshape(ne, nd, 1, td))


def _combine_kernel(pos_cur, pos_nxt, gcol_ref, x_ref, ys_ref, o_ref, gbuf, sem, *, tmc, n_tiles):
    i = pl.program_id(0)
    slot = i % 2

    def row_copy(pos_ref, sl, r, k):
        return pltpu.make_async_copy(ys_ref.at[pl.ds(pos_ref[k, r], 1), :],
                                     gbuf.at[sl, k, pl.ds(r, 1), :], sem.at[sl])

    def issue(pos_ref, sl):
        def body(r, c):
            for k in range(TOP_K):
                row_copy(pos_ref, sl, r, k).start()
            return c
        lax.fori_loop(0, tmc, body, 0)

    @pl.when(i == 0)
    def _():
        issue(pos_cur, 0)

    @pl.when(i + 1 < n_tiles)
    def _():
        issue(pos_nxt, 1 - slot)

    def drain(r, c):
        for k in range(TOP_K):
            row_copy(pos_cur, slot, r, k).wait()
        return c

    lax.fori_loop(0, tmc, drain, 0)
    acc = x_ref[...]
    for k in range(TOP_K):
        acc = acc + gcol_ref[:, k:k + 1] * gbuf[slot, k]
    o_ref[...] = acc


def _combine(pos_t, gcol, x1, ys):
    t, d = x1.shape
    tmc = 128
    n_tiles = t // tmc
    return pl.pallas_call(
        functools.partial(_combine_kernel, tmc=tmc, n_tiles=n_tiles),
        grid=(n_tiles,),
        in_specs=[
            pl.BlockSpec((TOP_K, tmc), lambda i: (0, i), memory_space=pltpu.SMEM),
            pl.BlockSpec((TOP_K, tmc), lambda i: (0, jnp.minimum(i + 1, n_tiles - 1)), memory_space=pltpu.SMEM),
            pl.BlockSpec((tmc, LANES), lambda i: (i, 0)),
            pl.BlockSpec((tmc, d), lambda i: (i, 0)),
            pl.BlockSpec(memory_space=pl.ANY),
        ],
        out_specs=pl.BlockSpec((tmc, d), lambda i: (i, 0)),
        out_shape=jax.ShapeDtypeStruct((t, d), f32),
        scratch_shapes=[pltpu.VMEM((2, TOP_K, tmc, d), f32), pltpu.SemaphoreType.DMA((2,))],
        compiler_params=_cparams(("arbitrary",)),
        name="combine",
    )(pos_t, pos_t, gcol, x1, ys)


def _layer(x2, batch, seq, g_mix, w_in, b_forget, g_q, g_k, w_conv, g_attn_out, g_conv_out, w_out,
           g_ffn, w_router, b_router, w_gate_up, b_gate_up, w_down, b_down):
    t, d = x2.shape
    nheads = b_forget.shape[0]
    att_width = nheads * HEAD_DIM
    ne = w_router.shape[1]
    c0 = 3 * att_width
    w_f = w_in[:, c0:c0 + nheads]
    w_cv = w_in[:, c0 + nheads:].astype(bf16)

    h, c = _prenorm(x2, g_mix, w_f, b_forget, seq)
    qkv = _qkv(h, w_in, g_q, g_k, att_width)
    conv = _conv(h, w_cv, w_conv, seq)
    att = _attention(qkv, c, batch, seq, nheads)
    x1 = _outproj(att, conv, g_attn_out, g_conv_out, w_out.astype(bf16), x2)

    h2, logits_t = _ffn_norm(x1, g_ffn, w_router, b_router)
    pos_t, gcol, meta = _route(logits_t)
    n_rows = t * TOP_K + ne * ROW_PAD
    xs = _dispatch(meta, pos_t, h2, n_rows, ne)
    ys = _experts(meta, xs, w_gate_up, b_gate_up, w_down, b_down)
    return _combine(pos_t, gcol, x1, ys)


def kernel(x, g_mix, w_in, b_forget, g_q, g_k, w_conv, g_attn_out, g_conv_out, w_out, g_ffn, w_router,
           b_router, w_gate_up, b_gate_up, w_down, b_down):
    b, s, d = x.shape
    x2 = x.reshape(b * s, d)
    for l in range(g_mix.shape[0]):
        x2 = _layer(x2, b, s, g_mix[l], w_in[l], b_forget[l], g_q[l], g_k[l], w_conv[l], g_attn_out[l],
                    g_conv_out[l], w_out[l], g_ffn[l], w_router[l], b_router[l], w_gate_up[l],
                    b_gate_up[l], w_down[l], b_down[l])
    return x2.reshape(b, s, d)
```

```python
import functools

import jax
import jax.numpy as jnp
import numpy as np
from jax import lax
from jax.experimental import pallas as pl
from jax.experimental.pallas import tpu as pltpu

f32 = jnp.float32
bf16 = jnp.bfloat16
i32 = jnp.int32

HEAD_DIM = 128
LANES = 128
SUBLANES = 8
TOP_K = 4
RMS_EPS = 1e-6
SWIGLU_ALPHA = 1.702
SWIGLU_LIMIT = 7.0
VMEM_LIMIT_BYTES = 56 * 1024 * 1024

ROW_PAD = 128
EXPERT_CAP = 1152
COMPUTE_ROWS = 512
MAX_ITEMS = 128
ATTN_ROW_GROUPS = 1


def _cparams(sem):
    return pltpu.CompilerParams(dimension_semantics=sem, vmem_limit_bytes=VMEM_LIMIT_BYTES)


def _log_sigmoid(z):
    return jnp.minimum(z, 0.0) - jnp.log1p(jnp.exp(-jnp.abs(z)))


def _split3(a):
    hi = a.astype(bf16)
    r1 = a - hi.astype(f32)
    mid = r1.astype(bf16)
    lo = (r1 - mid.astype(f32)).astype(bf16)
    return hi, mid, lo


def _dot_nt(a, b):
    return lax.dot_general(a, b, (((1,), (1,)), ((), ())), preferred_element_type=f32)


def _rms(y, g):
    ms = jnp.mean(y * y, axis=-1, keepdims=True)
    return y * lax.rsqrt(ms + RMS_EPS) * g


def _prenorm_kernel(x_ref, g_ref, wf_ref, bf_ref, h_ref, c_ref, carry, *, tiles_per_seq):
    i = pl.program_id(0)
    hb = _rms(x_ref[...], g_ref[...]).astype(bf16)
    h_ref[...] = hb
    logf = _log_sigmoid(_dot_nt(hb, wf_ref[...].astype(bf16)) + bf_ref[...])
    tm = hb.shape[0]
    a = lax.broadcasted_iota(i32, (tm, tm), 0)
    b = lax.broadcasted_iota(i32, (tm, tm), 1)
    lower = (b <= a).astype(bf16)
    c = sum(jnp.dot(lower, p, preferred_element_type=f32) for p in _split3(logf))

    @pl.when(i % tiles_per_seq == 0)
    def _():
        carry[...] = jnp.zeros_like(carry)

    c = c + carry[...]
    c_ref[...] = c
    carry[...] = c[tm - 1:tm, :]


def _prenorm(x2, g_mix, w_t, f_row0, b_f, seq):
    t, d = x2.shape
    nh = b_f.shape[0]
    tm = 512
    assert f_row0 % LANES == 0 and nh <= LANES
    bfr = jnp.pad(b_f, (0, LANES - nh)).reshape(1, LANES)
    return pl.pallas_call(
        functools.partial(_prenorm_kernel, tiles_per_seq=seq // tm),
        grid=(t // tm,),
        in_specs=[
            pl.BlockSpec((tm, d), lambda i: (i, 0)),
            pl.BlockSpec((1, d), lambda i: (0, 0)),
            pl.BlockSpec((LANES, d), lambda i: (f_row0 // LANES, 0)),
            pl.BlockSpec((1, LANES), lambda i: (0, 0)),
        ],
        out_specs=[
            pl.BlockSpec((tm, d), lambda i: (i, 0)),
            pl.BlockSpec((tm, LANES), lambda i: (i, 0)),
        ],
        out_shape=[
            jax.ShapeDtypeStruct((t, d), bf16),
            jax.ShapeDtypeStruct((t, LANES), f32),
        ],
        scratch_shapes=[pltpu.VMEM((1, LANES), f32)],
        compiler_params=_cparams(("arbitrary",)),
        name="prenorm",
    )(x2, g_mix.reshape(1, d), w_t, bfr)


def _qkv_kernel(h_ref, w_ref, g_ref, post_ref, o_ref, wb, *, n_norm_blocks):
    j = pl.program_id(0)
    i = pl.program_id(1)

    @pl.when(i == 0)
    def _():
        wb[...] = w_ref[...].astype(bf16)

    y = _dot_nt(h_ref[...], wb[...])

    @pl.when(j < n_norm_blocks)
    def _():
        for hh in range(y.shape[1] // HEAD_DIM):
            sl = slice(hh * HEAD_DIM, (hh + 1) * HEAD_DIM)
            o_ref[:, sl] = (_rms(y[:, sl], g_ref[:, sl]) * post_ref[:, sl]).astype(bf16)

    @pl.when(j >= n_norm_blocks)
    def _():
        o_ref[...] = y.astype(bf16)


def _qkv(h, w_t, g_q, g_k, att_width):
    t, d = h.shape
    n = 3 * att_width
    tm, tn = 1024, 512
    nheads = att_width // HEAD_DIM
    ones = jnp.ones((att_width,), f32)
    gain = jnp.concatenate([jnp.tile(g_q, nheads), jnp.tile(g_k, nheads), ones]).reshape(1, n)
    post = jnp.concatenate([ones * (1.0 / np.sqrt(HEAD_DIM)), ones, ones]).reshape(1, n)
    return pl.pallas_call(
        functools.partial(_qkv_kernel, n_norm_blocks=2 * att_width // tn),
        grid=(n // tn, t // tm),
        in_specs=[
            pl.BlockSpec((tm, d), lambda j, i: (i, 0)),
            pl.BlockSpec((tn, d), lambda j, i: (j, 0)),
            pl.BlockSpec((1, tn), lambda j, i: (0, j)),
            pl.BlockSpec((1, tn), lambda j, i: (0, j)),
        ],
        out_specs=pl.BlockSpec((tm, tn), lambda j, i: (i, j)),
        out_shape=jax.ShapeDtypeStruct((t, n), bf16),
        scratch_shapes=[pltpu.VMEM((tn, d), bf16)],
        compiler_params=_cparams(("arbitrary", "arbitrary")),
        name="qkv",
    )(h, w_t, gain, post)


def _conv_kernel(h_ref, wb_ref, wc_ref, wx_ref, wk_ref, o_ref, w3, carry, *, tiles_per_seq):
    i = pl.program_id(1)

    @pl.when(i == 0)
    def _():
        for k, w_ref in enumerate((wb_ref, wc_ref, wx_ref)):
            w3[k] = w_ref[...].astype(bf16)

    h = h_ref[...]
    gb = _dot_nt(h, w3[0])
    gc = _dot_nt(h, w3[1])
    xc = _dot_nt(h, w3[2])
    u = gc * xc
    tm = u.shape[0]

    @pl.when(i % tiles_per_seq == 0)
    def _():
        carry[...] = jnp.zeros_like(carry)

    p2 = carry[SUBLANES - 2:SUBLANES - 1, :]
    p1 = carry[SUBLANES - 1:SUBLANES, :]
    row = lax.broadcasted_iota(i32, u.shape, 0)
    u1 = jnp.where(row == 0, p1, pltpu.roll(u, 1, 0))
    u2 = jnp.where(row == 0, p2, jnp.where(row == 1, p1, pltpu.roll(u, 2, 0)))
    y = wk_ref[0:1, :] * u2 + wk_ref[1:2, :] * u1 + wk_ref[2:3, :] * u
    o_ref[...] = gb * y
    carry[...] = u[tm - SUBLANES:tm, :]


def _conv(h, w_t, row0, width, w_conv, seq):
    t, d = h.shape
    tm, tn = 1024, 256
    assert row0 % SUBLANES == 0 and width % tn == 0

    def w_spec(part):
        return pl.BlockSpec((pl.Element(tn), pl.Element(d)),
                            lambda j, i: (pl.multiple_of(row0 + part * width + j * tn, SUBLANES), 0),
                            pipeline_mode=pl.Buffered(1))

    return pl.pallas_call(
        functools.partial(_conv_kernel, tiles_per_seq=seq // tm),
        grid=(width // tn, t // tm),
        in_specs=[
            pl.BlockSpec((tm, d), lambda j, i: (i, 0)),
            w_spec(0), w_spec(1), w_spec(2),
            pl.BlockSpec((3, tn), lambda j, i: (0, j)),
        ],
        out_specs=pl.BlockSpec((tm, tn), lambda j, i: (i, j)),
        out_shape=jax.ShapeDtypeStruct((t, width), f32),
        scratch_shapes=[pltpu.VMEM((3, tn, d), bf16), pltpu.VMEM((SUBLANES, tn), f32)],
        compiler_params=_cparams(("arbitrary", "arbitrary")),
        name="conv",
    )(h, w_t, w_t, w_t, w_conv)


def _bias_lanes(c, ones_first):
    n = c.shape[0]
    lane = lax.broadcasted_iota(i32, (n, LANES), 1)
    out = jnp.zeros((n, LANES), f32)
    base_c, base_1 = (3, 0) if ones_first else (0, 3)
    for k, piece in enumerate(_split3(c)):
        out = jnp.where(lane == base_c + k, piece.astype(f32), out)
        out = jnp.where(lane == base_1 + k, 1.0, out)
    return out.astype(bf16)


def _attn_kernel(q_ref, k_ref, v_ref, cq_ref, ck_ref, o_ref, k_aug, *, tq):
    hd = pl.program_id(1)
    qi = pl.program_id(2)

    def head_column(c_ref):
        lane = lax.broadcasted_iota(i32, c_ref.shape, 1)
        return jnp.sum(jnp.where(lane == hd, c_ref[...], 0.0), axis=1, keepdims=True)

    @pl.when(qi == 0)
    def _():
        k_aug[:, :HEAD_DIM] = k_ref[...]
        k_aug[:, HEAD_DIM:] = _bias_lanes(-head_column(ck_ref), ones_first=True)

    q = jnp.concatenate([q_ref[...], _bias_lanes(head_column(cq_ref), ones_first=False)], axis=1)
    rq = tq // ATTN_ROW_GROUPS
    qs = [q[g * rq:(g + 1) * rq, :] for g in range(ATTN_ROW_GROUPS)]

    def update(qg, kj, vj, state, mask=None):
        m, l, acc = state
        s = lax.dot_general(qg, kj, (((1,), (1,)), ((), ())), preferred_element_type=f32)
        if mask is not None:
            s = jnp.where(mask, s, -jnp.inf)
        m_new = jnp.maximum(m, jnp.max(s, axis=1, keepdims=True))
        alpha = jnp.exp(m - m_new)
        p = jnp.exp(s - m_new)
        l = alpha * l + jnp.sum(p, axis=1, keepdims=True)
        acc = alpha * acc + jnp.dot(p.astype(bf16), vj, preferred_element_type=f32)
        return m_new, l, acc

    def body(j, states):
        rows = pl.ds(pl.multiple_of(j * tq, tq), tq)
        kj, vj = k_aug[rows, :], v_ref[rows, :]
        return tuple(update(qs[g], kj, vj, states[g]) for g in range(ATTN_ROW_GROUPS))

    init = (jnp.full((rq, 1), -jnp.inf, f32), jnp.zeros((rq, 1), f32), jnp.zeros((rq, HEAD_DIM), f32))
    states = lax.fori_loop(0, qi, body, (init,) * ATTN_ROW_GROUPS)
    rows = pl.ds(pl.multiple_of(qi * tq, tq), tq)
    kj, vj = k_aug[rows, :], v_ref[rows, :]
    r = lax.broadcasted_iota(i32, (rq, tq), 0)
    c = lax.broadcasted_iota(i32, (rq, tq), 1)
    for g in range(ATTN_ROW_GROUPS):
        _, l, acc = update(qs[g], kj, vj, states[g], mask=c <= r + g * rq)
        o_ref[g * rq:(g + 1) * rq, :] = acc / l


def _attention(qkv, c, batch, seq, nheads):
    t = qkv.shape[0]
    tq = 512
    nq = seq // tq
    return pl.pallas_call(
        functools.partial(_attn_kernel, tq=tq),
        grid=(batch, nheads, nq),
        in_specs=[
            pl.BlockSpec((tq, HEAD_DIM), lambda b, h, qi: (b * nq + qi, h)),
            pl.BlockSpec((seq, HEAD_DIM), lambda b, h, qi: (b, nheads + h)),
            pl.BlockSpec((seq, HEAD_DIM), lambda b, h, qi: (b, 2 * nheads + h)),
            pl.BlockSpec((tq, LANES), lambda b, h, qi: (b * nq + qi, 0)),
            pl.BlockSpec((seq, LANES), lambda b, h, qi: (b, 0)),
        ],
        out_specs=pl.BlockSpec((tq, HEAD_DIM), lambda b, h, qi: (b * nq + qi, h)),
        out_shape=jax.ShapeDtypeStruct((t, nheads * HEAD_DIM), f32),
        scratch_shapes=[pltpu.VMEM((seq, 2 * HEAD_DIM), bf16)],
        compiler_params=_cparams(("arbitrary", "arbitrary", "arbitrary")),
        name="attention",
    )(qkv, qkv, qkv, c, c)


def _outproj_kernel(att_ref, conv_ref, ga_ref, gc_ref, w_ref, x_ref, o_ref, mixed):
    j = pl.program_id(1)
    wa = att_ref.shape[1]

    @pl.when(j == 0)
    def _():
        mixed[:, :wa] = _rms(att_ref[...], ga_ref[...]).astype(bf16)
        mixed[:, wa:] = _rms(conv_ref[...], gc_ref[...]).astype(bf16)

    o_ref[...] = x_ref[...] + jnp.dot(mixed[...], w_ref[...], preferred_element_type=f32)


def _outproj(att, conv, g_a, g_c, w_o, x2):
    t, d = x2.shape
    wa, wc = att.shape[1], conv.shape[1]
    tm, tn = 512, 1024
    return pl.pallas_call(
        _outproj_kernel,
        grid=(t // tm, d // tn),
        in_specs=[
            pl.BlockSpec((tm, wa), lambda i, j: (i, 0)),
            pl.BlockSpec((tm, wc), lambda i, j: (i, 0)),
            pl.BlockSpec((1, wa), lambda i, j: (0, 0)),
            pl.BlockSpec((1, wc), lambda i, j: (0, 0)),
            pl.BlockSpec((wa + wc, tn), lambda i, j: (0, j)),
            pl.BlockSpec((tm, tn), lambda i, j: (i, j)),
        ],
        out_specs=pl.BlockSpec((tm, tn), lambda i, j: (i, j)),
        out_shape=jax.ShapeDtypeStruct((t, d), f32),
        scratch_shapes=[pltpu.VMEM((tm, wa + wc), bf16)],
        compiler_params=_cparams(("arbitrary", "arbitrary")),
        name="outproj",
    )(att, conv, g_a.reshape(1, wa), g_c.reshape(1, wc), w_o, x2)


def _ffn_norm_kernel(x_ref, g_ref, wrt_ref, br_ref, h_ref, lg_ref):
    h = _rms(x_ref[...], g_ref[...])
    h_ref[...] = h
    lg_ref[...] = lax.dot_general(wrt_ref[...], h, (((1,), (1,)), ((), ())),
                                  precision=lax.Precision.HIGHEST,
                                  preferred_element_type=f32) + br_ref[...]


def _ffn_norm(x1, g_ffn, w_router, b_router):
    t, d = x1.shape
    ne = w_router.shape[1]
    tm = 512
    return pl.pallas_call(
        _ffn_norm_kernel,
        grid=(t // tm,),
        in_specs=[
            pl.BlockSpec((tm, d), lambda i: (i, 0)),
            pl.BlockSpec((1, d), lambda i: (0, 0)),
            pl.BlockSpec((ne, d), lambda i: (0, 0)),
            pl.BlockSpec((ne, 1), lambda i: (0, 0)),
        ],
        out_specs=[
            pl.BlockSpec((tm, d), lambda i: (i, 0)),
            pl.BlockSpec((ne, tm), lambda i: (0, i)),
        ],
        out_shape=[jax.ShapeDtypeStruct((t, d), f32), jax.ShapeDtypeStruct((ne, t), f32)],
        compiler_params=_cparams(("arbitrary",)),
        name="ffn_norm",
    )(x1, g_ffn.reshape(1, d), w_router.T, b_router.reshape(ne, 1))


def _cumsum_sublanes(x):
    n = x.shape[0]
    row = lax.broadcasted_iota(i32, x.shape, 0)
    d = 1
    while d < n:
        x = x + jnp.where(row >= d, pltpu.roll(x, d, 0), 0.0)
        d *= 2
    return x


def _stack_rows(rows, n):
    width = rows[0].shape[1]
    sub = lax.broadcasted_iota(i32, (n, width), 0)
    out = jnp.zeros((n, width), rows[0].dtype)
    for k, r in enumerate(rows):
        out = jnp.where(sub == k, r, out)
    return out


def _route_kernel(lg_ref, pos_ref, gcol_ref, meta_ref, esel, rsel, *, chunk):
    ne, t = lg_ref.shape
    e_iota = lax.broadcasted_iota(i32, (ne, chunk), 0).astype(f32)
    a = lax.broadcasted_iota(i32, (chunk, chunk), 0)
    b = lax.broadcasted_iota(i32, (chunk, chunk), 1)
    before = (a < b).astype(bf16)
    counts = jnp.zeros((ne, 1), f32)
    for c in range(t // chunk):
        sl = slice(c * chunk, (c + 1) * chunk)
        vals = lg_ref[:, sl]
        tops, idxs, hots = [], [], []
        for _ in range(TOP_K):
            m = jnp.max(vals, axis=0, keepdims=True)
            idx = jnp.min(jnp.where(vals == m, e_iota, ne), axis=0, keepdims=True)
            hot = e_iota == idx
            vals = jnp.where(hot, -jnp.inf, vals)
            tops.append(m)
            idxs.append(idx)
            hots.append(hot)
        ex = [jnp.exp(v - tops[0]) for v in tops]
        den = ex[0] + ex[1] + ex[2] + ex[3]
        gates = [e / den for e in ex]
        member = sum(h.astype(f32) for h in hots)
        rank = jnp.dot(member.astype(bf16), before, preferred_element_type=f32) + counts
        counts = counts + jnp.sum(member, axis=1, keepdims=True)
        for k in range(TOP_K):
            rsel[k:k + 1, sl] = jnp.sum(jnp.where(hots[k], rank, 0.0), axis=0, keepdims=True)
            esel[k:k + 1, sl] = idxs[k]
        g8 = _stack_rows(gates, SUBLANES)
        gpad = jnp.concatenate([g8, jnp.zeros((LANES - SUBLANES, chunk), f32)], axis=0)
        gcol_ref[sl, :] = gpad.T

    cnt = jnp.broadcast_to(counts, (ne, LANES))
    pcnt = jnp.ceil(cnt * (1.0 / ROW_PAD)) * ROW_PAD
    pend = _cumsum_sublanes(pcnt)
    pstart = pend - pcnt
    npass = jnp.floor((pcnt + (EXPERT_CAP - ROW_PAD)) / EXPERT_CAP + 0.5 * ROW_PAD / EXPERT_CAP)
    iend = _cumsum_sublanes(npass)
    istart = iend - npass
    n_items = jnp.max(iend, axis=0, keepdims=True)

    pstart_col = pstart[:, 0:1]
    for c in range(t // chunk):
        sl = slice(c * chunk, (c + 1) * chunk)
        for k in range(TOP_K):
            hot = e_iota == esel[k:k + 1, sl]
            ps = jnp.sum(jnp.where(hot, pstart_col, 0.0), axis=0, keepdims=True)
            pos_ref[k:k + 1, sl] = (rsel[k:k + 1, sl] + ps).astype(i32)

    e_sub = lax.broadcasted_iota(i32, (ne, LANES), 0)
    lane = lax.broadcasted_iota(i32, (ne, LANES), 1)
    lane_f = lane.astype(f32)
    item = lane_f[0:1, :]
    e_of = jnp.minimum(jnp.sum((iend <= lane_f).astype(f32), axis=0, keepdims=True), ne - 1.0)
    hot = e_sub.astype(f32) == e_of
    pick = lambda v: jnp.sum(jnp.where(hot, v, 0.0), axis=0, keepdims=True)
    p_i = item - pick(istart)
    row0 = pick(pstart) + p_i * EXPERT_CAP
    nrows = jnp.clip(pick(pcnt) - p_i * EXPERT_CAP, 0.0, float(EXPERT_CAP))
    nsub = nrows * (1.0 / ROW_PAD)
    diag = e_sub == lane
    zrow = jnp.sum(jnp.where(diag, pend - ROW_PAD, 0.0), axis=0, keepdims=True)
    zval = jnp.sum(jnp.where(diag, (cnt > 0).astype(f32), 0.0), axis=0, keepdims=True)
    total = jnp.max(pend, axis=0, keepdims=True)
    rows = [e_of, row0, nsub, n_items, zrow, zval, total]
    meta_ref[...] = _stack_rows([r.astype(i32) for r in rows], SUBLANES)


def _route(logits_t):
    ne, t = logits_t.shape
    chunk = 512
    return pl.pallas_call(
        functools.partial(_route_kernel, chunk=chunk),
        out_shape=[
            jax.ShapeDtypeStruct((TOP_K, t), i32),
            jax.ShapeDtypeStruct((t, LANES), f32),
            jax.ShapeDtypeStruct((SUBLANES, LANES), i32),
        ],
        scratch_shapes=[pltpu.VMEM((TOP_K, t), f32), pltpu.VMEM((TOP_K, t), f32)],
        compiler_params=pltpu.CompilerParams(vmem_limit_bytes=VMEM_LIMIT_BYTES),
        name="route",
    )(logits_t)


_M_EXPERT, _M_ROW0, _M_NSUB, _M_NITEMS, _M_ZROW, _M_ZVALID, _M_TOTAL = range(7)


def _zero_tail(meta, zbuf, dst_ref, sem):
    total = pl.multiple_of(meta[_M_TOTAL, 0], ROW_PAD)
    n = (dst_ref.shape[0] - total) // ROW_PAD

    def copy(r):
        return pltpu.make_async_copy(zbuf, dst_ref.at[pl.ds(total + r * ROW_PAD, ROW_PAD), :], sem)

    def start(r, c):
        copy(r).start()
        return c

    def wait(r, c):
        copy(r).wait()
        return c

    lax.fori_loop(0, n, start, 0)
    lax.fori_loop(0, n, wait, 0)


def _dispatch_kernel(meta, pos_ref, h_ref, xs_ref, zbuf, sem, zsem, *, tmd, ne):
    i = pl.program_id(0)

    def zero_copy(e):
        start = pl.multiple_of(meta[_M_ZROW, e], ROW_PAD)
        return pltpu.make_async_copy(zbuf, xs_ref.at[pl.ds(start, ROW_PAD), :], zsem)

    @pl.when(i == 0)
    def _():
        zbuf[...] = jnp.zeros_like(zbuf)

        def zstart(e, c):
            @pl.when(meta[_M_ZVALID, e] > 0)
            def _():
                zero_copy(e).start()
            return c

        def zwait(e, c):
            @pl.when(meta[_M_ZVALID, e] > 0)
            def _():
                zero_copy(e).wait()
            return c

        lax.fori_loop(0, ne, zstart, 0)
        lax.fori_loop(0, ne, zwait, 0)
        _zero_tail(meta, zbuf, xs_ref, zsem)

    def row_copy(r, k):
        return pltpu.make_async_copy(h_ref.at[pl.ds(r, 1), :],
                                     xs_ref.at[pl.ds(pos_ref[k, r], 1), :], sem)

    def issue(r, c):
        for k in range(TOP_K):
            row_copy(r, k).start()
        return c

    def drain(r, c):
        for k in range(TOP_K):
            row_copy(r, k).wait()
        return c

    lax.fori_loop(0, tmd, issue, 0)
    lax.fori_loop(0, tmd, drain, 0)


def _dispatch(meta, pos_t, h2, n_rows, ne):
    t, d = h2.shape
    tmd = 256
    return pl.pallas_call(
        functools.partial(_dispatch_kernel, tmd=tmd, ne=ne),
        grid_spec=pltpu.PrefetchScalarGridSpec(
            num_scalar_prefetch=1,
            grid=(t // tmd,),
            in_specs=[
                pl.BlockSpec((TOP_K, tmd), lambda i, m: (0, i), memory_space=pltpu.SMEM),
                pl.BlockSpec((tmd, d), lambda i, m: (i, 0)),
            ],
            out_specs=pl.BlockSpec(memory_space=pl.ANY),
            scratch_shapes=[pltpu.VMEM((ROW_PAD, d), f32), pltpu.SemaphoreType.DMA, pltpu.SemaphoreType.DMA],
        ),
        out_shape=jax.ShapeDtypeStruct((n_rows, d), f32),
        compiler_params=_cparams(("arbitrary",)),
        name="dispatch",
    )(meta, pos_t, h2)


def _expert_kernel(meta, xs_ref, wgu_hbm, wd_hbm, bgu_ref, bd_ref, ys_ref,
                   x_scr, h_scr, wgu_scr, wd_scr, wbuf, stage, y_scr, w_sem, ld_sem, st_sem,
                   *, nf, nd, tf, td):
    it = pl.program_id(0)
    n_items = meta[_M_NITEMS, 0]
    f_dim = wd_hbm.shape[1]

    def up_copies(e, f, slot):
        cols = pl.ds(pl.multiple_of(f * tf, tf), tf)
        cols_up = pl.ds(pl.multiple_of(f_dim + f * tf, tf), tf)
        return (pltpu.make_async_copy(wgu_hbm.at[e, :, cols], wbuf.at[slot, :, pl.ds(0, tf)], w_sem.at[slot]),
                pltpu.make_async_copy(wgu_hbm.at[e, :, cols_up], wbuf.at[slot, :, pl.ds(tf, tf)], w_sem.at[slot]))

    def down_copy(e, dcol, slot):
        cols = pl.ds(pl.multiple_of(dcol * td, td), td)
        return pltpu.make_async_copy(wd_hbm.at[e, :, cols], wbuf.at[slot, pl.ds(0, f_dim), :], w_sem.at[slot])

    def load_copy(first_row, r, slot):
        return pltpu.make_async_copy(xs_ref.at[pl.ds(first_row + r * ROW_PAD, ROW_PAD), :],
                                     stage.at[slot], ld_sem.at[slot])

    def rows_of(r):
        return pl.ds(pl.multiple_of(r * ROW_PAD, ROW_PAD), ROW_PAD)

    @pl.when(it < n_items)
    def _():
        e = meta[_M_EXPERT, it]
        row0 = pl.multiple_of(meta[_M_ROW0, it], ROW_PAD)
        nsub = meta[_M_NSUB, it]
        has_next = it + 1 < n_items
        nxt = jnp.minimum(it + 1, n_items - 1)
        next_row0 = pl.multiple_of(meta[_M_ROW0, nxt], ROW_PAD)
        next_nsub = meta[_M_NSUB, nxt]

        @pl.when(it == 0)
        def _():
            for c in up_copies(e, 0, 0):
                c.start()
            stage[0] = jnp.zeros(stage.shape[1:], f32)
            _zero_tail(meta, stage.at[0], ys_ref, ld_sem.at[0])
            load_copy(row0, 0, 0).start()

            def body(r, c):
                slot = r % 2
                load_copy(row0, r, slot).wait()

                @pl.when(r + 1 < nsub)
                def _():
                    load_copy(row0, r + 1, 1 - slot).start()

                x_scr[rows_of(r), :] = stage[slot].astype(bf16)
                return c

            lax.fori_loop(0, nsub, body, 0)

        def for_row_tiles(fn):
            per = COMPUTE_ROWS // ROW_PAD

            def body(r, c):
                fn(pl.multiple_of(r * COMPUTE_ROWS, COMPUTE_ROWS), COMPUTE_ROWS)
                return c

            lax.fori_loop(0, nsub // per, body, 0)
            for tail in range(1, per):
                @pl.when(nsub % per == tail)
                def _():
                    fn(pl.multiple_of((nsub // per) * COMPUTE_ROWS, COMPUTE_ROWS), tail * ROW_PAD)

        def up_chunk(f, carry):
            slot = f % 2
            for c in up_copies(e, f, slot):
                c.wait()

            @pl.when(f + 1 < nf)
            def _():
                for c in up_copies(e, f + 1, 1 - slot):
                    c.start()

            @pl.when(f + 1 == nf)
            def _():
                down_copy(e, 0, 1 - slot).start()

            wgu_scr[...] = wbuf[slot].astype(bf16)
            bg = bgu_ref[f]
            bu = bgu_ref[nf + f]

            def up_tile(first, n):
                rows = pl.ds(first, n)
                gu = jnp.dot(x_scr[rows, :], wgu_scr[...], preferred_element_type=f32)
                g = jnp.minimum(gu[:, :tf] + bg, SWIGLU_LIMIT)
                u = jnp.clip(gu[:, tf:] + bu, -SWIGLU_LIMIT, SWIGLU_LIMIT)
                act = (u + 1.0) * (g * jax.nn.sigmoid(SWIGLU_ALPHA * g))
                h_scr[f, rows, :] = act.astype(bf16)

            for_row_tiles(up_tile)
            return carry

        lax.fori_loop(0, nf, up_chunk, 0)

        def store_copy(first, dcol, sl):
            dst = ys_ref.at[pl.ds(row0 + first, ROW_PAD), pl.ds(pl.multiple_of(dcol * td, td), td)]
            return pltpu.make_async_copy(y_scr.at[sl, pl.ds(first, ROW_PAD), :], dst, st_sem.at[sl])

        def wait_stores(sl, count):
            def w(r, c):
                store_copy(0, 0, sl).wait()
                return c
            lax.fori_loop(0, count, w, 0)

        def down_chunk(dcol, carry):
            slot = (nf + dcol) % 2
            yslot = dcol % 2
            down_copy(e, dcol, slot).wait()

            @pl.when(dcol + 1 < nd)
            def _():
                down_copy(e, dcol + 1, 1 - slot).start()

            @pl.when((dcol + 1 == nd) & has_next)
            def _():
                for c in up_copies(meta[_M_EXPERT, nxt], 0, 0):
                    c.start()

            wd_scr[...] = wbuf[slot, 0:f_dim, :].astype(bf16)
            bd = bd_ref[dcol]

            for j in range(stage.shape[0]):
                @pl.when(has_next & (dcol + j * nd < next_nsub))
                def _():
                    load_copy(next_row0, dcol + j * nd, j).start()

            @pl.when(dcol >= 2)
            def _():
                wait_stores(yslot, nsub)

            @pl.when((dcol < 2) & (it > 0))
            def _():
                wait_stores(yslot, meta[_M_NSUB, jnp.maximum(it - 1, 0)])

            def down_tile(first, n):
                rows = pl.ds(first, n)
                hidden = jnp.concatenate([h_scr[cf, rows, :] for cf in range(nf)], axis=1)
                y_scr[yslot, rows, :] = bd + jnp.dot(hidden, wd_scr[...], preferred_element_type=f32)
                for part in range(n // ROW_PAD):
                    store_copy(first + part * ROW_PAD, dcol, yslot).start()

            for_row_tiles(down_tile)

            for j in range(stage.shape[0]):
                @pl.when(has_next & (dcol + j * nd < next_nsub))
                def _():
                    load_copy(next_row0, dcol + j * nd, j).wait()
                    x_scr[rows_of(dcol + j * nd), :] = stage[j].astype(bf16)

            return carry

        lax.fori_loop(0, nd, down_chunk, 0)

        @pl.when(jnp.logical_not(has_next))
        def _():
            wait_stores(0, nsub)
            wait_stores(1, nsub)


def _experts(meta, xs, w_gate_up, b_gate_up, w_down, b_down):
    n_rows, d = xs.shape
    ne, _, f2 = w_gate_up.shape
    f = f2 // 2
    tf, td = 256, 512
    nf, nd = f // tf, d // td
    max_items = (n_rows + ne * (EXPERT_CAP - ROW_PAD)) // EXPERT_CAP
    stage_slots = max(2, pl.cdiv(EXPERT_CAP // ROW_PAD, nd))
    assert max_items <= MAX_ITEMS and td == 2 * tf and f <= d
    assert nf % 2 == 0 and nd % 2 == 0

    def expert(i, m):
        return m[_M_EXPERT, jnp.minimum(i, m[_M_NITEMS, 0] - 1)]

    return pl.pallas_call(
        functools.partial(_expert_kernel, nf=nf, nd=nd, tf=tf, td=td),
        grid_spec=pltpu.PrefetchScalarGridSpec(
            num_scalar_prefetch=1,
            grid=(max_items,),
            in_specs=[
                pl.BlockSpec(memory_space=pl.ANY),
                pl.BlockSpec(memory_space=pl.ANY),
                pl.BlockSpec(memory_space=pl.ANY),
                pl.BlockSpec((None, 2 * nf, 1, tf), lambda i, m: (expert(i, m), 0, 0, 0)),
                pl.BlockSpec((None, nd, 1, td), lambda i, m: (expert(i, m), 0, 0, 0)),
            ],
            out_specs=pl.BlockSpec(memory_space=pl.ANY),
            scratch_shapes=[
                pltpu.VMEM((EXPERT_CAP, d), bf16),
                pltpu.VMEM((nf, EXPERT_CAP, tf), bf16),
                pltpu.VMEM((d, 2 * tf), bf16),
                pltpu.VMEM((f, td), bf16),
                pltpu.VMEM((2, d, 2 * tf), f32),
                pltpu.VMEM((stage_slots, ROW_PAD, d), f32),
                pltpu.VMEM((2, EXPERT_CAP, td), f32),
                pltpu.SemaphoreType.DMA((2,)),
                pltpu.SemaphoreType.DMA((stage_slots,)),
                pltpu.SemaphoreType.DMA((2,)),
            ],
        ),
        out_shape=jax.ShapeDtypeStruct((n_rows, d), f32),
        compiler_params=_cparams(("arbitrary",)),
        name="experts",
    )(meta, xs, w_gate_up, w_down, b_gate_up.reshape(ne, 2 * nf, 1, tf), b_down.reshape(ne, nd, 1, td))


def _combine_kernel(pos_cur, pos_nxt, gcol_ref, x_ref, ys_ref, o_ref, gbuf, sem, *, tmc, n_tiles):
    i = pl.program_id(0)
    slot = i % 2

    def row_copy(pos_ref, sl, r, k):
        return pltpu.make_async_copy(ys_ref.at[pl.ds(pos_ref[k, r], 1), :],
                                     gbuf.at[sl, k, pl.ds(r, 1), :], sem.at[sl])

    def issue(pos_ref, sl):
        def body(r, c):
            for k in range(TOP_K):
                row_copy(pos_ref, sl, r, k).start()
            return c
        lax.fori_loop(0, tmc, body, 0)

    @pl.when(i == 0)
    def _():
        issue(pos_cur, 0)

    @pl.when(i + 1 < n_tiles)
    def _():
        issue(pos_nxt, 1 - slot)

    def drain(r, c):
        for k in range(TOP_K):
            row_copy(pos_cur, slot, r, k).wait()
        return c

    lax.fori_loop(0, tmc, drain, 0)
    acc = x_ref[...]
    for k in range(TOP_K):
        acc = acc + gcol_ref[:, k:k + 1] * gbuf[slot, k]
    o_ref[...] = acc


def _combine(pos_t, gcol, x1, ys):
    t, d = x1.shape
    tmc = 128
    n_tiles = t // tmc
    return pl.pallas_call(
        functools.partial(_combine_kernel, tmc=tmc, n_tiles=n_tiles),
        grid=(n_tiles,),
        in_specs=[
            pl.BlockSpec((TOP_K, tmc), lambda i: (0, i), memory_space=pltpu.SMEM),
            pl.BlockSpec((TOP_K, tmc), lambda i: (0, jnp.minimum(i + 1, n_tiles - 1)), memory_space=pltpu.SMEM),
            pl.BlockSpec((tmc, LANES), lambda i: (i, 0)),
            pl.BlockSpec((tmc, d), lambda i: (i, 0)),
            pl.BlockSpec(memory_space=pl.ANY),
        ],
        out_specs=pl.BlockSpec((tmc, d), lambda i: (i, 0)),
        out_shape=jax.ShapeDtypeStruct((t, d), f32),
        scratch_shapes=[pltpu.VMEM((2, TOP_K, tmc, d), f32), pltpu.SemaphoreType.DMA((2,))],
        compiler_params=_cparams(("arbitrary",)),
        name="combine",
    )(pos_t, pos_t, gcol, x1, ys)


def _layer(x2, batch, seq, g_mix, w_in, b_forget, g_q, g_k, w_conv, g_attn_out, g_conv_out, w_out,
           g_ffn, w_router, b_router, w_gate_up, b_gate_up, w_down, b_down):
    t, d = x2.shape
    nheads = b_forget.shape[0]
    att_width = nheads * HEAD_DIM
    ne = w_router.shape[1]
    c0 = 3 * att_width
    w_t = w_in.T
    conv_width = (w_t.shape[0] - c0 - nheads) // 3

    h, c = _prenorm(x2, g_mix, w_t, c0, b_forget, seq)
    qkv = _qkv(h, w_t, g_q, g_k, att_width)
    conv = _conv(h, w_t, c0 + nheads, conv_width, w_conv, seq)
    att = _attention(qkv, c, batch, seq, nheads)
    x1 = _outproj(att, conv, g_attn_out, g_conv_out, w_out.astype(bf16), x2)

    h2, logits_t = _ffn_norm(x1, g_ffn, w_router, b_router)
    pos_t, gcol, meta = _route(logits_t)
    n_rows = t * TOP_K + ne * ROW_PAD
    xs = _dispatch(meta, pos_t, h2, n_rows, ne)
    ys = _experts(meta, xs, w_gate_up, b_gate_up, w_down, b_down)
    return _combine(pos_t, gcol, x1, ys)


def kernel(x, g_mix, w_in, b_forget, g_q, g_k, w_conv, g_attn_out, g_conv_out, w_out, g_ffn, w_router,
           b_router, w_gate_up, b_gate_up, w_down, b_down):
    b, s, d = x.shape
    x2 = x.reshape(b * s, d)
    for l in range(g_mix.shape[0]):
        x2 = _layer(x2, b, s, g_mix[l], w_in[l], b_forget[l], g_q[l], g_k[l], w_conv[l], g_attn_out[l],
                    g_conv_out[l], w_out[l], g_ffn[l], w_router[l], b_router[l], w_gate_up[l],
                    b_gate_up[l], w_down[l], b_down[l])
    return x2.reshape(b, s, d)
```

```python
import functools

import jax
import jax.numpy as jnp
import numpy as np
from jax import lax
from jax.experimental import pallas as pl
from jax.experimental.pallas import tpu as pltpu

f32 = jnp.float32
bf16 = jnp.bfloat16
i32 = jnp.int32

HEAD_DIM = 128
LANES = 128
SUBLANES = 8
TOP_K = 4
RMS_EPS = 1e-6
SWIGLU_ALPHA = 1.702
SWIGLU_LIMIT = 7.0
VMEM_LIMIT_BYTES = 56 * 1024 * 1024

ROW_PAD = 128
EXPERT_CAP = 1152
COMPUTE_ROWS = 512
MAX_ITEMS = 128


def _cparams(sem):
    return pltpu.CompilerParams(dimension_semantics=sem, vmem_limit_bytes=VMEM_LIMIT_BYTES)


def _log_sigmoid(z):
    return jnp.minimum(z, 0.0) - jnp.log1p(jnp.exp(-jnp.abs(z)))


def _split3(a):
    hi = a.astype(bf16)
    r1 = a - hi.astype(f32)
    mid = r1.astype(bf16)
    lo = (r1 - mid.astype(f32)).astype(bf16)
    return hi, mid, lo


def _dot_nt(a, b):
    return lax.dot_general(a, b, (((1,), (1,)), ((), ())), preferred_element_type=f32)


def _rms(y, g):
    ms = jnp.mean(y * y, axis=-1, keepdims=True)
    return y * lax.rsqrt(ms + RMS_EPS) * g


def _prenorm_kernel(x_ref, g_ref, wf_ref, bf_ref, h_ref, c_ref, carry, *, tiles_per_seq):
    i = pl.program_id(0)
    hb = _rms(x_ref[...], g_ref[...]).astype(bf16)
    h_ref[...] = hb
    logf = _log_sigmoid(_dot_nt(hb, wf_ref[...].astype(bf16)) + bf_ref[...])
    tm = hb.shape[0]
    a = lax.broadcasted_iota(i32, (tm, tm), 0)
    b = lax.broadcasted_iota(i32, (tm, tm), 1)
    lower = (b <= a).astype(bf16)
    c = sum(jnp.dot(lower, p, preferred_element_type=f32) for p in _split3(logf))

    @pl.when(i % tiles_per_seq == 0)
    def _():
        carry[...] = jnp.zeros_like(carry)

    c = c + carry[...]
    c_ref[...] = c
    carry[...] = c[tm - 1:tm, :]


def _prenorm(x2, g_mix, w_t, f_row0, b_f, seq):
    t, d = x2.shape
    nh = b_f.shape[0]
    tm = 512
    assert f_row0 % LANES == 0 and nh <= LANES
    bfr = jnp.pad(b_f, (0, LANES - nh)).reshape(1, LANES)
    return pl.pallas_call(
        functools.partial(_prenorm_kernel, tiles_per_seq=seq // tm),
        grid=(t // tm,),
        in_specs=[
            pl.BlockSpec((tm, d), lambda i: (i, 0)),
            pl.BlockSpec((1, d), lambda i: (0, 0)),
            pl.BlockSpec((LANES, d), lambda i: (f_row0 // LANES, 0)),
            pl.BlockSpec((1, LANES), lambda i: (0, 0)),
        ],
        out_specs=[
            pl.BlockSpec((tm, d), lambda i: (i, 0)),
            pl.BlockSpec((tm, LANES), lambda i: (i, 0)),
        ],
        out_shape=[
            jax.ShapeDtypeStruct((t, d), bf16),
            jax.ShapeDtypeStruct((t, LANES), f32),
        ],
        scratch_shapes=[pltpu.VMEM((1, LANES), f32)],
        compiler_params=_cparams(("arbitrary",)),
        name="prenorm",
    )(x2, g_mix.reshape(1, d), w_t, bfr)


def _qkv_kernel(h_ref, w_ref, g_ref, post_ref, o_ref, *, n_norm_blocks):
    j = pl.program_id(0)
    y = _dot_nt(h_ref[...], w_ref[...].astype(bf16))

    @pl.when(j < n_norm_blocks)
    def _():
        for hh in range(y.shape[1] // HEAD_DIM):
            sl = slice(hh * HEAD_DIM, (hh + 1) * HEAD_DIM)
            o_ref[:, sl] = (_rms(y[:, sl], g_ref[:, sl]) * post_ref[:, sl]).astype(bf16)

    @pl.when(j >= n_norm_blocks)
    def _():
        o_ref[...] = y.astype(bf16)


def _qkv(h, w_t, g_q, g_k, att_width):
    t, d = h.shape
    n = 3 * att_width
    tm, tn = 1024, 512
    nheads = att_width // HEAD_DIM
    ones = jnp.ones((att_width,), f32)
    gain = jnp.concatenate([jnp.tile(g_q, nheads), jnp.tile(g_k, nheads), ones]).reshape(1, n)
    post = jnp.concatenate([ones * (1.0 / np.sqrt(HEAD_DIM)), ones, ones]).reshape(1, n)
    return pl.pallas_call(
        functools.partial(_qkv_kernel, n_norm_blocks=2 * att_width // tn),
        grid=(n // tn, t // tm),
        in_specs=[
            pl.BlockSpec((tm, d), lambda j, i: (i, 0)),
            pl.BlockSpec((tn, d), lambda j, i: (j, 0)),
            pl.BlockSpec((1, tn), lambda j, i: (0, j)),
            pl.BlockSpec((1, tn), lambda j, i: (0, j)),
        ],
        out_specs=pl.BlockSpec((tm, tn), lambda j, i: (i, j)),
        out_shape=jax.ShapeDtypeStruct((t, n), bf16),
        compiler_params=_cparams(("arbitrary", "arbitrary")),
        name="qkv",
    )(h, w_t, gain, post)


def _conv_kernel(h_ref, wb_ref, wc_ref, wx_ref, wk_ref, o_ref, carry, *, tiles_per_seq):
    i = pl.program_id(1)
    h = h_ref[...]
    gb = _dot_nt(h, wb_ref[...].astype(bf16))
    gc = _dot_nt(h, wc_ref[...].astype(bf16))
    xc = _dot_nt(h, wx_ref[...].astype(bf16))
    u = gc * xc
    tm = u.shape[0]

    @pl.when(i % tiles_per_seq == 0)
    def _():
        carry[...] = jnp.zeros_like(carry)

    p2 = carry[SUBLANES - 2:SUBLANES - 1, :]
    p1 = carry[SUBLANES - 1:SUBLANES, :]
    row = lax.broadcasted_iota(i32, u.shape, 0)
    u1 = jnp.where(row == 0, p1, pltpu.roll(u, 1, 0))
    u2 = jnp.where(row == 0, p2, jnp.where(row == 1, p1, pltpu.roll(u, 2, 0)))
    y = wk_ref[0:1, :] * u2 + wk_ref[1:2, :] * u1 + wk_ref[2:3, :] * u
    o_ref[...] = gb * y
    carry[...] = u[tm - SUBLANES:tm, :]


def _conv(h, w_t, row0, width, w_conv, seq):
    t, d = h.shape
    tm, tn = 1024, 256
    assert row0 % SUBLANES == 0 and width % tn == 0

    def w_spec(part):
        return pl.BlockSpec((pl.Element(tn), pl.Element(d)),
                            lambda j, i: (pl.multiple_of(row0 + part * width + j * tn, SUBLANES), 0))

    return pl.pallas_call(
        functools.partial(_conv_kernel, tiles_per_seq=seq // tm),
        grid=(width // tn, t // tm),
        in_specs=[
            pl.BlockSpec((tm, d), lambda j, i: (i, 0)),
            w_spec(0), w_spec(1), w_spec(2),
            pl.BlockSpec((3, tn), lambda j, i: (0, j)),
        ],
        out_specs=pl.BlockSpec((tm, tn), lambda j, i: (i, j)),
        out_shape=jax.ShapeDtypeStruct((t, width), f32),
        scratch_shapes=[pltpu.VMEM((SUBLANES, tn), f32)],
        compiler_params=_cparams(("arbitrary", "arbitrary")),
        name="conv",
    )(h, w_t, w_t, w_t, w_conv)


def _bias_lanes(c, ones_first):
    n = c.shape[0]
    lane = lax.broadcasted_iota(i32, (n, LANES), 1)
    out = jnp.zeros((n, LANES), f32)
    base_c, base_1 = (3, 0) if ones_first else (0, 3)
    for k, piece in enumerate(_split3(c)):
        out = jnp.where(lane == base_c + k, piece.astype(f32), out)
        out = jnp.where(lane == base_1 + k, 1.0, out)
    return out.astype(bf16)


def _attn_kernel(q_ref, k_ref, v_ref, cq_ref, ck_ref, o_ref, k_aug, *, tq):
    hd = pl.program_id(1)
    qi = pl.program_id(2)

    def head_column(c_ref):
        lane = lax.broadcasted_iota(i32, c_ref.shape, 1)
        return jnp.sum(jnp.where(lane == hd, c_ref[...], 0.0), axis=1, keepdims=True)

    @pl.when(qi == 0)
    def _():
        k_aug[:, :HEAD_DIM] = k_ref[...]
        k_aug[:, HEAD_DIM:] = _bias_lanes(-head_column(ck_ref), ones_first=True)

    q = jnp.concatenate([q_ref[...], _bias_lanes(head_column(cq_ref), ones_first=False)], axis=1)
    def rows_of(j):
        return pl.ds(pl.multiple_of(j * tq, tq), tq)

    def scores(j):
        return _dot_nt(q, k_aug[rows_of(j), :])

    def update(s, vj, state):
        m, l, acc = state
        m_new = jnp.maximum(m, jnp.max(s, axis=1, keepdims=True))
        alpha = jnp.exp(m - m_new)
        p = jnp.exp(s - m_new)
        l = alpha * l + jnp.sum(p, axis=1, keepdims=True)
        acc = alpha * acc + jnp.dot(p.astype(bf16), vj, preferred_element_type=f32)
        return m_new, l, acc

    def body(j, state):
        return update(scores(j), v_ref[rows_of(j), :], state)

    init = (jnp.full((tq, 1), -jnp.inf, f32), jnp.zeros((tq, 1), f32), jnp.zeros((tq, HEAD_DIM), f32))
    state = lax.fori_loop(0, qi, body, init)
    r = lax.broadcasted_iota(i32, (tq, tq), 0)
    c = lax.broadcasted_iota(i32, (tq, tq), 1)
    _, l, acc = update(jnp.where(c <= r, scores(qi), -jnp.inf), v_ref[rows_of(qi), :], state)
    o_ref[...] = acc / l


def _attention(qkv, c, batch, seq, nheads):
    t = qkv.shape[0]
    tq = 512
    nq = seq // tq
    return pl.pallas_call(
        functools.partial(_attn_kernel, tq=tq),
        grid=(batch, nheads, nq),
        in_specs=[
            pl.BlockSpec((tq, HEAD_DIM), lambda b, h, qi: (b * nq + qi, h)),
            pl.BlockSpec((seq, HEAD_DIM), lambda b, h, qi: (b, nheads + h)),
            pl.BlockSpec((seq, HEAD_DIM), lambda b, h, qi: (b, 2 * nheads + h)),
            pl.BlockSpec((tq, LANES), lambda b, h, qi: (b * nq + qi, 0)),
            pl.BlockSpec((seq, LANES), lambda b, h, qi: (b, 0)),
        ],
        out_specs=pl.BlockSpec((tq, HEAD_DIM), lambda b, h, qi: (b * nq + qi, h)),
        out_shape=jax.ShapeDtypeStruct((t, nheads * HEAD_DIM), f32),
        scratch_shapes=[pltpu.VMEM((seq, 2 * HEAD_DIM), bf16)],
        compiler_params=_cparams(("arbitrary", "arbitrary", "arbitrary")),
        name="attention",
    )(qkv, qkv, qkv, c, c)


def _outproj_kernel(att_ref, conv_ref, ga_ref, gc_ref, w_ref, x_ref, o_ref, mixed):
    j = pl.program_id(1)
    wa = att_ref.shape[1]

    @pl.when(j == 0)
    def _():
        mixed[:, :wa] = _rms(att_ref[...], ga_ref[...]).astype(bf16)
        mixed[:, wa:] = _rms(conv_ref[...], gc_ref[...]).astype(bf16)

    o_ref[...] = x_ref[...] + jnp.dot(mixed[...], w_ref[...], preferred_element_type=f32)


def _outproj(att, conv, g_a, g_c, w_o, x2):
    t, d = x2.shape
    wa, wc = att.shape[1], conv.shape[1]
    tm, tn = 512, 1024
    return pl.pallas_call(
        _outproj_kernel,
        grid=(t // tm, d // tn),
        in_specs=[
            pl.BlockSpec((tm, wa), lambda i, j: (i, 0)),
            pl.BlockSpec((tm, wc), lambda i, j: (i, 0)),
            pl.BlockSpec((1, wa), lambda i, j: (0, 0)),
            pl.BlockSpec((1, wc), lambda i, j: (0, 0)),
            pl.BlockSpec((wa + wc, tn), lambda i, j: (0, j)),
            pl.BlockSpec((tm, tn), lambda i, j: (i, j)),
        ],
        out_specs=pl.BlockSpec((tm, tn), lambda i, j: (i, j)),
        out_shape=jax.ShapeDtypeStruct((t, d), f32),
        scratch_shapes=[pltpu.VMEM((tm, wa + wc), bf16)],
        compiler_params=_cparams(("arbitrary", "arbitrary")),
        name="outproj",
    )(att, conv, g_a.reshape(1, wa), g_c.reshape(1, wc), w_o, x2)


def _router_kernel(x_ref, g_ref, wrt_ref, br_ref, lg_ref):
    h = _rms(x_ref[...], g_ref[...])
    lg_ref[...] = lax.dot_general(wrt_ref[...], h, (((1,), (1,)), ((), ())),
                                  precision=lax.Precision.HIGHEST,
                                  preferred_element_type=f32) + br_ref[...]


def _router(x1, g_ffn, w_router, b_router):
    t, d = x1.shape
    ne = w_router.shape[1]
    tm = 512
    return pl.pallas_call(
        _router_kernel,
        grid=(t // tm,),
        in_specs=[
            pl.BlockSpec((tm, d), lambda i: (i, 0)),
            pl.BlockSpec((1, d), lambda i: (0, 0)),
            pl.BlockSpec((ne, d), lambda i: (0, 0)),
            pl.BlockSpec((ne, 1), lambda i: (0, 0)),
        ],
        out_specs=pl.BlockSpec((ne, tm), lambda i: (0, i)),
        out_shape=jax.ShapeDtypeStruct((ne, t), f32),
        compiler_params=_cparams(("arbitrary",)),
        name="router",
    )(x1, g_ffn.reshape(1, d), w_router.T, b_router.reshape(ne, 1))


def _cumsum_sublanes(x):
    n = x.shape[0]
    row = lax.broadcasted_iota(i32, x.shape, 0)
    d = 1
    while d < n:
        x = x + jnp.where(row >= d, pltpu.roll(x, d, 0), 0.0)
        d *= 2
    return x


def _stack_rows(rows, n):
    width = rows[0].shape[1]
    sub = lax.broadcasted_iota(i32, (n, width), 0)
    out = jnp.zeros((n, width), rows[0].dtype)
    for k, r in enumerate(rows):
        out = jnp.where(sub == k, r, out)
    return out


def _route_kernel(lg_ref, pos_ref, gcol_ref, meta_ref, esel, rsel, *, chunk):
    ne, t = lg_ref.shape
    e_iota = lax.broadcasted_iota(i32, (ne, chunk), 0).astype(f32)
    a = lax.broadcasted_iota(i32, (chunk, chunk), 0)
    b = lax.broadcasted_iota(i32, (chunk, chunk), 1)
    before = (a < b).astype(bf16)
    counts = jnp.zeros((ne, 1), f32)
    for c in range(t // chunk):
        sl = slice(c * chunk, (c + 1) * chunk)
        vals = lg_ref[:, sl]
        tops, idxs, hots = [], [], []
        for _ in range(TOP_K):
            m = jnp.max(vals, axis=0, keepdims=True)
            idx = jnp.min(jnp.where(vals == m, e_iota, ne), axis=0, keepdims=True)
            hot = e_iota == idx
            vals = jnp.where(hot, -jnp.inf, vals)
            tops.append(m)
            idxs.append(idx)
            hots.append(hot)
        ex = [jnp.exp(v - tops[0]) for v in tops]
        den = ex[0] + ex[1] + ex[2] + ex[3]
        gates = [e / den for e in ex]
        member = sum(h.astype(f32) for h in hots)
        rank = jnp.dot(member.astype(bf16), before, preferred_element_type=f32) + counts
        counts = counts + jnp.sum(member, axis=1, keepdims=True)
        for k in range(TOP_K):
            rsel[k:k + 1, sl] = jnp.sum(jnp.where(hots[k], rank, 0.0), axis=0, keepdims=True)
            esel[k:k + 1, sl] = idxs[k]
        g8 = _stack_rows(gates, SUBLANES)
        gpad = jnp.concatenate([g8, jnp.zeros((LANES - SUBLANES, chunk), f32)], axis=0)
        gcol_ref[sl, :] = gpad.T

    cnt = jnp.broadcast_to(counts, (ne, LANES))
    pcnt = jnp.ceil(cnt * (1.0 / ROW_PAD)) * ROW_PAD
    pend = _cumsum_sublanes(pcnt)
    pstart = pend - pcnt
    npass = jnp.floor((pcnt + (EXPERT_CAP - ROW_PAD)) / EXPERT_CAP + 0.5 * ROW_PAD / EXPERT_CAP)
    iend = _cumsum_sublanes(npass)
    istart = iend - npass
    n_items = jnp.max(iend, axis=0, keepdims=True)

    pstart_col = pstart[:, 0:1]
    for c in range(t // chunk):
        sl = slice(c * chunk, (c + 1) * chunk)
        for k in range(TOP_K):
            hot = e_iota == esel[k:k + 1, sl]
            ps = jnp.sum(jnp.where(hot, pstart_col, 0.0), axis=0, keepdims=True)
            pos_ref[k:k + 1, sl] = (rsel[k:k + 1, sl] + ps).astype(i32)

    e_sub = lax.broadcasted_iota(i32, (ne, LANES), 0)
    lane = lax.broadcasted_iota(i32, (ne, LANES), 1)
    lane_f = lane.astype(f32)
    item = lane_f[0:1, :]
    e_of = jnp.minimum(jnp.sum((iend <= lane_f).astype(f32), axis=0, keepdims=True), ne - 1.0)
    hot = e_sub.astype(f32) == e_of
    pick = lambda v: jnp.sum(jnp.where(hot, v, 0.0), axis=0, keepdims=True)
    p_i = item - pick(istart)
    row0 = pick(pstart) + p_i * EXPERT_CAP
    nrows = jnp.clip(pick(pcnt) - p_i * EXPERT_CAP, 0.0, float(EXPERT_CAP))
    nsub = nrows * (1.0 / ROW_PAD)
    diag = e_sub == lane
    zrow = jnp.sum(jnp.where(diag, pend - ROW_PAD, 0.0), axis=0, keepdims=True)
    zval = jnp.sum(jnp.where(diag, (cnt > 0).astype(f32), 0.0), axis=0, keepdims=True)
    total = jnp.max(pend, axis=0, keepdims=True)
    rows = [e_of, row0, nsub, n_items, zrow, zval, total]
    meta_ref[...] = _stack_rows([r.astype(i32) for r in rows], SUBLANES)


def _route(logits_t):
    ne, t = logits_t.shape
    chunk = 512
    return pl.pallas_call(
        functools.partial(_route_kernel, chunk=chunk),
        out_shape=[
            jax.ShapeDtypeStruct((TOP_K, t), i32),
            jax.ShapeDtypeStruct((t, LANES), f32),
            jax.ShapeDtypeStruct((SUBLANES, LANES), i32),
        ],
        scratch_shapes=[pltpu.VMEM((TOP_K, t), f32), pltpu.VMEM((TOP_K, t), f32)],
        compiler_params=pltpu.CompilerParams(vmem_limit_bytes=VMEM_LIMIT_BYTES),
        name="route",
    )(logits_t)


_M_EXPERT, _M_ROW0, _M_NSUB, _M_NITEMS, _M_ZROW, _M_ZVALID, _M_TOTAL = range(7)


def _zero_tail(meta, zbuf, dst_ref, sem):
    total = pl.multiple_of(meta[_M_TOTAL, 0], ROW_PAD)
    n = (dst_ref.shape[0] - total) // ROW_PAD

    def copy(r):
        return pltpu.make_async_copy(zbuf, dst_ref.at[pl.ds(total + r * ROW_PAD, ROW_PAD), :], sem)

    def start(r, c):
        copy(r).start()
        return c

    def wait(r, c):
        copy(r).wait()
        return c

    lax.fori_loop(0, n, start, 0)
    lax.fori_loop(0, n, wait, 0)


def _dispatch_kernel(meta, pos_ref, x_ref, g_ref, xs_ref, h_scr, zbuf, sem, zsem, *, tmd, ne):
    i = pl.program_id(0)
    h_scr[...] = _rms(x_ref[...], g_ref[...])

    def zero_copy(e):
        start = pl.multiple_of(meta[_M_ZROW, e], ROW_PAD)
        return pltpu.make_async_copy(zbuf, xs_ref.at[pl.ds(start, ROW_PAD), :], zsem)

    @pl.when(i == 0)
    def _():
        zbuf[...] = jnp.zeros_like(zbuf)

        def zstart(e, c):
            @pl.when(meta[_M_ZVALID, e] > 0)
            def _():
                zero_copy(e).start()
            return c

        def zwait(e, c):
            @pl.when(meta[_M_ZVALID, e] > 0)
            def _():
                zero_copy(e).wait()
            return c

        lax.fori_loop(0, ne, zstart, 0)
        lax.fori_loop(0, ne, zwait, 0)
        _zero_tail(meta, zbuf, xs_ref, zsem)

    def row_copy(r, k):
        return pltpu.make_async_copy(h_scr.at[pl.ds(r, 1), :],
                                     xs_ref.at[pl.ds(pos_ref[k, r], 1), :], sem)

    def issue(r, c):
        for k in range(TOP_K):
            row_copy(r, k).start()
        return c

    def drain(r, c):
        for k in range(TOP_K):
            row_copy(r, k).wait()
        return c

    lax.fori_loop(0, tmd, issue, 0)
    lax.fori_loop(0, tmd, drain, 0)


def _dispatch(meta, pos_t, x1, g_ffn, n_rows, ne):
    t, d = x1.shape
    tmd = 256
    return pl.pallas_call(
        functools.partial(_dispatch_kernel, tmd=tmd, ne=ne),
        grid_spec=pltpu.PrefetchScalarGridSpec(
            num_scalar_prefetch=1,
            grid=(t // tmd,),
            in_specs=[
                pl.BlockSpec((TOP_K, tmd), lambda i, m: (0, i), memory_space=pltpu.SMEM),
                pl.BlockSpec((tmd, d), lambda i, m: (i, 0)),
                pl.BlockSpec((1, d), lambda i, m: (0, 0)),
            ],
            out_specs=pl.BlockSpec(memory_space=pl.ANY),
            scratch_shapes=[pltpu.VMEM((tmd, d), f32), pltpu.VMEM((ROW_PAD, d), f32),
                            pltpu.SemaphoreType.DMA, pltpu.SemaphoreType.DMA],
        ),
        out_shape=jax.ShapeDtypeStruct((n_rows, d), f32),
        compiler_params=_cparams(("arbitrary",)),
        name="dispatch",
    )(meta, pos_t, x1, g_ffn.reshape(1, d))


def _expert_kernel(meta, xs_ref, wgu_hbm, wd_hbm, bgu_ref, bd_ref, ys_ref,
                   x_scr, h_scr, wbuf, stage, y_scr, w_sem, ld_sem, st_sem,
                   *, nf, nd, tf, td):
    it = pl.program_id(0)
    n_items = meta[_M_NITEMS, 0]
    f_dim = wd_hbm.shape[1]

    def up_copies(e, f, slot):
        cols = pl.ds(pl.multiple_of(f * tf, tf), tf)
        cols_up = pl.ds(pl.multiple_of(f_dim + f * tf, tf), tf)
        return (pltpu.make_async_copy(wgu_hbm.at[e, :, cols], wbuf.at[slot, :, pl.ds(0, tf)], w_sem.at[slot]),
                pltpu.make_async_copy(wgu_hbm.at[e, :, cols_up], wbuf.at[slot, :, pl.ds(tf, tf)], w_sem.at[slot]))

    def down_copy(e, dcol, slot):
        cols = pl.ds(pl.multiple_of(dcol * td, td), td)
        return pltpu.make_async_copy(wd_hbm.at[e, :, cols], wbuf.at[slot, pl.ds(0, f_dim), :], w_sem.at[slot])

    def load_copy(first_row, r, slot):
        return pltpu.make_async_copy(xs_ref.at[pl.ds(first_row + r * ROW_PAD, ROW_PAD), :],
                                     stage.at[slot], ld_sem.at[slot])

    def rows_of(r):
        return pl.ds(pl.multiple_of(r * ROW_PAD, ROW_PAD), ROW_PAD)

    @pl.when(it < n_items)
    def _():
        e = meta[_M_EXPERT, it]
        row0 = pl.multiple_of(meta[_M_ROW0, it], ROW_PAD)
        nsub = meta[_M_NSUB, it]
        has_next = it + 1 < n_items
        nxt = jnp.minimum(it + 1, n_items - 1)
        next_row0 = pl.multiple_of(meta[_M_ROW0, nxt], ROW_PAD)
        next_nsub = meta[_M_NSUB, nxt]

        @pl.when(it == 0)
        def _():
            for c in up_copies(e, 0, 0):
                c.start()
            stage[0] = jnp.zeros(stage.shape[1:], f32)
            _zero_tail(meta, stage.at[0], ys_ref, ld_sem.at[0])
            load_copy(row0, 0, 0).start()

            def body(r, c):
                slot = r % 2
                load_copy(row0, r, slot).wait()

                @pl.when(r + 1 < nsub)
                def _():
                    load_copy(row0, r + 1, 1 - slot).start()

                x_scr[rows_of(r), :] = stage[slot].astype(bf16)
                return c

            lax.fori_loop(0, nsub, body, 0)

        def for_row_tiles(fn):
            per = COMPUTE_ROWS // ROW_PAD
            n_full = nsub // per

            def pair(r, c):
                first = pl.multiple_of(r * (2 * COMPUTE_ROWS), COMPUTE_ROWS)
                fn(first, COMPUTE_ROWS)
                fn(first + COMPUTE_ROWS, COMPUTE_ROWS)
                return c

            lax.fori_loop(0, n_full // 2, pair, 0)

            @pl.when(n_full % 2 == 1)
            def _():
                fn(pl.multiple_of((n_full - 1) * COMPUTE_ROWS, COMPUTE_ROWS), COMPUTE_ROWS)

            for tail in range(1, per):
                @pl.when(nsub % per == tail)
                def _():
                    fn(pl.multiple_of(n_full * COMPUTE_ROWS, COMPUTE_ROWS), tail * ROW_PAD)

        def up_chunk(f, carry):
            slot = f % 2
            for c in up_copies(e, f, slot):
                c.wait()

            @pl.when(f + 1 < nf)
            def _():
                for c in up_copies(e, f + 1, 1 - slot):
                    c.start()

            @pl.when(f + 1 == nf)
            def _():
                down_copy(e, 0, 1 - slot).start()

            bg = bgu_ref[f]
            bu = bgu_ref[nf + f]

            def up_tile(first, n):
                rows = pl.ds(first, n)
                gu = jnp.dot(x_scr[rows, :], wbuf[slot].astype(bf16), preferred_element_type=f32)
                g = jnp.minimum(gu[:, :tf] + bg, SWIGLU_LIMIT)
                u = jnp.clip(gu[:, tf:] + bu, -SWIGLU_LIMIT, SWIGLU_LIMIT)
                act = (u + 1.0) * (g * jax.nn.sigmoid(SWIGLU_ALPHA * g))
                h_scr[f, rows, :] = act.astype(bf16)

            for_row_tiles(up_tile)
            return carry

        lax.fori_loop(0, nf, up_chunk, 0)

        def store_copy(first, dcol, sl):
            dst = ys_ref.at[pl.ds(row0 + first, ROW_PAD), pl.ds(pl.multiple_of(dcol * td, td), td)]
            return pltpu.make_async_copy(y_scr.at[sl, pl.ds(first, ROW_PAD), :], dst, st_sem.at[sl])

        def wait_stores(sl, count):
            def w(r, c):
                store_copy(0, 0, sl).wait()
                return c
            lax.fori_loop(0, count, w, 0)

        def down_chunk(dcol, carry):
            slot = (nf + dcol) % 2
            yslot = dcol % 2
            down_copy(e, dcol, slot).wait()

            @pl.when(dcol + 1 < nd)
            def _():
                down_copy(e, dcol + 1, 1 - slot).start()

            @pl.when((dcol + 1 == nd) & has_next)
            def _():
                for c in up_copies(meta[_M_EXPERT, nxt], 0, 0):
                    c.start()

            bd = bd_ref[dcol]

            for j in range(stage.shape[0]):
                @pl.when(has_next & (dcol + j * nd < next_nsub))
                def _():
                    load_copy(next_row0, dcol + j * nd, j).start()

            @pl.when(dcol >= 2)
            def _():
                wait_stores(yslot, nsub)

            @pl.when((dcol < 2) & (it > 0))
            def _():
                wait_stores(yslot, meta[_M_NSUB, jnp.maximum(it - 1, 0)])

            def down_tile(first, n):
                rows = pl.ds(first, n)
                hidden = jnp.concatenate([h_scr[cf, rows, :] for cf in range(nf)], axis=1)
                y_scr[yslot, rows, :] = bd + jnp.dot(hidden, wbuf[slot, 0:f_dim, :].astype(bf16),
                                                     preferred_element_type=f32)
                for part in range(n // ROW_PAD):
                    store_copy(first + part * ROW_PAD, dcol, yslot).start()

            for_row_tiles(down_tile)

            for j in range(stage.shape[0]):
                @pl.when(has_next & (dcol + j * nd < next_nsub))
                def _():
                    load_copy(next_row0, dcol + j * nd, j).wait()
                    x_scr[rows_of(dcol + j * nd), :] = stage[j].astype(bf16)

            return carry

        lax.fori_loop(0, nd, down_chunk, 0)

        @pl.when(jnp.logical_not(has_next))
        def _():
            wait_stores(0, nsub)
            wait_stores(1, nsub)


def _experts(meta, xs, w_gate_up, b_gate_up, w_down, b_down):
    n_rows, d = xs.shape
    ne, _, f2 = w_gate_up.shape
    f = f2 // 2
    tf, td = 256, 512
    nf, nd = f // tf, d // td
    max_items = (n_rows + ne * (EXPERT_CAP - ROW_PAD)) // EXPERT_CAP
    stage_slots = max(2, pl.cdiv(EXPERT_CAP // ROW_PAD, nd))
    assert max_items <= MAX_ITEMS and td == 2 * tf and f <= d
    assert nf % 2 == 0 and nd % 2 == 0

    def expert(i, m):
        return m[_M_EXPERT, jnp.minimum(i, m[_M_NITEMS, 0] - 1)]

    return pl.pallas_call(
        functools.partial(_expert_kernel, nf=nf, nd=nd, tf=tf, td=td),
        grid_spec=pltpu.PrefetchScalarGridSpec(
            num_scalar_prefetch=1,
            grid=(max_items,),
            in_specs=[
                pl.BlockSpec(memory_space=pl.ANY),
                pl.BlockSpec(memory_space=pl.ANY),
                pl.BlockSpec(memory_space=pl.ANY),
                pl.BlockSpec((None, 2 * nf, 1, tf), lambda i, m: (expert(i, m), 0, 0, 0)),
                pl.BlockSpec((None, nd, 1, td), lambda i, m: (expert(i, m), 0, 0, 0)),
            ],
            out_specs=pl.BlockSpec(memory_space=pl.ANY),
            scratch_shapes=[
                pltpu.VMEM((EXPERT_CAP, d), bf16),
                pltpu.VMEM((nf, EXPERT_CAP, tf), bf16),
                pltpu.VMEM((2, d, 2 * tf), f32),
                pltpu.VMEM((stage_slots, ROW_PAD, d), f32),
                pltpu.VMEM((2, EXPERT_CAP, td), f32),
                pltpu.SemaphoreType.DMA((2,)),
                pltpu.SemaphoreType.DMA((stage_slots,)),
                pltpu.SemaphoreType.DMA((2,)),
            ],
        ),
        out_shape=jax.ShapeDtypeStruct((n_rows, d), f32),
        compiler_params=_cparams(("arbitrary",)),
        name="experts",
    )(meta, xs, w_gate_up, w_down, b_gate_up.reshape(ne, 2 * nf, 1, tf), b_down.reshape(ne, nd, 1, td))


def _combine_kernel(pos_cur, pos_nxt, gcol_ref, x_ref, ys_ref, o_ref, gbuf, sem, *, tmc, n_tiles):
    i = pl.program_id(0)
    slot = i % 2

    def row_copy(pos_ref, sl, r, k):
        return pltpu.make_async_copy(ys_ref.at[pl.ds(pos_ref[k, r], 1), :],
                                     gbuf.at[sl, k, pl.ds(r, 1), :], sem.at[sl])

    def issue(pos_ref, sl):
        def body(r, c):
            for k in range(TOP_K):
                row_copy(pos_ref, sl, r, k).start()
            return c
        lax.fori_loop(0, tmc, body, 0)

    @pl.when(i == 0)
    def _():
        issue(pos_cur, 0)

    @pl.when(i + 1 < n_tiles)
    def _():
        issue(pos_nxt, 1 - slot)

    def drain(r, c):
        for k in range(TOP_K):
            row_copy(pos_cur, slot, r, k).wait()
        return c

    lax.fori_loop(0, tmc, drain, 0)
    acc = x_ref[...]
    for k in range(TOP_K):
        acc = acc + gcol_ref[:, k:k + 1] * gbuf[slot, k]
    o_ref[...] = acc


def _combine(pos_t, gcol, x1, ys):
    t, d = x1.shape
    tmc = 128
    n_tiles = t // tmc
    return pl.pallas_call(
        functools.partial(_combine_kernel, tmc=tmc, n_tiles=n_tiles),
        grid=(n_tiles,),
        in_specs=[
            pl.BlockSpec((TOP_K, tmc), lambda i: (0, i), memory_space=pltpu.SMEM),
            pl.BlockSpec((TOP_K, tmc), lambda i: (0, jnp.minimum(i + 1, n_tiles - 1)), memory_space=pltpu.SMEM),
            pl.BlockSpec((tmc, LANES), lambda i: (i, 0)),
            pl.BlockSpec((tmc, d), lambda i: (i, 0)),
            pl.BlockSpec(memory_space=pl.ANY),
        ],
        out_specs=pl.BlockSpec((tmc, d), lambda i: (i, 0)),
        out_shape=jax.ShapeDtypeStruct((t, d), f32),
        scratch_shapes=[pltpu.VMEM((2, TOP_K, tmc, d), f32), pltpu.SemaphoreType.DMA((2,))],
        compiler_params=_cparams(("arbitrary",)),
        name="combine",
    )(pos_t, pos_t, gcol, x1, ys)


def _layer(x2, batch, seq, g_mix, w_in, b_forget, g_q, g_k, w_conv, g_attn_out, g_conv_out, w_out,
           g_ffn, w_router, b_router, w_gate_up, b_gate_up, w_down, b_down):
    t, d = x2.shape
    nheads = b_forget.shape[0]
    att_width = nheads * HEAD_DIM
    ne = w_router.shape[1]
    c0 = 3 * att_width
    w_t = w_in.T
    conv_width = (w_t.shape[0] - c0 - nheads) // 3

    h, c = _prenorm(x2, g_mix, w_t, c0, b_forget, seq)
    qkv = _qkv(h, w_t, g_q, g_k, att_width)
    conv = _conv(h, w_t, c0 + nheads, conv_width, w_conv, seq)
    att = _attention(qkv, c, batch, seq, nheads)
    x1 = _outproj(att, conv, g_attn_out, g_conv_out, w_out.astype(bf16), x2)

    pos_t, gcol, meta = _route(_router(x1, g_ffn, w_router, b_router))
    n_rows = t * TOP_K + ne * ROW_PAD
    xs = _dispatch(meta, pos_t, x1, g_ffn, n_rows, ne)
    ys = _experts(meta, xs, w_gate_up, b_gate_up, w_down, b_down)
    return _combine(pos_t, gcol, x1, ys)


def kernel(x, g_mix, w_in, b_forget, g_q, g_k, w_conv, g_attn_out, g_conv_out, w_out, g_ffn, w_router,
           b_router, w_gate_up, b_gate_up, w_down, b_down):
    b, s, d = x.shape
    x2 = x.reshape(b * s, d)
    for l in range(g_mix.shape[0]):
        x2 = _layer(x2, b, s, g_mix[l], w_in[l], b_forget[l], g_q[l], g_k[l], w_conv[l], g_attn_out[l],
                    g_conv_out[l], w_out[l], g_ffn[l], w_router[l], b_router[l], w_gate_up[l],
                    b_gate_up[l], w_down[l], b_down[l])
    return x2.reshape(b, s, d)
```

```python
import functools

import jax
import jax.numpy as jnp
import numpy as np
from jax import lax
from jax.experimental import pallas as pl
from jax.experimental.pallas import tpu as pltpu

f32 = jnp.float32
bf16 = jnp.bfloat16
i32 = jnp.int32

HEAD_DIM = 128
LANES = 128
SUBLANES = 8
TOP_K = 4
RMS_EPS = 1e-6
SWIGLU_ALPHA = 1.702
SWIGLU_LIMIT = 7.0
VMEM_LIMIT_BYTES = 56 * 1024 * 1024

ROW_PAD = 128
EXPERT_CAP = 1152
COMPUTE_ROWS = 512
MAX_ITEMS = 128


def _cparams(sem):
    return pltpu.CompilerParams(dimension_semantics=sem, vmem_limit_bytes=VMEM_LIMIT_BYTES)


def _log_sigmoid(z):
    return jnp.minimum(z, 0.0) - jnp.log1p(jnp.exp(-jnp.abs(z)))


def _split3(a):
    hi = a.astype(bf16)
    r1 = a - hi.astype(f32)
    mid = r1.astype(bf16)
    lo = (r1 - mid.astype(f32)).astype(bf16)
    return hi, mid, lo


def _dot_nt(a, b):
    return lax.dot_general(a, b, (((1,), (1,)), ((), ())), preferred_element_type=f32)


def _rms(y, g):
    ms = jnp.mean(y * y, axis=-1, keepdims=True)
    return y * lax.rsqrt(ms + RMS_EPS) * g


def _prenorm_kernel(x_ref, g_ref, wf_ref, bf_ref, h_ref, c_ref, carry, *, tiles_per_seq):
    i = pl.program_id(0)
    hb = _rms(x_ref[...], g_ref[...]).astype(bf16)
    h_ref[...] = hb
    logf = _log_sigmoid(_dot_nt(hb, wf_ref[...].astype(bf16)) + bf_ref[...])
    tm = hb.shape[0]
    a = lax.broadcasted_iota(i32, (tm, tm), 0)
    b = lax.broadcasted_iota(i32, (tm, tm), 1)
    lower = (b <= a).astype(bf16)
    c = sum(jnp.dot(lower, p, preferred_element_type=f32) for p in _split3(logf))

    @pl.when(i % tiles_per_seq == 0)
    def _():
        carry[...] = jnp.zeros_like(carry)

    c = c + carry[...]
    c_ref[...] = c
    carry[...] = c[tm - 1:tm, :]


def _prenorm(x2, g_mix, w_t, f_row0, b_f, seq):
    t, d = x2.shape
    nh = b_f.shape[0]
    tm = 512
    assert f_row0 % LANES == 0 and nh <= LANES
    bfr = jnp.pad(b_f, (0, LANES - nh)).reshape(1, LANES)
    return pl.pallas_call(
        functools.partial(_prenorm_kernel, tiles_per_seq=seq // tm),
        grid=(t // tm,),
        in_specs=[
            pl.BlockSpec((tm, d), lambda i: (i, 0)),
            pl.BlockSpec((1, d), lambda i: (0, 0)),
            pl.BlockSpec((LANES, d), lambda i: (f_row0 // LANES, 0)),
            pl.BlockSpec((1, LANES), lambda i: (0, 0)),
        ],
        out_specs=[
            pl.BlockSpec((tm, d), lambda i: (i, 0)),
            pl.BlockSpec((tm, LANES), lambda i: (i, 0)),
        ],
        out_shape=[
            jax.ShapeDtypeStruct((t, d), bf16),
            jax.ShapeDtypeStruct((t, LANES), f32),
        ],
        scratch_shapes=[pltpu.VMEM((1, LANES), f32)],
        compiler_params=_cparams(("arbitrary",)),
        name="prenorm",
    )(x2, g_mix.reshape(1, d), w_t, bfr)


def _qkv_kernel(h_ref, w_ref, g_ref, post_ref, o_ref, *, n_norm_blocks):
    normed = pl.program_id(0) < n_norm_blocks
    tm, tn = o_ref.shape
    for half in range(2):
        rows = slice(half * (tm // 2), (half + 1) * (tm // 2))
        y = _dot_nt(h_ref[rows, :], w_ref[...].astype(bf16))
        for hh in range(tn // HEAD_DIM):
            sl = slice(hh * HEAD_DIM, (hh + 1) * HEAD_DIM)
            yh = y[:, sl]
            inv = lax.rsqrt(jnp.mean(yh * yh, axis=-1, keepdims=True) + RMS_EPS)
            o_ref[rows, sl] = (yh * jnp.where(normed, inv, 1.0) * g_ref[:, sl] * post_ref[:, sl]).astype(bf16)


def _qkv(h, w_t, g_q, g_k, att_width):
    t, d = h.shape
    n = 3 * att_width
    tm, tn = 1024, 512
    nheads = att_width // HEAD_DIM
    ones = jnp.ones((att_width,), f32)
    gain = jnp.concatenate([jnp.tile(g_q, nheads), jnp.tile(g_k, nheads), ones]).reshape(1, n)
    post = jnp.concatenate([ones * (1.0 / np.sqrt(HEAD_DIM)), ones, ones]).reshape(1, n)
    return pl.pallas_call(
        functools.partial(_qkv_kernel, n_norm_blocks=2 * att_width // tn),
        grid=(n // tn, t // tm),
        in_specs=[
            pl.BlockSpec((tm, d), lambda j, i: (i, 0)),
            pl.BlockSpec((tn, d), lambda j, i: (j, 0)),
            pl.BlockSpec((1, tn), lambda j, i: (0, j)),
            pl.BlockSpec((1, tn), lambda j, i: (0, j)),
        ],
        out_specs=pl.BlockSpec((tm, tn), lambda j, i: (i, j)),
        out_shape=jax.ShapeDtypeStruct((t, n), bf16),
        compiler_params=_cparams(("arbitrary", "arbitrary")),
        name="qkv",
    )(h, w_t, gain, post)


def _conv_kernel(h_ref, wb_ref, wc_ref, wx_ref, wk_ref, o_ref, carry, *, tiles_per_seq):
    i = pl.program_id(1)
    h = h_ref[...]
    gb = _dot_nt(h, wb_ref[...].astype(bf16))
    gc = _dot_nt(h, wc_ref[...].astype(bf16))
    xc = _dot_nt(h, wx_ref[...].astype(bf16))
    u = gc * xc
    tm = u.shape[0]

    @pl.when(i % tiles_per_seq == 0)
    def _():
        carry[...] = jnp.zeros_like(carry)

    p2 = carry[SUBLANES - 2:SUBLANES - 1, :]
    p1 = carry[SUBLANES - 1:SUBLANES, :]
    row = lax.broadcasted_iota(i32, u.shape, 0)
    u1 = jnp.where(row == 0, p1, pltpu.roll(u, 1, 0))
    u2 = jnp.where(row == 0, p2, jnp.where(row == 1, p1, pltpu.roll(u, 2, 0)))
    y = wk_ref[0:1, :] * u2 + wk_ref[1:2, :] * u1 + wk_ref[2:3, :] * u
    o_ref[...] = gb * y
    carry[...] = u[tm - SUBLANES:tm, :]


def _conv(h, w_t, row0, width, w_conv, seq):
    t, d = h.shape
    tm, tn = 1024, 256
    assert row0 % SUBLANES == 0 and width % tn == 0

    def w_spec(part):
        return pl.BlockSpec((pl.Element(tn), pl.Element(d)),
                            lambda j, i: (pl.multiple_of(row0 + part * width + j * tn, SUBLANES), 0))

    return pl.pallas_call(
        functools.partial(_conv_kernel, tiles_per_seq=seq // tm),
        grid=(width // tn, t // tm),
        in_specs=[
            pl.BlockSpec((tm, d), lambda j, i: (i, 0)),
            w_spec(0), w_spec(1), w_spec(2),
            pl.BlockSpec((3, tn), lambda j, i: (0, j)),
        ],
        out_specs=pl.BlockSpec((tm, tn), lambda j, i: (i, j)),
        out_shape=jax.ShapeDtypeStruct((t, width), f32),
        scratch_shapes=[pltpu.VMEM((SUBLANES, tn), f32)],
        compiler_params=_cparams(("arbitrary", "arbitrary")),
        name="conv",
    )(h, w_t, w_t, w_t, w_conv)


def _bias_lanes(c, ones_first):
    n = c.shape[0]
    lane = lax.broadcasted_iota(i32, (n, LANES), 1)
    out = jnp.zeros((n, LANES), f32)
    base_c, base_1 = (3, 0) if ones_first else (0, 3)
    for k, piece in enumerate(_split3(c)):
        out = jnp.where(lane == base_c + k, piece.astype(f32), out)
        out = jnp.where(lane == base_1 + k, 1.0, out)
    return out.astype(bf16)


def _attn_kernel(q_ref, k_ref, v_ref, cq_ref, ck_ref, o_ref, k_aug, *, tq):
    hd = pl.program_id(1)
    qi = pl.program_id(2)

    def head_column(c_ref):
        lane = lax.broadcasted_iota(i32, c_ref.shape, 1)
        return jnp.sum(jnp.where(lane == hd, c_ref[...], 0.0), axis=1, keepdims=True)

    @pl.when(qi == 0)
    def _():
        k_aug[:, :HEAD_DIM] = k_ref[...]
        k_aug[:, HEAD_DIM:] = _bias_lanes(-head_column(ck_ref), ones_first=True)

    q = jnp.concatenate([q_ref[...], _bias_lanes(head_column(cq_ref), ones_first=False)], axis=1)
    def rows_of(j):
        return pl.ds(pl.multiple_of(j * tq, tq), tq)

    def scores(j):
        return _dot_nt(q, k_aug[rows_of(j), :])

    def update(s, vj, state):
        m, l, acc = state
        m_new = jnp.maximum(m, jnp.max(s, axis=1, keepdims=True))
        alpha = jnp.exp(m - m_new)
        p = jnp.exp(s - m_new)
        l = alpha * l + jnp.sum(p, axis=1, keepdims=True)
        acc = alpha * acc + jnp.dot(p.astype(bf16), vj, preferred_element_type=f32)
        return m_new, l, acc

    def body(j, state):
        return update(scores(j), v_ref[rows_of(j), :], state)

    init = (jnp.full((tq, 1), -jnp.inf, f32), jnp.zeros((tq, 1), f32), jnp.zeros((tq, HEAD_DIM), f32))
    state = lax.fori_loop(0, qi, body, init)
    r = lax.broadcasted_iota(i32, (tq, tq), 0)
    c = lax.broadcasted_iota(i32, (tq, tq), 1)
    _, l, acc = update(jnp.where(c <= r, scores(qi), -jnp.inf), v_ref[rows_of(qi), :], state)
    o_ref[...] = acc / l


def _attention(qkv, c, batch, seq, nheads):
    t = qkv.shape[0]
    tq = 512
    nq = seq // tq
    return pl.pallas_call(
        functools.partial(_attn_kernel, tq=tq),
        grid=(batch, nheads, nq),
        in_specs=[
            pl.BlockSpec((tq, HEAD_DIM), lambda b, h, qi: (b * nq + qi, h)),
            pl.BlockSpec((seq, HEAD_DIM), lambda b, h, qi: (b, nheads + h)),
            pl.BlockSpec((seq, HEAD_DIM), lambda b, h, qi: (b, 2 * nheads + h)),
            pl.BlockSpec((tq, LANES), lambda b, h, qi: (b * nq + qi, 0)),
            pl.BlockSpec((seq, LANES), lambda b, h, qi: (b, 0)),
        ],
        out_specs=pl.BlockSpec((tq, HEAD_DIM), lambda b, h, qi: (b * nq + qi, h)),
        out_shape=jax.ShapeDtypeStruct((t, nheads * HEAD_DIM), f32),
        scratch_shapes=[pltpu.VMEM((seq, 2 * HEAD_DIM), bf16)],
        compiler_params=_cparams(("arbitrary", "arbitrary", "arbitrary")),
        name="attention",
    )(qkv, qkv, qkv, c, c)


def _outproj_kernel(att_ref, conv_ref, ga_ref, gc_ref, w_ref, x_ref, o_ref, mixed):
    j = pl.program_id(1)
    wa = att_ref.shape[1]

    @pl.when(j == 0)
    def _():
        mixed[:, :wa] = _rms(att_ref[...], ga_ref[...]).astype(bf16)
        mixed[:, wa:] = _rms(conv_ref[...], gc_ref[...]).astype(bf16)

    o_ref[...] = x_ref[...] + jnp.dot(mixed[...], w_ref[...], preferred_element_type=f32)


def _outproj(att, conv, g_a, g_c, w_o, x2):
    t, d = x2.shape
    wa, wc = att.shape[1], conv.shape[1]
    tm, tn = 512, 1024
    return pl.pallas_call(
        _outproj_kernel,
        grid=(t // tm, d // tn),
        in_specs=[
            pl.BlockSpec((tm, wa), lambda i, j: (i, 0)),
            pl.BlockSpec((tm, wc), lambda i, j: (i, 0)),
            pl.BlockSpec((1, wa), lambda i, j: (0, 0)),
            pl.BlockSpec((1, wc), lambda i, j: (0, 0)),
            pl.BlockSpec((wa + wc, tn), lambda i, j: (0, j)),
            pl.BlockSpec((tm, tn), lambda i, j: (i, j)),
        ],
        out_specs=pl.BlockSpec((tm, tn), lambda i, j: (i, j)),
        out_shape=jax.ShapeDtypeStruct((t, d), f32),
        scratch_shapes=[pltpu.VMEM((tm, wa + wc), bf16)],
        compiler_params=_cparams(("arbitrary", "arbitrary")),
        name="outproj",
    )(att, conv, g_a.reshape(1, wa), g_c.reshape(1, wc), w_o, x2)


def _router_kernel(x_ref, g_ref, wrt_ref, br_ref, lg_ref):
    h_hi, h_lo, _ = _split3(_rms(x_ref[...], g_ref[...]))
    w_hi, w_lo, _ = _split3(wrt_ref[...])
    lg_ref[...] = _dot_nt(w_hi, h_hi) + (_dot_nt(w_lo, h_hi) + _dot_nt(w_hi, h_lo)) + br_ref[...]


def _router(x1, g_ffn, w_router, b_router):
    t, d = x1.shape
    ne = w_router.shape[1]
    tm = 512
    return pl.pallas_call(
        _router_kernel,
        grid=(t // tm,),
        in_specs=[
            pl.BlockSpec((tm, d), lambda i: (i, 0)),
            pl.BlockSpec((1, d), lambda i: (0, 0)),
            pl.BlockSpec((ne, d), lambda i: (0, 0)),
            pl.BlockSpec((ne, 1), lambda i: (0, 0)),
        ],
        out_specs=pl.BlockSpec((ne, tm), lambda i: (0, i)),
        out_shape=jax.ShapeDtypeStruct((ne, t), f32),
        compiler_params=_cparams(("arbitrary",)),
        name="router",
    )(x1, g_ffn.reshape(1, d), w_router.T, b_router.reshape(ne, 1))


def _cumsum_sublanes(x):
    n = x.shape[0]
    row = lax.broadcasted_iota(i32, x.shape, 0)
    d = 1
    while d < n:
        x = x + jnp.where(row >= d, pltpu.roll(x, d, 0), 0.0)
        d *= 2
    return x


def _stack_rows(rows, n):
    width = rows[0].shape[1]
    sub = lax.broadcasted_iota(i32, (n, width), 0)
    out = jnp.zeros((n, width), rows[0].dtype)
    for k, r in enumerate(rows):
        out = jnp.where(sub == k, r, out)
    return out


def _route_kernel(lg_ref, pos_ref, gcol_ref, meta_ref, esel, rsel, *, chunk):
    ne, t = lg_ref.shape
    e_iota = lax.broadcasted_iota(i32, (ne, chunk), 0).astype(f32)
    a = lax.broadcasted_iota(i32, (chunk, chunk), 0)
    b = lax.broadcasted_iota(i32, (chunk, chunk), 1)
    before = (a < b).astype(bf16)
    counts = jnp.zeros((ne, 1), f32)
    for c in range(t // chunk):
        sl = slice(c * chunk, (c + 1) * chunk)
        vals = lg_ref[:, sl]
        tops, idxs, hots = [], [], []
        for _ in range(TOP_K):
            m = jnp.max(vals, axis=0, keepdims=True)
            idx = jnp.min(jnp.where(vals == m, e_iota, ne), axis=0, keepdims=True)
            hot = e_iota == idx
            vals = jnp.where(hot, -jnp.inf, vals)
            tops.append(m)
            idxs.append(idx)
            hots.append(hot)
        ex = [jnp.exp(v - tops[0]) for v in tops]
        den = ex[0] + ex[1] + ex[2] + ex[3]
        gates = [e / den for e in ex]
        member = sum(h.astype(f32) for h in hots)
        rank = jnp.dot(member.astype(bf16), before, preferred_element_type=f32) + counts
        counts = counts + jnp.sum(member, axis=1, keepdims=True)
        for k in range(TOP_K):
            rsel[k:k + 1, sl] = jnp.sum(jnp.where(hots[k], rank, 0.0), axis=0, keepdims=True)
            esel[k:k + 1, sl] = idxs[k]
        g8 = _stack_rows(gates, SUBLANES)
        gpad = jnp.concatenate([g8, jnp.zeros((LANES - SUBLANES, chunk), f32)], axis=0)
        gcol_ref[sl, :] = gpad.T

    cnt = jnp.broadcast_to(counts, (ne, LANES))
    pcnt = jnp.ceil(cnt * (1.0 / ROW_PAD)) * ROW_PAD
    pend = _cumsum_sublanes(pcnt)
    pstart = pend - pcnt
    npass = jnp.floor((pcnt + (EXPERT_CAP - ROW_PAD)) / EXPERT_CAP + 0.5 * ROW_PAD / EXPERT_CAP)
    iend = _cumsum_sublanes(npass)
    istart = iend - npass
    n_items = jnp.max(iend, axis=0, keepdims=True)

    pstart_col = pstart[:, 0:1]
    for c in range(t // chunk):
        sl = slice(c * chunk, (c + 1) * chunk)
        for k in range(TOP_K):
            hot = e_iota == esel[k:k + 1, sl]
            ps = jnp.sum(jnp.where(hot, pstart_col, 0.0), axis=0, keepdims=True)
            pos_ref[k:k + 1, sl] = (rsel[k:k + 1, sl] + ps).astype(i32)

    e_sub = lax.broadcasted_iota(i32, (ne, LANES), 0)
    lane = lax.broadcasted_iota(i32, (ne, LANES), 1)
    lane_f = lane.astype(f32)
    item = lane_f[0:1, :]
    e_of = jnp.minimum(jnp.sum((iend <= lane_f).astype(f32), axis=0, keepdims=True), ne - 1.0)
    hot = e_sub.astype(f32) == e_of
    pick = lambda v: jnp.sum(jnp.where(hot, v, 0.0), axis=0, keepdims=True)
    p_i = item - pick(istart)
    row0 = pick(pstart) + p_i * EXPERT_CAP
    nrows = jnp.clip(pick(pcnt) - p_i * EXPERT_CAP, 0.0, float(EXPERT_CAP))
    nsub = nrows * (1.0 / ROW_PAD)
    diag = e_sub == lane
    zrow = jnp.sum(jnp.where(diag, pend - ROW_PAD, 0.0), axis=0, keepdims=True)
    zval = jnp.sum(jnp.where(diag, (cnt > 0).astype(f32), 0.0), axis=0, keepdims=True)
    total = jnp.max(pend, axis=0, keepdims=True)
    rows = [e_of, row0, nsub, n_items, zrow, zval, total]
    meta_ref[...] = _stack_rows([r.astype(i32) for r in rows], SUBLANES)


def _route(logits_t):
    ne, t = logits_t.shape
    chunk = 512
    return pl.pallas_call(
        functools.partial(_route_kernel, chunk=chunk),
        out_shape=[
            jax.ShapeDtypeStruct((TOP_K, t), i32),
            jax.ShapeDtypeStruct((t, LANES), f32),
            jax.ShapeDtypeStruct((SUBLANES, LANES), i32),
        ],
        scratch_shapes=[pltpu.VMEM((TOP_K, t), f32), pltpu.VMEM((TOP_K, t), f32)],
        compiler_params=pltpu.CompilerParams(vmem_limit_bytes=VMEM_LIMIT_BYTES),
        name="route",
    )(logits_t)


_M_EXPERT, _M_ROW0, _M_NSUB, _M_NITEMS, _M_ZROW, _M_ZVALID, _M_TOTAL = range(7)


def _zero_tail(meta, zbuf, dst_ref, sem):
    total = pl.multiple_of(meta[_M_TOTAL, 0], ROW_PAD)
    n = (dst_ref.shape[0] - total) // ROW_PAD

    def copy(r):
        return pltpu.make_async_copy(zbuf, dst_ref.at[pl.ds(total + r * ROW_PAD, ROW_PAD), :], sem)

    def start(r, c):
        copy(r).start()
        return c

    def wait(r, c):
        copy(r).wait()
        return c

    lax.fori_loop(0, n, start, 0)
    lax.fori_loop(0, n, wait, 0)


def _dispatch_kernel(meta, pos_ref, x_ref, g_ref, xs_ref, h_scr, zbuf, sem, zsem, *, tmd, ne):
    i = pl.program_id(0)
    h_scr[...] = _rms(x_ref[...], g_ref[...])

    def zero_copy(e):
        start = pl.multiple_of(meta[_M_ZROW, e], ROW_PAD)
        return pltpu.make_async_copy(zbuf, xs_ref.at[pl.ds(start, ROW_PAD), :], zsem)

    @pl.when(i == 0)
    def _():
        zbuf[...] = jnp.zeros_like(zbuf)

        def zstart(e, c):
            @pl.when(meta[_M_ZVALID, e] > 0)
            def _():
                zero_copy(e).start()
            return c

        def zwait(e, c):
            @pl.when(meta[_M_ZVALID, e] > 0)
            def _():
                zero_copy(e).wait()
            return c

        lax.fori_loop(0, ne, zstart, 0)
        lax.fori_loop(0, ne, zwait, 0)
        _zero_tail(meta, zbuf, xs_ref, zsem)

    def row_copy(r, k):
        return pltpu.make_async_copy(h_scr.at[pl.ds(r, 1), :],
                                     xs_ref.at[pl.ds(pos_ref[k, r], 1), :], sem)

    def issue(r, c):
        for k in range(TOP_K):
            row_copy(r, k).start()
        return c

    def drain(r, c):
        for k in range(TOP_K):
            row_copy(r, k).wait()
        return c

    lax.fori_loop(0, tmd, issue, 0)
    lax.fori_loop(0, tmd, drain, 0)


def _dispatch(meta, pos_t, x1, g_ffn, n_rows, ne):
    t, d = x1.shape
    tmd = 256
    return pl.pallas_call(
        functools.partial(_dispatch_kernel, tmd=tmd, ne=ne),
        grid_spec=pltpu.PrefetchScalarGridSpec(
            num_scalar_prefetch=1,
            grid=(t // tmd,),
            in_specs=[
                pl.BlockSpec((TOP_K, tmd), lambda i, m: (0, i), memory_space=pltpu.SMEM),
                pl.BlockSpec((tmd, d), lambda i, m: (i, 0)),
                pl.BlockSpec((1, d), lambda i, m: (0, 0)),
            ],
            out_specs=pl.BlockSpec(memory_space=pl.ANY),
            scratch_shapes=[pltpu.VMEM((tmd, d), f32), pltpu.VMEM((ROW_PAD, d), f32),
                            pltpu.SemaphoreType.DMA, pltpu.SemaphoreType.DMA],
        ),
        out_shape=jax.ShapeDtypeStruct((n_rows, d), f32),
        compiler_params=_cparams(("arbitrary",)),
        name="dispatch",
    )(meta, pos_t, x1, g_ffn.reshape(1, d))


def _expert_kernel(meta, xs_ref, wgu_hbm, wd_hbm, bgu_ref, bd_ref, ys_ref,
                   x_scr, h_scr, wbuf, stage, y_scr, w_sem, ld_sem, st_sem,
                   *, nf, nd, tf, td):
    it = pl.program_id(0)
    n_items = meta[_M_NITEMS, 0]
    f_dim = wd_hbm.shape[1]

    def up_copies(e, f, slot):
        cols = pl.ds(pl.multiple_of(f * tf, tf), tf)
        cols_up = pl.ds(pl.multiple_of(f_dim + f * tf, tf), tf)
        return (pltpu.make_async_copy(wgu_hbm.at[e, :, cols], wbuf.at[slot, :, pl.ds(0, tf)], w_sem.at[slot]),
                pltpu.make_async_copy(wgu_hbm.at[e, :, cols_up], wbuf.at[slot, :, pl.ds(tf, tf)], w_sem.at[slot]))

    def down_copy(e, dcol, slot):
        cols = pl.ds(pl.multiple_of(dcol * td, td), td)
        return pltpu.make_async_copy(wd_hbm.at[e, :, cols], wbuf.at[slot, pl.ds(0, f_dim), :], w_sem.at[slot])

    def load_copy(first_row, r, slot):
        return pltpu.make_async_copy(xs_ref.at[pl.ds(first_row + r * ROW_PAD, ROW_PAD), :],
                                     stage.at[slot], ld_sem.at[slot])

    def rows_of(r):
        return pl.ds(pl.multiple_of(r * ROW_PAD, ROW_PAD), ROW_PAD)

    @pl.when(it < n_items)
    def _():
        e = meta[_M_EXPERT, it]
        row0 = pl.multiple_of(meta[_M_ROW0, it], ROW_PAD)
        nsub = meta[_M_NSUB, it]
        has_next = it + 1 < n_items
        nxt = jnp.minimum(it + 1, n_items - 1)
        next_row0 = pl.multiple_of(meta[_M_ROW0, nxt], ROW_PAD)
        next_nsub = meta[_M_NSUB, nxt]
        xcur = it % 2

        @pl.when(it == 0)
        def _():
            for c in up_copies(e, 0, 0):
                c.start()
            stage[0] = jnp.zeros(stage.shape[1:], f32)
            _zero_tail(meta, stage.at[0], ys_ref, ld_sem.at[0])
            load_copy(row0, 0, 0).start()

            def body(r, c):
                slot = r % 2
                load_copy(row0, r, slot).wait()

                @pl.when(r + 1 < nsub)
                def _():
                    load_copy(row0, r + 1, 1 - slot).start()

                x_scr[0, rows_of(r), :] = stage[slot].astype(bf16)
                return c

            lax.fori_loop(0, nsub, body, 0)

        def for_row_tiles(fn):
            per = COMPUTE_ROWS // ROW_PAD
            n_full = nsub // per

            def pair(r, c):
                first = pl.multiple_of(r * (2 * COMPUTE_ROWS), COMPUTE_ROWS)
                fn(first, COMPUTE_ROWS)
                fn(first + COMPUTE_ROWS, COMPUTE_ROWS)
                return c

            lax.fori_loop(0, n_full // 2, pair, 0)

            @pl.when(n_full % 2 == 1)
            def _():
                fn(pl.multiple_of((n_full - 1) * COMPUTE_ROWS, COMPUTE_ROWS), COMPUTE_ROWS)

            for tail in range(1, per):
                @pl.when(nsub % per == tail)
                def _():
                    fn(pl.multiple_of(n_full * COMPUTE_ROWS, COMPUTE_ROWS), tail * ROW_PAD)

        def up_chunk(f, carry):
            slot = f % 2
            for c in up_copies(e, f, slot):
                c.wait()

            @pl.when(f + 1 < nf)
            def _():
                for c in up_copies(e, f + 1, 1 - slot):
                    c.start()

            @pl.when(f + 1 == nf)
            def _():
                down_copy(e, 0, 1 - slot).start()

            bg = bgu_ref[f]
            bu = bgu_ref[nf + f]

            for j in range(stage.shape[0]):
                @pl.when(has_next & (f + j * nf < next_nsub))
                def _():
                    load_copy(next_row0, f + j * nf, j).start()

            def up_tile(first, n):
                rows = pl.ds(first, n)
                gu = jnp.dot(x_scr[xcur, rows, :], wbuf[slot].astype(bf16), preferred_element_type=f32)
                g = jnp.minimum(gu[:, :tf] + bg, SWIGLU_LIMIT)
                u = jnp.clip(gu[:, tf:] + bu, -SWIGLU_LIMIT, SWIGLU_LIMIT)
                act = (u + 1.0) * (g * jax.nn.sigmoid(SWIGLU_ALPHA * g))
                h_scr[f, rows, :] = act.astype(bf16)

            for_row_tiles(up_tile)

            for j in range(stage.shape[0]):
                @pl.when(has_next & (f + j * nf < next_nsub))
                def _():
                    load_copy(next_row0, f + j * nf, j).wait()
                    x_scr[1 - xcur, rows_of(f + j * nf), :] = stage[j].astype(bf16)

            return carry

        lax.fori_loop(0, nf, up_chunk, 0)

        def store_copy(first, dcol, sl):
            dst = ys_ref.at[pl.ds(row0 + first, ROW_PAD), pl.ds(pl.multiple_of(dcol * td, td), td)]
            return pltpu.make_async_copy(y_scr.at[sl, pl.ds(first, ROW_PAD), :], dst, st_sem.at[sl])

        def wait_stores(sl, count):
            def w(r, c):
                store_copy(0, 0, sl).wait()
                return c
            lax.fori_loop(0, count, w, 0)

        def down_chunk(dcol, carry):
            slot = (nf + dcol) % 2
            yslot = dcol % 2
            down_copy(e, dcol, slot).wait()

            @pl.when(dcol + 1 < nd)
            def _():
                down_copy(e, dcol + 1, 1 - slot).start()

            @pl.when((dcol + 1 == nd) & has_next)
            def _():
                for c in up_copies(meta[_M_EXPERT, nxt], 0, 0):
                    c.start()

            bd = bd_ref[dcol]

            @pl.when(dcol >= 2)
            def _():
                wait_stores(yslot, nsub)

            @pl.when((dcol < 2) & (it > 0))
            def _():
                wait_stores(yslot, meta[_M_NSUB, jnp.maximum(it - 1, 0)])

            def down_tile(first, n):
                rows = pl.ds(first, n)
                hidden = jnp.concatenate([h_scr[cf, rows, :] for cf in range(nf)], axis=1)
                y_scr[yslot, rows, :] = bd + jnp.dot(hidden, wbuf[slot, 0:f_dim, :].astype(bf16),
                                                     preferred_element_type=f32)
                for part in range(n // ROW_PAD):
                    store_copy(first + part * ROW_PAD, dcol, yslot).start()

            for_row_tiles(down_tile)

            return carry

        lax.fori_loop(0, nd, down_chunk, 0)

        @pl.when(jnp.logical_not(has_next))
        def _():
            wait_stores(0, nsub)
            wait_stores(1, nsub)


def _experts(meta, xs, w_gate_up, b_gate_up, w_down, b_down):
    n_rows, d = xs.shape
    ne, _, f2 = w_gate_up.shape
    f = f2 // 2
    tf, td = 256, 512
    nf, nd = f // tf, d // td
    max_items = (n_rows + ne * (EXPERT_CAP - ROW_PAD)) // EXPERT_CAP
    stage_slots = max(2, pl.cdiv(EXPERT_CAP // ROW_PAD, nf))
    assert max_items <= MAX_ITEMS and td == 2 * tf and f <= d
    assert nf % 2 == 0 and nd % 2 == 0

    def expert(i, m):
        return m[_M_EXPERT, jnp.minimum(i, m[_M_NITEMS, 0] - 1)]

    return pl.pallas_call(
        functools.partial(_expert_kernel, nf=nf, nd=nd, tf=tf, td=td),
        grid_spec=pltpu.PrefetchScalarGridSpec(
            num_scalar_prefetch=1,
            grid=(max_items,),
            in_specs=[
                pl.BlockSpec(memory_space=pl.ANY),
                pl.BlockSpec(memory_space=pl.ANY),
                pl.BlockSpec(memory_space=pl.ANY),
                pl.BlockSpec((None, 2 * nf, 1, tf), lambda i, m: (expert(i, m), 0, 0, 0)),
                pl.BlockSpec((None, nd, 1, td), lambda i, m: (expert(i, m), 0, 0, 0)),
            ],
            out_specs=pl.BlockSpec(memory_space=pl.ANY),
            scratch_shapes=[
                pltpu.VMEM((2, EXPERT_CAP, d), bf16),
                pltpu.VMEM((nf, EXPERT_CAP, tf), bf16),
                pltpu.VMEM((2, d, 2 * tf), f32),
                pltpu.VMEM((stage_slots, ROW_PAD, d), f32),
                pltpu.VMEM((2, EXPERT_CAP, td), f32),
                pltpu.SemaphoreType.DMA((2,)),
                pltpu.SemaphoreType.DMA((stage_slots,)),
                pltpu.SemaphoreType.DMA((2,)),
            ],
        ),
        out_shape=jax.ShapeDtypeStruct((n_rows, d), f32),
        compiler_params=_cparams(("arbitrary",)),
        name="experts",
    )(meta, xs, w_gate_up, w_down, b_gate_up.reshape(ne, 2 * nf, 1, tf), b_down.reshape(ne, nd, 1, td))


def _combine_kernel(pos_cur, pos_nxt, gcol_ref, x_ref, ys_ref, o_ref, gbuf, sem, *, tmc, n_tiles):
    i = pl.program_id(0)
    slot = i % 2

    def row_copy(pos_ref, sl, r, k):
        return pltpu.make_async_copy(ys_ref.at[pl.ds(pos_ref[k, r], 1), :],
                                     gbuf.at[sl, k, pl.ds(r, 1), :], sem.at[sl])

    def issue(pos_ref, sl):
        def body(r, c):
            for k in range(TOP_K):
                row_copy(pos_ref, sl, r, k).start()
            return c
        lax.fori_loop(0, tmc, body, 0)

    @pl.when(i == 0)
    def _():
        issue(pos_cur, 0)

    @pl.when(i + 1 < n_tiles)
    def _():
        issue(pos_nxt, 1 - slot)

    def drain(r, c):
        for k in range(TOP_K):
            row_copy(pos_cur, slot, r, k).wait()
        return c

    lax.fori_loop(0, tmc, drain, 0)
    acc = x_ref[...]
    for k in range(TOP_K):
        acc = acc + gcol_ref[:, k:k + 1] * gbuf[slot, k]
    o_ref[...] = acc


def _combine(pos_t, gcol, x1, ys):
    t, d = x1.shape
    tmc = 128
    n_tiles = t // tmc
    return pl.pallas_call(
        functools.partial(_combine_kernel, tmc=tmc, n_tiles=n_tiles),
        grid=(n_tiles,),
        in_specs=[
            pl.BlockSpec((TOP_K, tmc), lambda i: (0, i), memory_space=pltpu.SMEM),
            pl.BlockSpec((TOP_K, tmc), lambda i: (0, jnp.minimum(i + 1, n_tiles - 1)), memory_space=pltpu.SMEM),
            pl.BlockSpec((tmc, LANES), lambda i: (i, 0)),
            pl.BlockSpec((tmc, d), lambda i: (i, 0)),
            pl.BlockSpec(memory_space=pl.ANY),
        ],
        out_specs=pl.BlockSpec((tmc, d), lambda i: (i, 0)),
        out_shape=jax.ShapeDtypeStruct((t, d), f32),
        scratch_shapes=[pltpu.VMEM((2, TOP_K, tmc, d), f32), pltpu.SemaphoreType.DMA((2,))],
        compiler_params=_cparams(("arbitrary",)),
        name="combine",
    )(pos_t, pos_t, gcol, x1, ys)


def _layer(x2, batch, seq, g_mix, w_in, b_forget, g_q, g_k, w_conv, g_attn_out, g_conv_out, w_out,
           g_ffn, w_router, b_router, w_gate_up, b_gate_up, w_down, b_down):
    t, d = x2.shape
    nheads = b_forget.shape[0]
    att_width = nheads * HEAD_DIM
    ne = w_router.shape[1]
    c0 = 3 * att_width
    w_t = w_in.T
    conv_width = (w_t.shape[0] - c0 - nheads) // 3

    h, c = _prenorm(x2, g_mix, w_t, c0, b_forget, seq)
    qkv = _qkv(h, w_t, g_q, g_k, att_width)
    conv = _conv(h, w_t, c0 + nheads, conv_width, w_conv, seq)
    att = _attention(qkv, c, batch, seq, nheads)
    x1 = _outproj(att, conv, g_attn_out, g_conv_out, w_out.astype(bf16), x2)

    pos_t, gcol, meta = _route(_router(x1, g_ffn, w_router, b_router))
    n_rows = t * TOP_K + ne * ROW_PAD
    xs = _dispatch(meta, pos_t, x1, g_ffn, n_rows, ne)
    ys = _experts(meta, xs, w_gate_up, b_gate_up, w_down, b_down)
    return _combine(pos_t, gcol, x1, ys)


def kernel(x, g_mix, w_in, b_forget, g_q, g_k, w_conv, g_attn_out, g_conv_out, w_out, g_ffn, w_router,
           b_router, w_gate_up, b_gate_up, w_down, b_down):
    b, s, d = x.shape
    x2 = x.reshape(b * s, d)
    for l in range(g_mix.shape[0]):
        x2 = _layer(x2, b, s, g_mix[l], w_in[l], b_forget[l], g_q[l], g_k[l], w_conv[l], g_attn_out[l],
                    g_conv_out[l], w_out[l], g_ffn[l], w_router[l], b_router[l], w_gate_up[l],
                    b_gate_up[l], w_down[l], b_down[l])
    return x2.reshape(b, s, d)
```

```python
import functools

import jax
import jax.numpy as jnp
import numpy as np
from jax import lax
from jax.experimental import pallas as pl
from jax.experimental.pallas import tpu as pltpu

f32 = jnp.float32
bf16 = jnp.bfloat16
i32 = jnp.int32

HEAD_DIM = 128
LANES = 128
SUBLANES = 8
TOP_K = 4
RMS_EPS = 1e-6
SWIGLU_ALPHA = 1.702
SWIGLU_LIMIT = 7.0
VMEM_LIMIT_BYTES = 56 * 1024 * 1024

ROW_PAD = 128
EXPERT_CAP = 1152
COMPUTE_ROWS = 512
MAX_ITEMS = 128


def _cparams(sem):
    return pltpu.CompilerParams(dimension_semantics=sem, vmem_limit_bytes=VMEM_LIMIT_BYTES)


def _log_sigmoid(z):
    return jnp.minimum(z, 0.0) - jnp.log1p(jnp.exp(-jnp.abs(z)))


def _split3(a):
    hi = a.astype(bf16)
    r1 = a - hi.astype(f32)
    mid = r1.astype(bf16)
    lo = (r1 - mid.astype(f32)).astype(bf16)
    return hi, mid, lo


def _dot_nt(a, b):
    return lax.dot_general(a, b, (((1,), (1,)), ((), ())), preferred_element_type=f32)


def _rms(y, g):
    ms = jnp.mean(y * y, axis=-1, keepdims=True)
    return y * lax.rsqrt(ms + RMS_EPS) * g


def _prenorm_kernel(x_ref, g_ref, wf_ref, bf_ref, h_ref, c_ref, carry, *, tiles_per_seq):
    i = pl.program_id(0)
    hb = _rms(x_ref[...], g_ref[...]).astype(bf16)
    h_ref[...] = hb
    logf = _log_sigmoid(_dot_nt(hb, wf_ref[...].astype(bf16)) + bf_ref[...])
    tm = hb.shape[0]
    a = lax.broadcasted_iota(i32, (tm, tm), 0)
    b = lax.broadcasted_iota(i32, (tm, tm), 1)
    lower = (b <= a).astype(bf16)
    c = sum(jnp.dot(lower, p, preferred_element_type=f32) for p in _split3(logf))

    @pl.when(i % tiles_per_seq == 0)
    def _():
        carry[...] = jnp.zeros_like(carry)

    c = c + carry[...]
    c_ref[...] = c
    carry[...] = c[tm - 1:tm, :]


def _prenorm(x2, g_mix, w_t, f_row0, b_f, seq):
    t, d = x2.shape
    nh = b_f.shape[0]
    tm = 512
    assert f_row0 % LANES == 0 and nh <= LANES
    bfr = jnp.pad(b_f, (0, LANES - nh)).reshape(1, LANES)
    return pl.pallas_call(
        functools.partial(_prenorm_kernel, tiles_per_seq=seq // tm),
        grid=(t // tm,),
        in_specs=[
            pl.BlockSpec((tm, d), lambda i: (i, 0)),
            pl.BlockSpec((1, d), lambda i: (0, 0)),
            pl.BlockSpec((LANES, d), lambda i: (f_row0 // LANES, 0)),
            pl.BlockSpec((1, LANES), lambda i: (0, 0)),
        ],
        out_specs=[
            pl.BlockSpec((tm, d), lambda i: (i, 0)),
            pl.BlockSpec((tm, LANES), lambda i: (i, 0)),
        ],
        out_shape=[
            jax.ShapeDtypeStruct((t, d), bf16),
            jax.ShapeDtypeStruct((t, LANES), f32),
        ],
        scratch_shapes=[pltpu.VMEM((1, LANES), f32)],
        compiler_params=_cparams(("arbitrary",)),
        name="prenorm",
    )(x2, g_mix.reshape(1, d), w_t, bfr)


def _qkv_kernel(h_ref, w_ref, g_ref, post_ref, o_ref, *, n_norm_blocks):
    normed = pl.program_id(0) < n_norm_blocks
    tm, tn = o_ref.shape
    for half in range(2):
        rows = slice(half * (tm // 2), (half + 1) * (tm // 2))
        y = _dot_nt(h_ref[rows, :], w_ref[...].astype(bf16))
        for hh in range(tn // HEAD_DIM):
            sl = slice(hh * HEAD_DIM, (hh + 1) * HEAD_DIM)
            yh = y[:, sl]
            inv = lax.rsqrt(jnp.mean(yh * yh, axis=-1, keepdims=True) + RMS_EPS)
            o_ref[rows, sl] = (yh * jnp.where(normed, inv, 1.0) * g_ref[:, sl] * post_ref[:, sl]).astype(bf16)


def _qkv(h, w_t, g_q, g_k, att_width):
    t, d = h.shape
    n = 3 * att_width
    tm, tn = 1024, 512
    nheads = att_width // HEAD_DIM
    ones = jnp.ones((att_width,), f32)
    gain = jnp.concatenate([jnp.tile(g_q, nheads), jnp.tile(g_k, nheads), ones]).reshape(1, n)
    post = jnp.concatenate([ones * (1.0 / np.sqrt(HEAD_DIM)), ones, ones]).reshape(1, n)
    return pl.pallas_call(
        functools.partial(_qkv_kernel, n_norm_blocks=2 * att_width // tn),
        grid=(n // tn, t // tm),
        in_specs=[
            pl.BlockSpec((tm, d), lambda j, i: (i, 0)),
            pl.BlockSpec((tn, d), lambda j, i: (j, 0)),
            pl.BlockSpec((1, tn), lambda j, i: (0, j)),
            pl.BlockSpec((1, tn), lambda j, i: (0, j)),
        ],
        out_specs=pl.BlockSpec((tm, tn), lambda j, i: (i, j)),
        out_shape=jax.ShapeDtypeStruct((t, n), bf16),
        compiler_params=_cparams(("arbitrary", "arbitrary")),
        name="qkv",
    )(h, w_t, gain, post)


def _conv_kernel(h_ref, wb_ref, wc_ref, wx_ref, wk_ref, o_ref, carry, *, tiles_per_seq):
    i = pl.program_id(1)
    h = h_ref[...]
    gb = _dot_nt(h, wb_ref[...].astype(bf16))
    gc = _dot_nt(h, wc_ref[...].astype(bf16))
    xc = _dot_nt(h, wx_ref[...].astype(bf16))
    u = gc * xc
    tm = u.shape[0]

    @pl.when(i % tiles_per_seq == 0)
    def _():
        carry[...] = jnp.zeros_like(carry)

    p2 = carry[SUBLANES - 2:SUBLANES - 1, :]
    p1 = carry[SUBLANES - 1:SUBLANES, :]
    row = lax.broadcasted_iota(i32, u.shape, 0)
    u1 = jnp.where(row == 0, p1, pltpu.roll(u, 1, 0))
    u2 = jnp.where(row == 0, p2, jnp.where(row == 1, p1, pltpu.roll(u, 2, 0)))
    y = wk_ref[0:1, :] * u2 + wk_ref[1:2, :] * u1 + wk_ref[2:3, :] * u
    o_ref[...] = gb * y
    carry[...] = u[tm - SUBLANES:tm, :]


def _conv(h, w_t, row0, width, w_conv, seq):
    t, d = h.shape
    tm, tn = 1024, 256
    assert row0 % SUBLANES == 0 and width % tn == 0

    def w_spec(part):
        return pl.BlockSpec((pl.Element(tn), pl.Element(d)),
                            lambda j, i: (pl.multiple_of(row0 + part * width + j * tn, SUBLANES), 0))

    return pl.pallas_call(
        functools.partial(_conv_kernel, tiles_per_seq=seq // tm),
        grid=(width // tn, t // tm),
        in_specs=[
            pl.BlockSpec((tm, d), lambda j, i: (i, 0)),
            w_spec(0), w_spec(1), w_spec(2),
            pl.BlockSpec((3, tn), lambda j, i: (0, j)),
        ],
        out_specs=pl.BlockSpec((tm, tn), lambda j, i: (i, j)),
        out_shape=jax.ShapeDtypeStruct((t, width), f32),
        scratch_shapes=[pltpu.VMEM((SUBLANES, tn), f32)],
        compiler_params=_cparams(("arbitrary", "arbitrary")),
        name="conv",
    )(h, w_t, w_t, w_t, w_conv)


def _bias_lanes(c, ones_first):
    n = c.shape[0]
    lane = lax.broadcasted_iota(i32, (n, LANES), 1)
    out = jnp.zeros((n, LANES), f32)
    base_c, base_1 = (3, 0) if ones_first else (0, 3)
    for k, piece in enumerate(_split3(c)):
        out = jnp.where(lane == base_c + k, piece.astype(f32), out)
        out = jnp.where(lane == base_1 + k, 1.0, out)
    return out.astype(bf16)


def _attn_kernel(q_ref, k_ref, v_ref, cq_ref, ck_ref, o_ref, k_aug, *, tq, tk):
    hd = pl.program_id(1)
    qi = pl.program_id(2)

    def head_column(c_ref):
        lane = lax.broadcasted_iota(i32, c_ref.shape, 1)
        return jnp.sum(jnp.where(lane == hd, c_ref[...], 0.0), axis=1, keepdims=True)

    @pl.when(qi == 0)
    def _():
        k_aug[:, :HEAD_DIM] = k_ref[...]
        k_aug[:, HEAD_DIM:] = _bias_lanes(-head_column(ck_ref), ones_first=True)

    q = jnp.concatenate([q_ref[...], _bias_lanes(head_column(cq_ref), ones_first=False)], axis=1)
    def rows_of(j):
        return pl.ds(pl.multiple_of(j * tk, tk), tk)

    def scores(j):
        return _dot_nt(q, k_aug[rows_of(j), :])

    def update(s, vj, state):
        m, l, acc = state
        m_new = jnp.maximum(m, jnp.max(s, axis=1, keepdims=True))
        alpha = jnp.exp(m - m_new)
        p = jnp.exp(s - m_new)
        l = alpha * l + jnp.sum(p, axis=1, keepdims=True)
        acc = alpha * acc + jnp.dot(p.astype(bf16), vj, preferred_element_type=f32)
        return m_new, l, acc

    def body(j, state):
        return update(scores(j), v_ref[rows_of(j), :], state)

    init = (jnp.full((tq, 1), -jnp.inf, f32), jnp.zeros((tq, 1), f32), jnp.zeros((tq, HEAD_DIM), f32))
    n_full = (qi * tq) // tk
    state = lax.fori_loop(0, n_full, body, init)
    r = lax.broadcasted_iota(i32, (tq, tk), 0) + (qi * tq - n_full * tk)
    c = lax.broadcasted_iota(i32, (tq, tk), 1)
    _, l, acc = update(jnp.where(c <= r, scores(n_full), -jnp.inf), v_ref[rows_of(n_full), :], state)
    o_ref[...] = acc / l


def _attention(qkv, c, batch, seq, nheads):
    t = qkv.shape[0]
    tq, tk = 512, 512
    nq = seq // tq
    assert tk % tq == 0 and seq % tk == 0
    return pl.pallas_call(
        functools.partial(_attn_kernel, tq=tq, tk=tk),
        grid=(batch, nheads, nq),
        in_specs=[
            pl.BlockSpec((tq, HEAD_DIM), lambda b, h, qi: (b * nq + qi, h)),
            pl.BlockSpec((seq, HEAD_DIM), lambda b, h, qi: (b, nheads + h)),
            pl.BlockSpec((seq, HEAD_DIM), lambda b, h, qi: (b, 2 * nheads + h)),
            pl.BlockSpec((tq, LANES), lambda b, h, qi: (b * nq + qi, 0)),
            pl.BlockSpec((seq, LANES), lambda b, h, qi: (b, 0)),
        ],
        out_specs=pl.BlockSpec((tq, HEAD_DIM), lambda b, h, qi: (b * nq + qi, h)),
        out_shape=jax.ShapeDtypeStruct((t, nheads * HEAD_DIM), f32),
        scratch_shapes=[pltpu.VMEM((seq, 2 * HEAD_DIM), bf16)],
        compiler_params=_cparams(("arbitrary", "arbitrary", "arbitrary")),
        name="attention",
    )(qkv, qkv, qkv, c, c)


def _outproj_kernel(att_ref, conv_ref, ga_ref, gc_ref, w_ref, x_ref, o_ref, mixed):
    j = pl.program_id(1)
    wa = att_ref.shape[1]

    @pl.when(j == 0)
    def _():
        mixed[:, :wa] = _rms(att_ref[...], ga_ref[...]).astype(bf16)
        mixed[:, wa:] = _rms(conv_ref[...], gc_ref[...]).astype(bf16)

    o_ref[...] = x_ref[...] + jnp.dot(mixed[...], w_ref[...], preferred_element_type=f32)


def _outproj(att, conv, g_a, g_c, w_o, x2):
    t, d = x2.shape
    wa, wc = att.shape[1], conv.shape[1]
    tm, tn = 512, 1024
    return pl.pallas_call(
        _outproj_kernel,
        grid=(t // tm, d // tn),
        in_specs=[
            pl.BlockSpec((tm, wa), lambda i, j: (i, 0)),
            pl.BlockSpec((tm, wc), lambda i, j: (i, 0)),
            pl.BlockSpec((1, wa), lambda i, j: (0, 0)),
            pl.BlockSpec((1, wc), lambda i, j: (0, 0)),
            pl.BlockSpec((wa + wc, tn), lambda i, j: (0, j)),
            pl.BlockSpec((tm, tn), lambda i, j: (i, j)),
        ],
        out_specs=pl.BlockSpec((tm, tn), lambda i, j: (i, j)),
        out_shape=jax.ShapeDtypeStruct((t, d), f32),
        scratch_shapes=[pltpu.VMEM((tm, wa + wc), bf16)],
        compiler_params=_cparams(("arbitrary", "arbitrary")),
        name="outproj",
    )(att, conv, g_a.reshape(1, wa), g_c.reshape(1, wc), w_o, x2)


def _router_kernel(x_ref, g_ref, wrt_ref, br_ref, lg_ref):
    h_hi, h_lo, _ = _split3(_rms(x_ref[...], g_ref[...]))
    w_hi, w_lo, _ = _split3(wrt_ref[...])
    lg_ref[...] = _dot_nt(w_hi, h_hi) + (_dot_nt(w_lo, h_hi) + _dot_nt(w_hi, h_lo)) + br_ref[...]


def _router(x1, g_ffn, w_router, b_router):
    t, d = x1.shape
    ne = w_router.shape[1]
    tm = 512
    return pl.pallas_call(
        _router_kernel,
        grid=(t // tm,),
        in_specs=[
            pl.BlockSpec((tm, d), lambda i: (i, 0)),
            pl.BlockSpec((1, d), lambda i: (0, 0)),
            pl.BlockSpec((ne, d), lambda i: (0, 0)),
            pl.BlockSpec((ne, 1), lambda i: (0, 0)),
        ],
        out_specs=pl.BlockSpec((ne, tm), lambda i: (0, i)),
        out_shape=jax.ShapeDtypeStruct((ne, t), f32),
        compiler_params=_cparams(("arbitrary",)),
        name="router",
    )(x1, g_ffn.reshape(1, d), w_router.T, b_router.reshape(ne, 1))


def _cumsum_sublanes(x):
    n = x.shape[0]
    row = lax.broadcasted_iota(i32, x.shape, 0)
    d = 1
    while d < n:
        x = x + jnp.where(row >= d, pltpu.roll(x, d, 0), 0.0)
        d *= 2
    return x


def _stack_rows(rows, n):
    width = rows[0].shape[1]
    sub = lax.broadcasted_iota(i32, (n, width), 0)
    out = jnp.zeros((n, width), rows[0].dtype)
    for k, r in enumerate(rows):
        out = jnp.where(sub == k, r, out)
    return out


def _route_kernel(lg_ref, pos_ref, gcol_ref, meta_ref, esel, rsel, *, chunk):
    ne, t = lg_ref.shape
    e_iota = lax.broadcasted_iota(i32, (ne, chunk), 0).astype(f32)
    a = lax.broadcasted_iota(i32, (chunk, chunk), 0)
    b = lax.broadcasted_iota(i32, (chunk, chunk), 1)
    before = (a < b).astype(bf16)
    counts = jnp.zeros((ne, 1), f32)
    for c in range(t // chunk):
        sl = slice(c * chunk, (c + 1) * chunk)
        vals = lg_ref[:, sl]
        tops, idxs, hots = [], [], []
        for _ in range(TOP_K):
            m = jnp.max(vals, axis=0, keepdims=True)
            idx = jnp.min(jnp.where(vals == m, e_iota, ne), axis=0, keepdims=True)
            hot = e_iota == idx
            vals = jnp.where(hot, -jnp.inf, vals)
            tops.append(m)
            idxs.append(idx)
            hots.append(hot)
        ex = [jnp.exp(v - tops[0]) for v in tops]
        den = ex[0] + ex[1] + ex[2] + ex[3]
        gates = [e / den for e in ex]
        member = sum(h.astype(f32) for h in hots)
        rank = jnp.dot(member.astype(bf16), before, preferred_element_type=f32) + counts
        counts = counts + jnp.sum(member, axis=1, keepdims=True)
        for k in range(TOP_K):
            rsel[k:k + 1, sl] = jnp.sum(jnp.where(hots[k], rank, 0.0), axis=0, keepdims=True)
            esel[k:k + 1, sl] = idxs[k]
        g8 = _stack_rows(gates, SUBLANES)
        gpad = jnp.concatenate([g8, jnp.zeros((LANES - SUBLANES, chunk), f32)], axis=0)
        gcol_ref[sl, :] = gpad.T

    cnt = jnp.broadcast_to(counts, (ne, LANES))
    pcnt = jnp.ceil(cnt * (1.0 / ROW_PAD)) * ROW_PAD
    pend = _cumsum_sublanes(pcnt)
    pstart = pend - pcnt
    npass = jnp.floor((pcnt + (EXPERT_CAP - ROW_PAD)) / EXPERT_CAP + 0.5 * ROW_PAD / EXPERT_CAP)
    iend = _cumsum_sublanes(npass)
    istart = iend - npass
    n_items = jnp.max(iend, axis=0, keepdims=True)

    pstart_col = pstart[:, 0:1]
    for c in range(t // chunk):
        sl = slice(c * chunk, (c + 1) * chunk)
        for k in range(TOP_K):
            hot = e_iota == esel[k:k + 1, sl]
            ps = jnp.sum(jnp.where(hot, pstart_col, 0.0), axis=0, keepdims=True)
            pos_ref[k:k + 1, sl] = (rsel[k:k + 1, sl] + ps).astype(i32)

    e_sub = lax.broadcasted_iota(i32, (ne, LANES), 0)
    lane = lax.broadcasted_iota(i32, (ne, LANES), 1)
    lane_f = lane.astype(f32)
    item = lane_f[0:1, :]
    e_of = jnp.minimum(jnp.sum((iend <= lane_f).astype(f32), axis=0, keepdims=True), ne - 1.0)
    hot = e_sub.astype(f32) == e_of
    pick = lambda v: jnp.sum(jnp.where(hot, v, 0.0), axis=0, keepdims=True)
    p_i = item - pick(istart)
    row0 = pick(pstart) + p_i * EXPERT_CAP
    nrows = jnp.clip(pick(pcnt) - p_i * EXPERT_CAP, 0.0, float(EXPERT_CAP))
    nsub = nrows * (1.0 / ROW_PAD)
    diag = e_sub == lane
    zrow = jnp.sum(jnp.where(diag, pend - ROW_PAD, 0.0), axis=0, keepdims=True)
    zval = jnp.sum(jnp.where(diag, (cnt > 0).astype(f32), 0.0), axis=0, keepdims=True)
    total = jnp.max(pend, axis=0, keepdims=True)
    rows = [e_of, row0, nsub, n_items, zrow, zval, total]
    meta_ref[...] = _stack_rows([r.astype(i32) for r in rows], SUBLANES)


def _route(logits_t):
    ne, t = logits_t.shape
    chunk = 512
    return pl.pallas_call(
        functools.partial(_route_kernel, chunk=chunk),
        out_shape=[
            jax.ShapeDtypeStruct((TOP_K, t), i32),
            jax.ShapeDtypeStruct((t, LANES), f32),
            jax.ShapeDtypeStruct((SUBLANES, LANES), i32),
        ],
        scratch_shapes=[pltpu.VMEM((TOP_K, t), f32), pltpu.VMEM((TOP_K, t), f32)],
        compiler_params=pltpu.CompilerParams(vmem_limit_bytes=VMEM_LIMIT_BYTES),
        name="route",
    )(logits_t)


_M_EXPERT, _M_ROW0, _M_NSUB, _M_NITEMS, _M_ZROW, _M_ZVALID, _M_TOTAL = range(7)


def _zero_tail(meta, zbuf, dst_ref, sem):
    total = pl.multiple_of(meta[_M_TOTAL, 0], ROW_PAD)
    n = (dst_ref.shape[0] - total) // ROW_PAD

    def copy(r):
        return pltpu.make_async_copy(zbuf, dst_ref.at[pl.ds(total + r * ROW_PAD, ROW_PAD), :], sem)

    def start(r, c):
        copy(r).start()
        return c

    def wait(r, c):
        copy(r).wait()
        return c

    lax.fori_loop(0, n, start, 0)
    lax.fori_loop(0, n, wait, 0)


def _dispatch_kernel(meta, pos_ref, x_ref, g_ref, xs_ref, h_scr, zbuf, sem, zsem, *, tmd, ne, n_tiles):
    i = pl.program_id(0)
    slot = i % 2
    h_scr[slot] = _rms(x_ref[...], g_ref[...])

    def zero_copy(e):
        start = pl.multiple_of(meta[_M_ZROW, e], ROW_PAD)
        return pltpu.make_async_copy(zbuf, xs_ref.at[pl.ds(start, ROW_PAD), :], zsem)

    @pl.when(i == 0)
    def _():
        zbuf[...] = jnp.zeros_like(zbuf)

        def zstart(e, c):
            @pl.when(meta[_M_ZVALID, e] > 0)
            def _():
                zero_copy(e).start()
            return c

        def zwait(e, c):
            @pl.when(meta[_M_ZVALID, e] > 0)
            def _():
                zero_copy(e).wait()
            return c

        lax.fori_loop(0, ne, zstart, 0)
        lax.fori_loop(0, ne, zwait, 0)
        _zero_tail(meta, zbuf, xs_ref, zsem)

    def row_copy(sl, r, k):
        return pltpu.make_async_copy(h_scr.at[sl, pl.ds(r, 1), :],
                                     xs_ref.at[pl.ds(pos_ref[k, r], 1), :], sem.at[sl])

    def issue(r, c):
        for k in range(TOP_K):
            row_copy(slot, r, k).start()
        return c

    def drain(sl):
        def body(r, c):
            for k in range(TOP_K):
                row_copy(sl, r, k).wait()
            return c
        lax.fori_loop(0, tmd, body, 0)

    lax.fori_loop(0, tmd, issue, 0)

    @pl.when(i > 0)
    def _():
        drain(1 - slot)

    @pl.when(i == n_tiles - 1)
    def _():
        drain(slot)


def _dispatch(meta, pos_t, x1, g_ffn, n_rows, ne):
    t, d = x1.shape
    tmd = 256
    return pl.pallas_call(
        functools.partial(_dispatch_kernel, tmd=tmd, ne=ne, n_tiles=t // tmd),
        grid_spec=pltpu.PrefetchScalarGridSpec(
            num_scalar_prefetch=1,
            grid=(t // tmd,),
            in_specs=[
                pl.BlockSpec((TOP_K, tmd), lambda i, m: (0, i), memory_space=pltpu.SMEM),
                pl.BlockSpec((tmd, d), lambda i, m: (i, 0)),
                pl.BlockSpec((1, d), lambda i, m: (0, 0)),
            ],
            out_specs=pl.BlockSpec(memory_space=pl.ANY),
            scratch_shapes=[pltpu.VMEM((2, tmd, d), f32), pltpu.VMEM((ROW_PAD, d), f32),
                            pltpu.SemaphoreType.DMA((2,)), pltpu.SemaphoreType.DMA],
        ),
        out_shape=jax.ShapeDtypeStruct((n_rows, d), f32),
        compiler_params=_cparams(("arbitrary",)),
        name="dispatch",
    )(meta, pos_t, x1, g_ffn.reshape(1, d))


def _expert_kernel(meta, xs_ref, wgu_hbm, wd_hbm, bgu_ref, bd_ref, ys_ref,
                   x_scr, h_scr, wbuf, stage, y_scr, w_sem, ld_sem, st_sem,
                   *, nf, nd, tf, td):
    it = pl.program_id(0)
    n_items = meta[_M_NITEMS, 0]
    f_dim = wd_hbm.shape[1]

    def up_copies(e, f, slot):
        cols = pl.ds(pl.multiple_of(f * tf, tf), tf)
        cols_up = pl.ds(pl.multiple_of(f_dim + f * tf, tf), tf)
        return (pltpu.make_async_copy(wgu_hbm.at[e, :, cols], wbuf.at[slot, :, pl.ds(0, tf)], w_sem.at[slot]),
                pltpu.make_async_copy(wgu_hbm.at[e, :, cols_up], wbuf.at[slot, :, pl.ds(tf, tf)], w_sem.at[slot]))

    def down_copy(e, dcol, slot):
        cols = pl.ds(pl.multiple_of(dcol * td, td), td)
        return pltpu.make_async_copy(wd_hbm.at[e, :, cols], wbuf.at[slot, pl.ds(0, f_dim), :], w_sem.at[slot])

    def load_copy(first_row, r, slot):
        return pltpu.make_async_copy(xs_ref.at[pl.ds(first_row + r * ROW_PAD, ROW_PAD), :],
                                     stage.at[slot], ld_sem.at[slot])

    def rows_of(r):
        return pl.ds(pl.multiple_of(r * ROW_PAD, ROW_PAD), ROW_PAD)

    @pl.when(it < n_items)
    def _():
        e = meta[_M_EXPERT, it]
        row0 = pl.multiple_of(meta[_M_ROW0, it], ROW_PAD)
        nsub = meta[_M_NSUB, it]
        has_next = it + 1 < n_items
        nxt = jnp.minimum(it + 1, n_items - 1)
        next_row0 = pl.multiple_of(meta[_M_ROW0, nxt], ROW_PAD)
        next_nsub = meta[_M_NSUB, nxt]
        xcur = it % 2

        @pl.when(it == 0)
        def _():
            for c in up_copies(e, 0, 0):
                c.start()
            stage[0] = jnp.zeros(stage.shape[1:], f32)
            _zero_tail(meta, stage.at[0], ys_ref, ld_sem.at[0])
            load_copy(row0, 0, 0).start()

            def body(r, c):
                slot = r % 2
                load_copy(row0, r, slot).wait()

                @pl.when(r + 1 < nsub)
                def _():
                    load_copy(row0, r + 1, 1 - slot).start()

                x_scr[0, rows_of(r), :] = stage[slot].astype(bf16)
                return c

            lax.fori_loop(0, nsub, body, 0)

        def for_row_tiles(fn):
            per = COMPUTE_ROWS // ROW_PAD
            n_full = nsub // per

            def pair(r, c):
                first = pl.multiple_of(r * (2 * COMPUTE_ROWS), COMPUTE_ROWS)
                fn(first, COMPUTE_ROWS)
                fn(first + COMPUTE_ROWS, COMPUTE_ROWS)
                return c

            lax.fori_loop(0, n_full // 2, pair, 0)

            @pl.when(n_full % 2 == 1)
            def _():
                fn(pl.multiple_of((n_full - 1) * COMPUTE_ROWS, COMPUTE_ROWS), COMPUTE_ROWS)

            for tail in range(1, per):
                @pl.when(nsub % per == tail)
                def _():
                    fn(pl.multiple_of(n_full * COMPUTE_ROWS, COMPUTE_ROWS), tail * ROW_PAD)

        def up_chunk(f, carry):
            slot = f % 2
            for c in up_copies(e, f, slot):
                c.wait()

            @pl.when(f + 1 < nf)
            def _():
                for c in up_copies(e, f + 1, 1 - slot):
                    c.start()

            @pl.when(f + 1 == nf)
            def _():
                down_copy(e, 0, 1 - slot).start()

            bg = bgu_ref[f]
            bu = bgu_ref[nf + f]

            for j in range(stage.shape[0]):
                @pl.when(has_next & (f + j * nf < next_nsub))
                def _():
                    load_copy(next_row0, f + j * nf, j).start()

            def up_tile(first, n):
                rows = pl.ds(first, n)
                gu = jnp.dot(x_scr[xcur, rows, :], wbuf[slot].astype(bf16), preferred_element_type=f32)
                g = jnp.minimum(gu[:, :tf] + bg, SWIGLU_LIMIT)
                u = jnp.clip(gu[:, tf:] + bu, -SWIGLU_LIMIT, SWIGLU_LIMIT)
                act = (u + 1.0) * (g * jax.nn.sigmoid(SWIGLU_ALPHA * g))
                h_scr[f, rows, :] = act.astype(bf16)

            for_row_tiles(up_tile)

            for j in range(stage.shape[0]):
                @pl.when(has_next & (f + j * nf < next_nsub))
                def _():
                    load_copy(next_row0, f + j * nf, j).wait()
                    x_scr[1 - xcur, rows_of(f + j * nf), :] = stage[j].astype(bf16)

            return carry

        lax.fori_loop(0, nf, up_chunk, 0)

        def store_copy(first, dcol, sl):
            dst = ys_ref.at[pl.ds(row0 + first, ROW_PAD), pl.ds(pl.multiple_of(dcol * td, td), td)]
            return pltpu.make_async_copy(y_scr.at[sl, pl.ds(first, ROW_PAD), :], dst, st_sem.at[sl])

        def wait_stores(sl, count):
            def w(r, c):
                store_copy(0, 0, sl).wait()
                return c
            lax.fori_loop(0, count, w, 0)

        def down_chunk(dcol, carry):
            slot = (nf + dcol) % 2
            yslot = dcol % 2
            down_copy(e, dcol, slot).wait()

            @pl.when(dcol + 1 < nd)
            def _():
                down_copy(e, dcol + 1, 1 - slot).start()

            @pl.when((dcol + 1 == nd) & has_next)
            def _():
                for c in up_copies(meta[_M_EXPERT, nxt], 0, 0):
                    c.start()

            bd = bd_ref[dcol]

            @pl.when(dcol >= 2)
            def _():
                wait_stores(yslot, nsub)

            @pl.when((dcol < 2) & (it > 0))
            def _():
                wait_stores(yslot, meta[_M_NSUB, jnp.maximum(it - 1, 0)])

            def down_tile(first, n):
                rows = pl.ds(first, n)
                hidden = jnp.concatenate([h_scr[cf, rows, :] for cf in range(nf)], axis=1)
                y_scr[yslot, rows, :] = bd + jnp.dot(hidden, wbuf[slot, 0:f_dim, :].astype(bf16),
                                                     preferred_element_type=f32)
                for part in range(n // ROW_PAD):
                    store_copy(first + part * ROW_PAD, dcol, yslot).start()

            for_row_tiles(down_tile)

            return carry

        lax.fori_loop(0, nd, down_chunk, 0)

        @pl.when(jnp.logical_not(has_next))
        def _():
            wait_stores(0, nsub)
            wait_stores(1, nsub)


def _experts(meta, xs, w_gate_up, b_gate_up, w_down, b_down):
    n_rows, d = xs.shape
    ne, _, f2 = w_gate_up.shape
    f = f2 // 2
    tf, td = 256, 512
    nf, nd = f // tf, d // td
    max_items = (n_rows + ne * (EXPERT_CAP - ROW_PAD)) // EXPERT_CAP
    stage_slots = max(2, pl.cdiv(EXPERT_CAP // ROW_PAD, nf))
    assert max_items <= MAX_ITEMS and td == 2 * tf and f <= d
    assert nf % 2 == 0 and nd % 2 == 0

    def expert(i, m):
        return m[_M_EXPERT, jnp.minimum(i, m[_M_NITEMS, 0] - 1)]

    return pl.pallas_call(
        functools.partial(_expert_kernel, nf=nf, nd=nd, tf=tf, td=td),
        grid_spec=pltpu.PrefetchScalarGridSpec(
            num_scalar_prefetch=1,
            grid=(max_items,),
            in_specs=[
                pl.BlockSpec(memory_space=pl.ANY),
                pl.BlockSpec(memory_space=pl.ANY),
                pl.BlockSpec(memory_space=pl.ANY),
                pl.BlockSpec((None, 2 * nf, 1, tf), lambda i, m: (expert(i, m), 0, 0, 0)),
                pl.BlockSpec((None, nd, 1, td), lambda i, m: (expert(i, m), 0, 0, 0)),
            ],
            out_specs=pl.BlockSpec(memory_space=pl.ANY),
            scratch_shapes=[
                pltpu.VMEM((2, EXPERT_CAP, d), bf16),
                pltpu.VMEM((nf, EXPERT_CAP, tf), bf16),
                pltpu.VMEM((2, d, 2 * tf), f32),
                pltpu.VMEM((stage_slots, ROW_PAD, d), f32),
                pltpu.VMEM((2, EXPERT_CAP, td), f32),
                pltpu.SemaphoreType.DMA((2,)),
                pltpu.SemaphoreType.DMA((stage_slots,)),
                pltpu.SemaphoreType.DMA((2,)),
            ],
        ),
        out_shape=jax.ShapeDtypeStruct((n_rows, d), f32),
        compiler_params=_cparams(("arbitrary",)),
        name="experts",
    )(meta, xs, w_gate_up, w_down, b_gate_up.reshape(ne, 2 * nf, 1, tf), b_down.reshape(ne, nd, 1, td))


def _combine_kernel(pos_cur, pos_nxt, gcol_ref, x_ref, ys_ref, o_ref, gbuf, sem, *, tmc, n_tiles):
    i = pl.program_id(0)
    slot = i % 2

    def row_copy(pos_ref, sl, r, k):
        return pltpu.make_async_copy(ys_ref.at[pl.ds(pos_ref[k, r], 1), :],
                                     gbuf.at[sl, k, pl.ds(r, 1), :], sem.at[sl])

    def issue(pos_ref, sl):
        def body(r, c):
            for k in range(TOP_K):
                row_copy(pos_ref, sl, r, k).start()
            return c
        lax.fori_loop(0, tmc, body, 0)

    @pl.when(i == 0)
    def _():
        issue(pos_cur, 0)

    @pl.when(i + 1 < n_tiles)
    def _():
        issue(pos_nxt, 1 - slot)

    def drain(r, c):
        for k in range(TOP_K):
            row_copy(pos_cur, slot, r, k).wait()
        return c

    lax.fori_loop(0, tmc, drain, 0)
    acc = x_ref[...]
    for k in range(TOP_K):
        acc = acc + gcol_ref[:, k:k + 1] * gbuf[slot, k]
    o_ref[...] = acc


def _combine(pos_t, gcol, x1, ys):
    t, d = x1.shape
    tmc = 128
    n_tiles = t // tmc
    return pl.pallas_call(
        functools.partial(_combine_kernel, tmc=tmc, n_tiles=n_tiles),
        grid=(n_tiles,),
        in_specs=[
            pl.BlockSpec((TOP_K, tmc), lambda i: (0, i), memory_space=pltpu.SMEM),
            pl.BlockSpec((TOP_K, tmc), lambda i: (0, jnp.minimum(i + 1, n_tiles - 1)), memory_space=pltpu.SMEM),
            pl.BlockSpec((tmc, LANES), lambda i: (i, 0)),
            pl.BlockSpec((tmc, d), lambda i: (i, 0)),
            pl.BlockSpec(memory_space=pl.ANY),
        ],
        out_specs=pl.BlockSpec((tmc, d), lambda i: (i, 0)),
        out_shape=jax.ShapeDtypeStruct((t, d), f32),
        scratch_shapes=[pltpu.VMEM((2, TOP_K, tmc, d), f32), pltpu.SemaphoreType.DMA((2,))],
        compiler_params=_cparams(("arbitrary",)),
        name="combine",
    )(pos_t, pos_t, gcol, x1, ys)


def _layer(x2, batch, seq, g_mix, w_in, b_forget, g_q, g_k, w_conv, g_attn_out, g_conv_out, w_out,
           g_ffn, w_router, b_router, w_gate_up, b_gate_up, w_down, b_down):
    t, d = x2.shape
    nheads = b_forget.shape[0]
    att_width = nheads * HEAD_DIM
    ne = w_router.shape[1]
    c0 = 3 * att_width
    w_t = w_in.T
    conv_width = (w_t.shape[0] - c0 - nheads) // 3

    h, c = _prenorm(x2, g_mix, w_t, c0, b_forget, seq)
    qkv = _qkv(h, w_t, g_q, g_k, att_width)
    conv = _conv(h, w_t, c0 + nheads, conv_width, w_conv, seq)
    att = _attention(qkv, c, batch, seq, nheads)
    x1 = _outproj(att, conv, g_attn_out, g_conv_out, w_out.astype(bf16), x2)

    pos_t, gcol, meta = _route(_router(x1, g_ffn, w_router, b_router))
    n_rows = t * TOP_K + ne * ROW_PAD
    xs = _dispatch(meta, pos_t, x1, g_ffn, n_rows, ne)
    ys = _experts(meta, xs, w_gate_up, b_gate_up, w_down, b_down)
    return _combine(pos_t, gcol, x1, ys)


def kernel(x, g_mix, w_in, b_forget, g_q, g_k, w_conv, g_attn_out, g_conv_out, w_out, g_ffn, w_router,
           b_router, w_gate_up, b_gate_up, w_down, b_down):
    b, s, d = x.shape
    x2 = x.reshape(b * s, d)
    for l in range(g_mix.shape[0]):
        x2 = _layer(x2, b, s, g_mix[l], w_in[l], b_forget[l], g_q[l], g_k[l], w_conv[l], g_attn_out[l],
                    g_conv_out[l], w_out[l], g_ffn[l], w_router[l], b_router[l], w_gate_up[l],
                    b_gate_up[l], w_down[l], b_down[l])
    return x2.reshape(b, s, d)
```

```python
import functools

import jax
import jax.numpy as jnp
import numpy as np
from jax import lax
from jax.experimental import pallas as pl
from jax.experimental.pallas import tpu as pltpu

f32 = jnp.float32
bf16 = jnp.bfloat16
i32 = jnp.int32

HEAD_DIM = 128
LANES = 128
SUBLANES = 8
TOP_K = 4
RMS_EPS = 1e-6
SWIGLU_ALPHA = 1.702
SWIGLU_LIMIT = 7.0
VMEM_LIMIT_BYTES = 56 * 1024 * 1024

ROW_PAD = 128
EXPERT_CAP = 1152
COMPUTE_ROWS = 512
MAX_ITEMS = 128


def _cparams(sem):
    return pltpu.CompilerParams(dimension_semantics=sem, vmem_limit_bytes=VMEM_LIMIT_BYTES)


def _log_sigmoid(z):
    return jnp.minimum(z, 0.0) - jnp.log1p(jnp.exp(-jnp.abs(z)))


def _split3(a):
    hi = a.astype(bf16)
    r1 = a - hi.astype(f32)
    mid = r1.astype(bf16)
    lo = (r1 - mid.astype(f32)).astype(bf16)
    return hi, mid, lo


def _dot_nt(a, b):
    return lax.dot_general(a, b, (((1,), (1,)), ((), ())), preferred_element_type=f32)


def _rms(y, g):
    ms = jnp.mean(y * y, axis=-1, keepdims=True)
    return y * lax.rsqrt(ms + RMS_EPS) * g


def _prenorm_kernel(x_ref, g_ref, wf_ref, bf_ref, h_ref, c_ref, carry, *, tiles_per_seq):
    i = pl.program_id(0)
    hb = _rms(x_ref[...], g_ref[...]).astype(bf16)
    h_ref[...] = hb
    logf = _log_sigmoid(_dot_nt(hb, wf_ref[...].astype(bf16)) + bf_ref[...])
    tm = hb.shape[0]
    a = lax.broadcasted_iota(i32, (tm, tm), 0)
    b = lax.broadcasted_iota(i32, (tm, tm), 1)
    lower = (b <= a).astype(bf16)
    c = sum(jnp.dot(lower, p, preferred_element_type=f32) for p in _split3(logf))

    @pl.when(i % tiles_per_seq == 0)
    def _():
        carry[...] = jnp.zeros_like(carry)

    c = c + carry[...]
    c_ref[...] = c
    carry[...] = c[tm - 1:tm, :]


def _prenorm(x2, g_mix, w_t, f_row0, b_f, seq):
    t, d = x2.shape
    nh = b_f.shape[0]
    tm = 512
    assert f_row0 % LANES == 0 and nh <= LANES
    bfr = jnp.pad(b_f, (0, LANES - nh)).reshape(1, LANES)
    return pl.pallas_call(
        functools.partial(_prenorm_kernel, tiles_per_seq=seq // tm),
        grid=(t // tm,),
        in_specs=[
            pl.BlockSpec((tm, d), lambda i: (i, 0)),
            pl.BlockSpec((1, d), lambda i: (0, 0)),
            pl.BlockSpec((LANES, d), lambda i: (f_row0 // LANES, 0)),
            pl.BlockSpec((1, LANES), lambda i: (0, 0)),
        ],
        out_specs=[
            pl.BlockSpec((tm, d), lambda i: (i, 0)),
            pl.BlockSpec((tm, LANES), lambda i: (i, 0)),
        ],
        out_shape=[
            jax.ShapeDtypeStruct((t, d), bf16),
            jax.ShapeDtypeStruct((t, LANES), f32),
        ],
        scratch_shapes=[pltpu.VMEM((1, LANES), f32)],
        compiler_params=_cparams(("arbitrary",)),
        name="prenorm",
    )(x2, g_mix.reshape(1, d), w_t, bfr)


def _qkv_kernel(h_ref, w_ref, g_ref, post_ref, o_ref, *, n_norm_blocks):
    normed = pl.program_id(0) < n_norm_blocks
    tm, tn = o_ref.shape
    for half in range(2):
        rows = slice(half * (tm // 2), (half + 1) * (tm // 2))
        y = _dot_nt(h_ref[rows, :], w_ref[...].astype(bf16))
        for hh in range(tn // HEAD_DIM):
            sl = slice(hh * HEAD_DIM, (hh + 1) * HEAD_DIM)
            yh = y[:, sl]
            inv = lax.rsqrt(jnp.mean(yh * yh, axis=-1, keepdims=True) + RMS_EPS)
            o_ref[rows, sl] = (yh * jnp.where(normed, inv, 1.0) * g_ref[:, sl] * post_ref[:, sl]).astype(bf16)


def _qkv(h, w_t, g_q, g_k, att_width):
    t, d = h.shape
    n = 3 * att_width
    tm, tn = 1024, 512
    nheads = att_width // HEAD_DIM
    ones = jnp.ones((att_width,), f32)
    gain = jnp.concatenate([jnp.tile(g_q, nheads), jnp.tile(g_k, nheads), ones]).reshape(1, n)
    post = jnp.concatenate([ones * (1.0 / np.sqrt(HEAD_DIM)), ones, ones]).reshape(1, n)
    return pl.pallas_call(
        functools.partial(_qkv_kernel, n_norm_blocks=2 * att_width // tn),
        grid=(n // tn, t // tm),
        in_specs=[
            pl.BlockSpec((tm, d), lambda j, i: (i, 0)),
            pl.BlockSpec((tn, d), lambda j, i: (j, 0)),
            pl.BlockSpec((1, tn), lambda j, i: (0, j)),
            pl.BlockSpec((1, tn), lambda j, i: (0, j)),
        ],
        out_specs=pl.BlockSpec((tm, tn), lambda j, i: (i, j)),
        out_shape=jax.ShapeDtypeStruct((t, n), bf16),
        compiler_params=_cparams(("arbitrary", "arbitrary")),
        name="qkv",
    )(h, w_t, gain, post)


def _conv_kernel(h_ref, wb_ref, wc_ref, wx_ref, wk_ref, o_ref, carry, *, tiles_per_seq):
    i = pl.program_id(1)
    h = h_ref[...]
    gb = _dot_nt(h, wb_ref[...].astype(bf16))
    gc = _dot_nt(h, wc_ref[...].astype(bf16))
    xc = _dot_nt(h, wx_ref[...].astype(bf16))
    u = gc * xc
    tm = u.shape[0]

    @pl.when(i % tiles_per_seq == 0)
    def _():
        carry[...] = jnp.zeros_like(carry)

    p2 = carry[SUBLANES - 2:SUBLANES - 1, :]
    p1 = carry[SUBLANES - 1:SUBLANES, :]
    row = lax.broadcasted_iota(i32, u.shape, 0)
    u1 = jnp.where(row == 0, p1, pltpu.roll(u, 1, 0))
    u2 = jnp.where(row == 0, p2, jnp.where(row == 1, p1, pltpu.roll(u, 2, 0)))
    y = wk_ref[0:1, :] * u2 + wk_ref[1:2, :] * u1 + wk_ref[2:3, :] * u
    o_ref[...] = gb * y
    carry[...] = u[tm - SUBLANES:tm, :]


def _conv(h, w_t, row0, width, w_conv, seq):
    t, d = h.shape
    tm, tn = 1024, 256
    assert row0 % SUBLANES == 0 and width % tn == 0

    def w_spec(part):
        return pl.BlockSpec((pl.Element(tn), pl.Element(d)),
                            lambda j, i: (pl.multiple_of(row0 + part * width + j * tn, SUBLANES), 0))

    return pl.pallas_call(
        functools.partial(_conv_kernel, tiles_per_seq=seq // tm),
        grid=(width // tn, t // tm),
        in_specs=[
            pl.BlockSpec((tm, d), lambda j, i: (i, 0)),
            w_spec(0), w_spec(1), w_spec(2),
            pl.BlockSpec((3, tn), lambda j, i: (0, j)),
        ],
        out_specs=pl.BlockSpec((tm, tn), lambda j, i: (i, j)),
        out_shape=jax.ShapeDtypeStruct((t, width), f32),
        scratch_shapes=[pltpu.VMEM((SUBLANES, tn), f32)],
        compiler_params=_cparams(("arbitrary", "arbitrary")),
        name="conv",
    )(h, w_t, w_t, w_t, w_conv)


def _bias_lanes(c, ones_first):
    n = c.shape[0]
    lane = lax.broadcasted_iota(i32, (n, LANES), 1)
    out = jnp.zeros((n, LANES), f32)
    base_c, base_1 = (3, 0) if ones_first else (0, 3)
    for k, piece in enumerate(_split3(c)):
        out = jnp.where(lane == base_c + k, piece.astype(f32), out)
        out = jnp.where(lane == base_1 + k, 1.0, out)
    return out.astype(bf16)


def _attn_kernel(q_ref, k_ref, v_ref, cq_ref, ck_ref, o_ref, k_aug, *, tq):
    hd = pl.program_id(1)
    qi = pl.program_id(2)

    def head_column(c_ref):
        lane = lax.broadcasted_iota(i32, c_ref.shape, 1)
        return jnp.sum(jnp.where(lane == hd, c_ref[...], 0.0), axis=1, keepdims=True)

    @pl.when(qi == 0)
    def _():
        k_aug[:, :HEAD_DIM] = k_ref[...]
        k_aug[:, HEAD_DIM:] = _bias_lanes(-head_column(ck_ref), ones_first=True)

    q = jnp.concatenate([q_ref[...], _bias_lanes(head_column(cq_ref), ones_first=False)], axis=1)
    def rows_of(j):
        return pl.ds(pl.multiple_of(j * tq, tq), tq)

    def scores(j):
        return _dot_nt(q, k_aug[rows_of(j), :])

    def update(s, vj, state):
        m, l, acc = state
        m_new = jnp.maximum(m, jnp.max(s, axis=1, keepdims=True))
        alpha = jnp.exp(m - m_new)
        p = jnp.exp(s - m_new)
        l = alpha * l + jnp.sum(p, axis=1, keepdims=True)
        acc = alpha * acc + jnp.dot(p.astype(bf16), vj, preferred_element_type=f32)
        return m_new, l, acc

    def body(j, state):
        return update(scores(j), v_ref[rows_of(j), :], state)

    init = (jnp.full((tq, 1), -jnp.inf, f32), jnp.zeros((tq, 1), f32), jnp.zeros((tq, HEAD_DIM), f32))
    state = lax.fori_loop(0, qi, body, init)
    r = lax.broadcasted_iota(i32, (tq, tq), 0)
    c = lax.broadcasted_iota(i32, (tq, tq), 1)
    _, l, acc = update(jnp.where(c <= r, scores(qi), -jnp.inf), v_ref[rows_of(qi), :], state)
    o_ref[...] = acc / l


def _attention(qkv, c, batch, seq, nheads):
    t = qkv.shape[0]
    tq = 512
    nq = seq // tq
    return pl.pallas_call(
        functools.partial(_attn_kernel, tq=tq),
        grid=(batch, nheads, nq),
        in_specs=[
            pl.BlockSpec((tq, HEAD_DIM), lambda b, h, qi: (b * nq + qi, h)),
            pl.BlockSpec((seq, HEAD_DIM), lambda b, h, qi: (b, nheads + h)),
            pl.BlockSpec((seq, HEAD_DIM), lambda b, h, qi: (b, 2 * nheads + h)),
            pl.BlockSpec((tq, LANES), lambda b, h, qi: (b * nq + qi, 0)),
            pl.BlockSpec((seq, LANES), lambda b, h, qi: (b, 0)),
        ],
        out_specs=pl.BlockSpec((tq, HEAD_DIM), lambda b, h, qi: (b * nq + qi, h)),
        out_shape=jax.ShapeDtypeStruct((t, nheads * HEAD_DIM), f32),
        scratch_shapes=[pltpu.VMEM((seq, 2 * HEAD_DIM), bf16)],
        compiler_params=_cparams(("arbitrary", "arbitrary", "arbitrary")),
        name="attention",
    )(qkv, qkv, qkv, c, c)


def _outproj_kernel(att_ref, conv_ref, ga_ref, gc_ref, w_ref, x_ref, o_ref, mixed):
    j = pl.program_id(1)
    wa = att_ref.shape[1]

    @pl.when(j == 0)
    def _():
        mixed[:, :wa] = _rms(att_ref[...], ga_ref[...]).astype(bf16)
        mixed[:, wa:] = _rms(conv_ref[...], gc_ref[...]).astype(bf16)

    o_ref[...] = x_ref[...] + jnp.dot(mixed[...], w_ref[...], preferred_element_type=f32)


def _outproj(att, conv, g_a, g_c, w_o, x2):
    t, d = x2.shape
    wa, wc = att.shape[1], conv.shape[1]
    tm, tn = 512, 1024
    return pl.pallas_call(
        _outproj_kernel,
        grid=(t // tm, d // tn),
        in_specs=[
            pl.BlockSpec((tm, wa), lambda i, j: (i, 0)),
            pl.BlockSpec((tm, wc), lambda i, j: (i, 0)),
            pl.BlockSpec((1, wa), lambda i, j: (0, 0)),
            pl.BlockSpec((1, wc), lambda i, j: (0, 0)),
            pl.BlockSpec((wa + wc, tn), lambda i, j: (0, j)),
            pl.BlockSpec((tm, tn), lambda i, j: (i, j)),
        ],
        out_specs=pl.BlockSpec((tm, tn), lambda i, j: (i, j)),
        out_shape=jax.ShapeDtypeStruct((t, d), f32),
        scratch_shapes=[pltpu.VMEM((tm, wa + wc), bf16)],
        compiler_params=_cparams(("arbitrary", "arbitrary")),
        name="outproj",
    )(att, conv, g_a.reshape(1, wa), g_c.reshape(1, wc), w_o, x2)


def _router_kernel(x_ref, g_ref, wrt_ref, br_ref, lg_ref):
    h_hi, h_lo, _ = _split3(_rms(x_ref[...], g_ref[...]))
    w_hi, w_lo, _ = _split3(wrt_ref[...])
    lg_ref[...] = _dot_nt(w_hi, h_hi) + (_dot_nt(w_lo, h_hi) + _dot_nt(w_hi, h_lo)) + br_ref[...]


def _router(x1, g_ffn, w_router, b_router):
    t, d = x1.shape
    ne = w_router.shape[1]
    tm = 512
    return pl.pallas_call(
        _router_kernel,
        grid=(t // tm,),
        in_specs=[
            pl.BlockSpec((tm, d), lambda i: (i, 0)),
            pl.BlockSpec((1, d), lambda i: (0, 0)),
            pl.BlockSpec((ne, d), lambda i: (0, 0)),
            pl.BlockSpec((ne, 1), lambda i: (0, 0)),
        ],
        out_specs=pl.BlockSpec((ne, tm), lambda i: (0, i)),
        out_shape=jax.ShapeDtypeStruct((ne, t), f32),
        compiler_params=_cparams(("arbitrary",)),
        name="router",
    )(x1, g_ffn.reshape(1, d), w_router.T, b_router.reshape(ne, 1))


def _cumsum_sublanes(x):
    n = x.shape[0]
    row = lax.broadcasted_iota(i32, x.shape, 0)
    d = 1
    while d < n:
        x = x + jnp.where(row >= d, pltpu.roll(x, d, 0), 0.0)
        d *= 2
    return x


def _stack_rows(rows, n):
    width = rows[0].shape[1]
    sub = lax.broadcasted_iota(i32, (n, width), 0)
    out = jnp.zeros((n, width), rows[0].dtype)
    for k, r in enumerate(rows):
        out = jnp.where(sub == k, r, out)
    return out


def _route_kernel(lg_ref, pos_ref, gcol_ref, meta_ref, esel, rsel, *, chunk):
    ne, t = lg_ref.shape
    e_iota = lax.broadcasted_iota(i32, (ne, chunk), 0).astype(f32)
    a = lax.broadcasted_iota(i32, (chunk, chunk), 0)
    b = lax.broadcasted_iota(i32, (chunk, chunk), 1)
    before = (a < b).astype(bf16)
    counts = jnp.zeros((ne, 1), f32)
    for c in range(t // chunk):
        sl = slice(c * chunk, (c + 1) * chunk)
        vals = lg_ref[:, sl]
        tops, idxs, hots = [], [], []
        for _ in range(TOP_K):
            m = jnp.max(vals, axis=0, keepdims=True)
            idx = jnp.min(jnp.where(vals == m, e_iota, ne), axis=0, keepdims=True)
            hot = e_iota == idx
            vals = jnp.where(hot, -jnp.inf, vals)
            tops.append(m)
            idxs.append(idx)
            hots.append(hot)
        ex = [jnp.exp(v - tops[0]) for v in tops]
        den = ex[0] + ex[1] + ex[2] + ex[3]
        gates = [e / den for e in ex]
        member = sum(h.astype(f32) for h in hots)
        rank = jnp.dot(member.astype(bf16), before, preferred_element_type=f32) + counts
        counts = counts + jnp.sum(member, axis=1, keepdims=True)
        for k in range(TOP_K):
            rsel[k:k + 1, sl] = jnp.sum(jnp.where(hots[k], rank, 0.0), axis=0, keepdims=True)
            esel[k:k + 1, sl] = idxs[k]
        g8 = _stack_rows(gates, SUBLANES)
        gpad = jnp.concatenate([g8, jnp.zeros((LANES - SUBLANES, chunk), f32)], axis=0)
        gcol_ref[sl, :] = gpad.T

    cnt = jnp.broadcast_to(counts, (ne, LANES))
    pcnt = jnp.ceil(cnt * (1.0 / ROW_PAD)) * ROW_PAD
    pend = _cumsum_sublanes(pcnt)
    pstart = pend - pcnt
    npass = jnp.floor((pcnt + (EXPERT_CAP - ROW_PAD)) / EXPERT_CAP + 0.5 * ROW_PAD / EXPERT_CAP)
    iend = _cumsum_sublanes(npass)
    istart = iend - npass
    n_items = jnp.max(iend, axis=0, keepdims=True)

    pstart_col = pstart[:, 0:1]
    for c in range(t // chunk):
        sl = slice(c * chunk, (c + 1) * chunk)
        for k in range(TOP_K):
            hot = e_iota == esel[k:k + 1, sl]
            ps = jnp.sum(jnp.where(hot, pstart_col, 0.0), axis=0, keepdims=True)
            pos_ref[k:k + 1, sl] = (rsel[k:k + 1, sl] + ps).astype(i32)

    e_sub = lax.broadcasted_iota(i32, (ne, LANES), 0)
    lane = lax.broadcasted_iota(i32, (ne, LANES), 1)
    lane_f = lane.astype(f32)
    item = lane_f[0:1, :]
    e_of = jnp.minimum(jnp.sum((iend <= lane_f).astype(f32), axis=0, keepdims=True), ne - 1.0)
    hot = e_sub.astype(f32) == e_of
    pick = lambda v: jnp.sum(jnp.where(hot, v, 0.0), axis=0, keepdims=True)
    p_i = item - pick(istart)
    row0 = pick(pstart) + p_i * EXPERT_CAP
    nrows = jnp.clip(pick(pcnt) - p_i * EXPERT_CAP, 0.0, float(EXPERT_CAP))
    nsub = nrows * (1.0 / ROW_PAD)
    diag = e_sub == lane
    zrow = jnp.sum(jnp.where(diag, pend - ROW_PAD, 0.0), axis=0, keepdims=True)
    zval = jnp.sum(jnp.where(diag, (cnt > 0).astype(f32), 0.0), axis=0, keepdims=True)
    total = jnp.max(pend, axis=0, keepdims=True)
    rows = [e_of, row0, nsub, n_items, zrow, zval, total]
    meta_ref[...] = _stack_rows([r.astype(i32) for r in rows], SUBLANES)


def _route(logits_t):
    ne, t = logits_t.shape
    chunk = 512
    return pl.pallas_call(
        functools.partial(_route_kernel, chunk=chunk),
        out_shape=[
            jax.ShapeDtypeStruct((TOP_K, t), i32),
            jax.ShapeDtypeStruct((t, LANES), f32),
            jax.ShapeDtypeStruct((SUBLANES, LANES), i32),
        ],
        scratch_shapes=[pltpu.VMEM((TOP_K, t), f32), pltpu.VMEM((TOP_K, t), f32)],
        compiler_params=pltpu.CompilerParams(vmem_limit_bytes=VMEM_LIMIT_BYTES),
        name="route",
    )(logits_t)


_M_EXPERT, _M_ROW0, _M_NSUB, _M_NITEMS, _M_ZROW, _M_ZVALID, _M_TOTAL = range(7)


def _zero_tail(meta, zbuf, dst_ref, sem):
    total = pl.multiple_of(meta[_M_TOTAL, 0], ROW_PAD)
    n = (dst_ref.shape[0] - total) // ROW_PAD

    def copy(r):
        return pltpu.make_async_copy(zbuf, dst_ref.at[pl.ds(total + r * ROW_PAD, ROW_PAD), :], sem)

    def start(r, c):
        copy(r).start()
        return c

    def wait(r, c):
        copy(r).wait()
        return c

    lax.fori_loop(0, n, start, 0)
    lax.fori_loop(0, n, wait, 0)


def _dispatch_kernel(meta, pos_ref, x_ref, g_ref, xs_ref, h_scr, zbuf, sem, zsem, *, tmd, ne):
    i = pl.program_id(0)
    h_scr[...] = _rms(x_ref[...], g_ref[...])

    def zero_copy(e):
        start = pl.multiple_of(meta[_M_ZROW, e], ROW_PAD)
        return pltpu.make_async_copy(zbuf, xs_ref.at[pl.ds(start, ROW_PAD), :], zsem)

    @pl.when(i == 0)
    def _():
        zbuf[...] = jnp.zeros_like(zbuf)

        def zstart(e, c):
            @pl.when(meta[_M_ZVALID, e] > 0)
            def _():
                zero_copy(e).start()
            return c

        def zwait(e, c):
            @pl.when(meta[_M_ZVALID, e] > 0)
            def _():
                zero_copy(e).wait()
            return c

        lax.fori_loop(0, ne, zstart, 0)
        lax.fori_loop(0, ne, zwait, 0)
        _zero_tail(meta, zbuf, xs_ref, zsem)

    def row_copy(r, k):
        return pltpu.make_async_copy(h_scr.at[pl.ds(r, 1), :],
                                     xs_ref.at[pl.ds(pos_ref[k, r], 1), :], sem)

    def issue(r, c):
        for k in range(TOP_K):
            row_copy(r, k).start()
        return c

    lax.fori_loop(0, tmd, issue, 0)
    for _ in range(TOP_K):
        pltpu.make_async_copy(h_scr, xs_ref.at[pl.ds(0, tmd), :], sem).wait()


def _dispatch(meta, pos_t, x1, g_ffn, n_rows, ne):
    t, d = x1.shape
    tmd = 256
    return pl.pallas_call(
        functools.partial(_dispatch_kernel, tmd=tmd, ne=ne),
        grid_spec=pltpu.PrefetchScalarGridSpec(
            num_scalar_prefetch=1,
            grid=(t // tmd,),
            in_specs=[
                pl.BlockSpec((TOP_K, tmd), lambda i, m: (0, i), memory_space=pltpu.SMEM),
                pl.BlockSpec((tmd, d), lambda i, m: (i, 0)),
                pl.BlockSpec((1, d), lambda i, m: (0, 0)),
            ],
            out_specs=pl.BlockSpec(memory_space=pl.ANY),
            scratch_shapes=[pltpu.VMEM((tmd, d), f32), pltpu.VMEM((ROW_PAD, d), f32),
                            pltpu.SemaphoreType.DMA, pltpu.SemaphoreType.DMA],
        ),
        out_shape=jax.ShapeDtypeStruct((n_rows, d), f32),
        compiler_params=_cparams(("arbitrary",)),
        name="dispatch",
    )(meta, pos_t, x1, g_ffn.reshape(1, d))


def _expert_kernel(meta, xs_ref, wgu_hbm, wd_hbm, bgu_ref, bd_ref, ys_ref,
                   x_scr, h_scr, wbuf, stage, y_scr, w_sem, ld_sem, st_sem,
                   *, nf, nd, tf, td):
    it = pl.program_id(0)
    n_items = meta[_M_NITEMS, 0]
    f_dim = wd_hbm.shape[1]

    def up_copies(e, f, slot):
        cols = pl.ds(pl.multiple_of(f * tf, tf), tf)
        cols_up = pl.ds(pl.multiple_of(f_dim + f * tf, tf), tf)
        return (pltpu.make_async_copy(wgu_hbm.at[e, :, cols], wbuf.at[slot, :, pl.ds(0, tf)], w_sem.at[slot]),
                pltpu.make_async_copy(wgu_hbm.at[e, :, cols_up], wbuf.at[slot, :, pl.ds(tf, tf)], w_sem.at[slot]))

    def down_copy(e, dcol, slot):
        cols = pl.ds(pl.multiple_of(dcol * td, td), td)
        return pltpu.make_async_copy(wd_hbm.at[e, :, cols], wbuf.at[slot, pl.ds(0, f_dim), :], w_sem.at[slot])

    def load_copy(first_row, r, slot):
        return pltpu.make_async_copy(xs_ref.at[pl.ds(first_row + r * ROW_PAD, ROW_PAD), :],
                                     stage.at[slot], ld_sem.at[slot])

    def rows_of(r):
        return pl.ds(pl.multiple_of(r * ROW_PAD, ROW_PAD), ROW_PAD)

    @pl.when(it < n_items)
    def _():
        e = meta[_M_EXPERT, it]
        row0 = pl.multiple_of(meta[_M_ROW0, it], ROW_PAD)
        nsub = meta[_M_NSUB, it]
        has_next = it + 1 < n_items
        nxt = jnp.minimum(it + 1, n_items - 1)
        next_row0 = pl.multiple_of(meta[_M_ROW0, nxt], ROW_PAD)
        next_nsub = meta[_M_NSUB, nxt]
        xcur = it % 2

        @pl.when(it == 0)
        def _():
            for c in up_copies(e, 0, 0):
                c.start()
            stage[0] = jnp.zeros(stage.shape[1:], f32)
            _zero_tail(meta, stage.at[0], ys_ref, ld_sem.at[0])
            load_copy(row0, 0, 0).start()

            def body(r, c):
                slot = r % 2
                load_copy(row0, r, slot).wait()

                @pl.when(r + 1 < nsub)
                def _():
                    load_copy(row0, r + 1, 1 - slot).start()

                x_scr[0, rows_of(r), :] = stage[slot].astype(bf16)
                return c

            lax.fori_loop(0, nsub, body, 0)

        def for_row_tiles(fn):
            per = COMPUTE_ROWS // ROW_PAD
            n_full = nsub // per

            def pair(r, c):
                first = pl.multiple_of(r * (2 * COMPUTE_ROWS), COMPUTE_ROWS)
                fn(first, COMPUTE_ROWS)
                fn(first + COMPUTE_ROWS, COMPUTE_ROWS)
                return c

            lax.fori_loop(0, n_full // 2, pair, 0)

            @pl.when(n_full % 2 == 1)
            def _():
                fn(pl.multiple_of((n_full - 1) * COMPUTE_ROWS, COMPUTE_ROWS), COMPUTE_ROWS)

            for tail in range(1, per):
                @pl.when(nsub % per == tail)
                def _():
                    fn(pl.multiple_of(n_full * COMPUTE_ROWS, COMPUTE_ROWS), tail * ROW_PAD)

        def up_chunk(f, carry):
            slot = f % 2
            for c in up_copies(e, f, slot):
                c.wait()

            @pl.when(f + 1 < nf)
            def _():
                for c in up_copies(e, f + 1, 1 - slot):
                    c.start()

            @pl.when(f + 1 == nf)
            def _():
                down_copy(e, 0, 1 - slot).start()

            bg = bgu_ref[f]
            bu = bgu_ref[nf + f]

            for j in range(stage.shape[0]):
                @pl.when(has_next & (f + j * nf < next_nsub))
                def _():
                    load_copy(next_row0, f + j * nf, j).start()

            def up_tile(first, n):
                rows = pl.ds(first, n)
                gu = jnp.dot(x_scr[xcur, rows, :], wbuf[slot].astype(bf16), preferred_element_type=f32)
                g = jnp.minimum(gu[:, :tf] + bg, SWIGLU_LIMIT)
                u = jnp.clip(gu[:, tf:] + bu, -SWIGLU_LIMIT, SWIGLU_LIMIT)
                act = (u + 1.0) * (g * jax.nn.sigmoid(SWIGLU_ALPHA * g))
                h_scr[f, rows, :] = act.astype(bf16)

            for_row_tiles(up_tile)

            for j in range(stage.shape[0]):
                @pl.when(has_next & (f + j * nf < next_nsub))
                def _():
                    load_copy(next_row0, f + j * nf, j).wait()
                    x_scr[1 - xcur, rows_of(f + j * nf), :] = stage[j].astype(bf16)

            return carry

        lax.fori_loop(0, nf, up_chunk, 0)

        def store_copy(first, dcol, sl):
            dst = ys_ref.at[pl.ds(row0 + first, ROW_PAD), pl.ds(pl.multiple_of(dcol * td, td), td)]
            return pltpu.make_async_copy(y_scr.at[sl, pl.ds(first, ROW_PAD), :], dst, st_sem.at[sl])

        def wait_stores(sl, count):
            def w(r, c):
                store_copy(0, 0, sl).wait()
                return c
            lax.fori_loop(0, count, w, 0)

        def down_chunk(dcol, carry):
            slot = (nf + dcol) % 2
            yslot = dcol % 2
            down_copy(e, dcol, slot).wait()

            @pl.when(dcol + 1 < nd)
            def _():
                down_copy(e, dcol + 1, 1 - slot).start()

            @pl.when((dcol + 1 == nd) & has_next)
            def _():
                for c in up_copies(meta[_M_EXPERT, nxt], 0, 0):
                    c.start()

            bd = bd_ref[dcol]

            @pl.when(dcol >= 2)
            def _():
                wait_stores(yslot, nsub)

            @pl.when((dcol < 2) & (it > 0))
            def _():
                wait_stores(yslot, meta[_M_NSUB, jnp.maximum(it - 1, 0)])

            def down_tile(first, n):
                rows = pl.ds(first, n)
                hidden = jnp.concatenate([h_scr[cf, rows, :] for cf in range(nf)], axis=1)
                y_scr[yslot, rows, :] = bd + jnp.dot(hidden, wbuf[slot, 0:f_dim, :].astype(bf16),
                                                     preferred_element_type=f32)
                for part in range(n // ROW_PAD):
                    store_copy(first + part * ROW_PAD, dcol, yslot).start()

            for_row_tiles(down_tile)

            return carry

        lax.fori_loop(0, nd, down_chunk, 0)

        @pl.when(jnp.logical_not(has_next))
        def _():
            wait_stores(0, nsub)
            wait_stores(1, nsub)


def _experts(meta, xs, w_gate_up, b_gate_up, w_down, b_down):
    n_rows, d = xs.shape
    ne, _, f2 = w_gate_up.shape
    f = f2 // 2
    tf, td = 256, 512
    nf, nd = f // tf, d // td
    max_items = (n_rows + ne * (EXPERT_CAP - ROW_PAD)) // EXPERT_CAP
    stage_slots = max(2, pl.cdiv(EXPERT_CAP // ROW_PAD, nf))
    assert max_items <= MAX_ITEMS and td == 2 * tf and f <= d
    assert nf % 2 == 0 and nd % 2 == 0

    def expert(i, m):
        return m[_M_EXPERT, jnp.minimum(i, m[_M_NITEMS, 0] - 1)]

    return pl.pallas_call(
        functools.partial(_expert_kernel, nf=nf, nd=nd, tf=tf, td=td),
        grid_spec=pltpu.PrefetchScalarGridSpec(
            num_scalar_prefetch=1,
            grid=(max_items,),
            in_specs=[
                pl.BlockSpec(memory_space=pl.ANY),
                pl.BlockSpec(memory_space=pl.ANY),
                pl.BlockSpec(memory_space=pl.ANY),
                pl.BlockSpec((None, 2 * nf, 1, tf), lambda i, m: (expert(i, m), 0, 0, 0)),
                pl.BlockSpec((None, nd, 1, td), lambda i, m: (expert(i, m), 0, 0, 0)),
            ],
            out_specs=pl.BlockSpec(memory_space=pl.ANY),
            scratch_shapes=[
                pltpu.VMEM((2, EXPERT_CAP, d), bf16),
                pltpu.VMEM((nf, EXPERT_CAP, tf), bf16),
                pltpu.VMEM((2, d, 2 * tf), f32),
                pltpu.VMEM((stage_slots, ROW_PAD, d), f32),
                pltpu.VMEM((2, EXPERT_CAP, td), f32),
                pltpu.SemaphoreType.DMA((2,)),
                pltpu.SemaphoreType.DMA((stage_slots,)),
                pltpu.SemaphoreType.DMA((2,)),
            ],
        ),
        out_shape=jax.ShapeDtypeStruct((n_rows, d), f32),
        compiler_params=_cparams(("arbitrary",)),
        name="experts",
    )(meta, xs, w_gate_up, w_down, b_gate_up.reshape(ne, 2 * nf, 1, tf), b_down.reshape(ne, nd, 1, td))


def _combine_kernel(pos_cur, pos_nxt, gcol_ref, x_ref, ys_ref, o_ref, gbuf, sem, *, tmc, n_tiles):
    i = pl.program_id(0)
    slot = i % 2

    def row_copy(pos_ref, sl, r, k):
        return pltpu.make_async_copy(ys_ref.at[pl.ds(pos_ref[k, r], 1), :],
                                     gbuf.at[sl, k, pl.ds(r, 1), :], sem.at[sl])

    def issue(pos_ref, sl):
        def body(r, c):
            for k in range(TOP_K):
                row_copy(pos_ref, sl, r, k).start()
            return c
        lax.fori_loop(0, tmc, body, 0)

    @pl.when(i == 0)
    def _():
        issue(pos_cur, 0)

    @pl.when(i + 1 < n_tiles)
    def _():
        issue(pos_nxt, 1 - slot)

    for k in range(TOP_K):
        pltpu.make_async_copy(ys_ref.at[pl.ds(0, tmc), :], gbuf.at[slot, k], sem.at[slot]).wait()
    acc = x_ref[...]
    for k in range(TOP_K):
        acc = acc + gcol_ref[:, k:k + 1] * gbuf[slot, k]
    o_ref[...] = acc


def _combine(pos_t, gcol, x1, ys):
    t, d = x1.shape
    tmc = 128
    n_tiles = t // tmc
    return pl.pallas_call(
        functools.partial(_combine_kernel, tmc=tmc, n_tiles=n_tiles),
        grid=(n_tiles,),
        in_specs=[
            pl.BlockSpec((TOP_K, tmc), lambda i: (0, i), memory_space=pltpu.SMEM),
            pl.BlockSpec((TOP_K, tmc), lambda i: (0, jnp.minimum(i + 1, n_tiles - 1)), memory_space=pltpu.SMEM),
            pl.BlockSpec((tmc, LANES), lambda i: (i, 0)),
            pl.BlockSpec((tmc, d), lambda i: (i, 0)),
            pl.BlockSpec(memory_space=pl.ANY),
        ],
        out_specs=pl.BlockSpec((tmc, d), lambda i: (i, 0)),
        out_shape=jax.ShapeDtypeStruct((t, d), f32),
        scratch_shapes=[pltpu.VMEM((2, TOP_K, tmc, d), f32), pltpu.SemaphoreType.DMA((2,))],
        compiler_params=_cparams(("arbitrary",)),
        name="combine",
    )(pos_t, pos_t, gcol, x1, ys)


def _layer(x2, batch, seq, g_mix, w_in, b_forget, g_q, g_k, w_conv, g_attn_out, g_conv_out, w_out,
           g_ffn, w_router, b_router, w_gate_up, b_gate_up, w_down, b_down):
    t, d = x2.shape
    nheads = b_forget.shape[0]
    att_width = nheads * HEAD_DIM
    ne = w_router.shape[1]
    c0 = 3 * att_width
    w_t = w_in.T
    conv_width = (w_t.shape[0] - c0 - nheads) // 3

    h, c = _prenorm(x2, g_mix, w_t, c0, b_forget, seq)
    qkv = _qkv(h, w_t, g_q, g_k, att_width)
    conv = _conv(h, w_t, c0 + nheads, conv_width, w_conv, seq)
    att = _attention(qkv, c, batch, seq, nheads)
    x1 = _outproj(att, conv, g_attn_out, g_conv_out, w_out.astype(bf16), x2)

    pos_t, gcol, meta = _route(_router(x1, g_ffn, w_router, b_router))
    n_rows = t * TOP_K + ne * ROW_PAD
    xs = _dispatch(meta, pos_t, x1, g_ffn, n_rows, ne)
    ys = _experts(meta, xs, w_gate_up, b_gate_up, w_down, b_down)
    return _combine(pos_t, gcol, x1, ys)


def kernel(x, g_mix, w_in, b_forget, g_q, g_k, w_conv, g_attn_out, g_conv_out, w_out, g_ffn, w_router,
           b_router, w_gate_up, b_gate_up, w_down, b_down):
    b, s, d = x.shape
    x2 = x.reshape(b * s, d)
    for l in range(g_mix.shape[0]):
        x2 = _layer(x2, b, s, g_mix[l], w_in[l], b_forget[l], g_q[l], g_k[l], w_conv[l], g_attn_out[l],
                    g_conv_out[l], w_out[l], g_ffn[l], w_router[l], b_router[l], w_gate_up[l],
                    b_gate_up[l], w_down[l], b_down[l])
    return x2.reshape(b, s, d)
```

```python
import functools

import jax
import jax.numpy as jnp
import numpy as np
from jax import lax
from jax.experimental import pallas as pl
from jax.experimental.pallas import tpu as pltpu

f32 = jnp.float32
bf16 = jnp.bfloat16
i32 = jnp.int32

HEAD_DIM = 128
LANES = 128
SUBLANES = 8
TOP_K = 4
RMS_EPS = 1e-6
SWIGLU_ALPHA = 1.702
SWIGLU_LIMIT = 7.0
VMEM_LIMIT_BYTES = 56 * 1024 * 1024

ROW_PAD = 128
EXPERT_CAP = 1152
COMPUTE_ROWS = 512
MAX_ITEMS = 128


def _cparams(sem):
    return pltpu.CompilerParams(dimension_semantics=sem, vmem_limit_bytes=VMEM_LIMIT_BYTES)


def _log_sigmoid(z):
    return jnp.minimum(z, 0.0) - jnp.log1p(jnp.exp(-jnp.abs(z)))


def _split3(a):
    hi = a.astype(bf16)
    r1 = a - hi.astype(f32)
    mid = r1.astype(bf16)
    lo = (r1 - mid.astype(f32)).astype(bf16)
    return hi, mid, lo


def _dot_nt(a, b):
    return lax.dot_general(a, b, (((1,), (1,)), ((), ())), preferred_element_type=f32)


def _rms(y, g):
    ms = jnp.mean(y * y, axis=-1, keepdims=True)
    return y * lax.rsqrt(ms + RMS_EPS) * g


def _prenorm_kernel(x_ref, g_ref, wf_ref, bf_ref, h_ref, c_ref, carry, *, tiles_per_seq):
    i = pl.program_id(0)
    hb = _rms(x_ref[...], g_ref[...]).astype(bf16)
    h_ref[...] = hb
    logf = _log_sigmoid(_dot_nt(hb, wf_ref[...].astype(bf16)) + bf_ref[...])
    tm = hb.shape[0]
    a = lax.broadcasted_iota(i32, (tm, tm), 0)
    b = lax.broadcasted_iota(i32, (tm, tm), 1)
    lower = (b <= a).astype(bf16)
    c = sum(jnp.dot(lower, p, preferred_element_type=f32) for p in _split3(logf))

    @pl.when(i % tiles_per_seq == 0)
    def _():
        carry[...] = jnp.zeros_like(carry)

    c = c + carry[...]
    c_ref[...] = c
    carry[...] = c[tm - 1:tm, :]


def _prenorm(x2, g_mix, w_t, f_row0, b_f, seq):
    t, d = x2.shape
    nh = b_f.shape[0]
    tm = 512
    assert f_row0 % LANES == 0 and nh <= LANES
    bfr = jnp.pad(b_f, (0, LANES - nh)).reshape(1, LANES)
    return pl.pallas_call(
        functools.partial(_prenorm_kernel, tiles_per_seq=seq // tm),
        grid=(t // tm,),
        in_specs=[
            pl.BlockSpec((tm, d), lambda i: (i, 0)),
            pl.BlockSpec((1, d), lambda i: (0, 0)),
            pl.BlockSpec((LANES, d), lambda i: (f_row0 // LANES, 0)),
            pl.BlockSpec((1, LANES), lambda i: (0, 0)),
        ],
        out_specs=[
            pl.BlockSpec((tm, d), lambda i: (i, 0)),
            pl.BlockSpec((tm, LANES), lambda i: (i, 0)),
        ],
        out_shape=[
            jax.ShapeDtypeStruct((t, d), bf16),
            jax.ShapeDtypeStruct((t, LANES), f32),
        ],
        scratch_shapes=[pltpu.VMEM((1, LANES), f32)],
        compiler_params=_cparams(("arbitrary",)),
        name="prenorm",
    )(x2, g_mix.reshape(1, d), w_t, bfr)


def _qkv_kernel(h_ref, w_ref, g_ref, post_ref, o_ref, *, n_norm_blocks):
    normed = pl.program_id(0) < n_norm_blocks
    tm, tn = o_ref.shape
    for half in range(2):
        rows = slice(half * (tm // 2), (half + 1) * (tm // 2))
        y = _dot_nt(h_ref[rows, :], w_ref[...].astype(bf16))
        for hh in range(tn // HEAD_DIM):
            sl = slice(hh * HEAD_DIM, (hh + 1) * HEAD_DIM)
            yh = y[:, sl]
            inv = lax.rsqrt(jnp.mean(yh * yh, axis=-1, keepdims=True) + RMS_EPS)
            o_ref[rows, sl] = (yh * jnp.where(normed, inv, 1.0) * g_ref[:, sl] * post_ref[:, sl]).astype(bf16)


def _qkv(h, w_t, g_q, g_k, att_width):
    t, d = h.shape
    n = 3 * att_width
    tm, tn = 1024, 512
    nheads = att_width // HEAD_DIM
    ones = jnp.ones((att_width,), f32)
    gain = jnp.concatenate([jnp.tile(g_q, nheads), jnp.tile(g_k, nheads), ones]).reshape(1, n)
    post = jnp.concatenate([ones * (1.0 / np.sqrt(HEAD_DIM)), ones, ones]).reshape(1, n)
    return pl.pallas_call(
        functools.partial(_qkv_kernel, n_norm_blocks=2 * att_width // tn),
        grid=(n // tn, t // tm),
        in_specs=[
            pl.BlockSpec((tm, d), lambda j, i: (i, 0)),
            pl.BlockSpec((tn, d), lambda j, i: (j, 0)),
            pl.BlockSpec((1, tn), lambda j, i: (0, j)),
            pl.BlockSpec((1, tn), lambda j, i: (0, j)),
        ],
        out_specs=pl.BlockSpec((tm, tn), lambda j, i: (i, j)),
        out_shape=jax.ShapeDtypeStruct((t, n), bf16),
        compiler_params=_cparams(("arbitrary", "arbitrary")),
        name="qkv",
    )(h, w_t, gain, post)


def _conv_kernel(h_ref, wb_ref, wc_ref, wx_ref, wk_ref, o_ref, carry, *, tiles_per_seq):
    i = pl.program_id(1)
    h = h_ref[...]
    gb = _dot_nt(h, wb_ref[...].astype(bf16))
    gc = _dot_nt(h, wc_ref[...].astype(bf16))
    xc = _dot_nt(h, wx_ref[...].astype(bf16))
    u = gc * xc
    tm = u.shape[0]

    @pl.when(i % tiles_per_seq == 0)
    def _():
        carry[...] = jnp.zeros_like(carry)

    p2 = carry[SUBLANES - 2:SUBLANES - 1, :]
    p1 = carry[SUBLANES - 1:SUBLANES, :]
    row = lax.broadcasted_iota(i32, u.shape, 0)
    u1 = jnp.where(row == 0, p1, pltpu.roll(u, 1, 0))
    u2 = jnp.where(row == 0, p2, jnp.where(row == 1, p1, pltpu.roll(u, 2, 0)))
    y = wk_ref[0:1, :] * u2 + wk_ref[1:2, :] * u1 + wk_ref[2:3, :] * u
    o_ref[...] = gb * y
    carry[...] = u[tm - SUBLANES:tm, :]


def _conv(h, w_t, row0, width, w_conv, seq):
    t, d = h.shape
    tm, tn = 1024, 256
    assert row0 % SUBLANES == 0 and width % tn == 0

    def w_spec(part):
        return pl.BlockSpec((pl.Element(tn), pl.Element(d)),
                            lambda j, i: (pl.multiple_of(row0 + part * width + j * tn, SUBLANES), 0))

    return pl.pallas_call(
        functools.partial(_conv_kernel, tiles_per_seq=seq // tm),
        grid=(width // tn, t // tm),
        in_specs=[
            pl.BlockSpec((tm, d), lambda j, i: (i, 0)),
            w_spec(0), w_spec(1), w_spec(2),
            pl.BlockSpec((3, tn), lambda j, i: (0, j)),
        ],
        out_specs=pl.BlockSpec((tm, tn), lambda j, i: (i, j)),
        out_shape=jax.ShapeDtypeStruct((t, width), f32),
        scratch_shapes=[pltpu.VMEM((SUBLANES, tn), f32)],
        compiler_params=_cparams(("arbitrary", "arbitrary")),
        name="conv",
    )(h, w_t, w_t, w_t, w_conv)


def _bias_lanes(c, ones_first):
    n = c.shape[0]
    lane = lax.broadcasted_iota(i32, (n, LANES), 1)
    out = jnp.zeros((n, LANES), f32)
    base_c, base_1 = (3, 0) if ones_first else (0, 3)
    for k, piece in enumerate(_split3(c)):
        out = jnp.where(lane == base_c + k, piece.astype(f32), out)
        out = jnp.where(lane == base_1 + k, 1.0, out)
    return out.astype(bf16)


def _attn_kernel(q_ref, k_ref, v_ref, cq_ref, ck_ref, o_ref, k_aug, *, tq):
    hd = pl.program_id(1)
    qi = pl.program_id(2)

    def head_column(c_ref):
        lane = lax.broadcasted_iota(i32, c_ref.shape, 1)
        return jnp.sum(jnp.where(lane == hd, c_ref[...], 0.0), axis=1, keepdims=True)

    @pl.when(qi == 0)
    def _():
        k_aug[:, :HEAD_DIM] = k_ref[...]
        k_aug[:, HEAD_DIM:] = _bias_lanes(-head_column(ck_ref), ones_first=True)

    q = jnp.concatenate([q_ref[...], _bias_lanes(head_column(cq_ref), ones_first=False)], axis=1)
    def rows_of(j):
        return pl.ds(pl.multiple_of(j * tq, tq), tq)

    def scores(j):
        return _dot_nt(q, k_aug[rows_of(j), :])

    def update(s, vj, state):
        m, l, acc = state
        m_new = jnp.maximum(m, jnp.max(s, axis=1, keepdims=True))
        alpha = jnp.exp(m - m_new)
        p = jnp.exp(s - m_new)
        l = alpha * l + jnp.sum(p, axis=1, keepdims=True)
        acc = alpha * acc + jnp.dot(p.astype(bf16), vj, preferred_element_type=f32)
        return m_new, l, acc

    def body(j, state):
        return update(scores(j), v_ref[rows_of(j), :], state)

    init = (jnp.full((tq, 1), -jnp.inf, f32), jnp.zeros((tq, 1), f32), jnp.zeros((tq, HEAD_DIM), f32))
    state = lax.fori_loop(0, qi, body, init)
    r = lax.broadcasted_iota(i32, (tq, tq), 0)
    c = lax.broadcasted_iota(i32, (tq, tq), 1)
    _, l, acc = update(jnp.where(c <= r, scores(qi), -jnp.inf), v_ref[rows_of(qi), :], state)
    o_ref[...] = acc / l


def _attention(qkv, c, batch, seq, nheads):
    t = qkv.shape[0]
    tq = 512
    nq = seq // tq
    return pl.pallas_call(
        functools.partial(_attn_kernel, tq=tq),
        grid=(batch, nheads, nq),
        in_specs=[
            pl.BlockSpec((tq, HEAD_DIM), lambda b, h, qi: (b * nq + qi, h)),
            pl.BlockSpec((seq, HEAD_DIM), lambda b, h, qi: (b, nheads + h)),
            pl.BlockSpec((seq, HEAD_DIM), lambda b, h, qi: (b, 2 * nheads + h)),
            pl.BlockSpec((tq, LANES), lambda b, h, qi: (b * nq + qi, 0)),
            pl.BlockSpec((seq, LANES), lambda b, h, qi: (b, 0)),
        ],
        out_specs=pl.BlockSpec((tq, HEAD_DIM), lambda b, h, qi: (b * nq + qi, h)),
        out_shape=jax.ShapeDtypeStruct((t, nheads * HEAD_DIM), f32),
        scratch_shapes=[pltpu.VMEM((seq, 2 * HEAD_DIM), bf16)],
        compiler_params=_cparams(("arbitrary", "arbitrary", "arbitrary")),
        name="attention",
    )(qkv, qkv, qkv, c, c)


def _outproj_kernel(att_ref, conv_ref, ga_ref, gc_ref, w_ref, x_ref, o_ref, mixed):
    j = pl.program_id(1)
    wa = att_ref.shape[1]

    @pl.when(j == 0)
    def _():
        mixed[:, :wa] = _rms(att_ref[...], ga_ref[...]).astype(bf16)
        mixed[:, wa:] = _rms(conv_ref[...], gc_ref[...]).astype(bf16)

    o_ref[...] = x_ref[...] + jnp.dot(mixed[...], w_ref[...], preferred_element_type=f32)


def _outproj(att, conv, g_a, g_c, w_o, x2):
    t, d = x2.shape
    wa, wc = att.shape[1], conv.shape[1]
    tm, tn = 512, 1024
    return pl.pallas_call(
        _outproj_kernel,
        grid=(t // tm, d // tn),
        in_specs=[
            pl.BlockSpec((tm, wa), lambda i, j: (i, 0)),
            pl.BlockSpec((tm, wc), lambda i, j: (i, 0)),
            pl.BlockSpec((1, wa), lambda i, j: (0, 0)),
            pl.BlockSpec((1, wc), lambda i, j: (0, 0)),
            pl.BlockSpec((wa + wc, tn), lambda i, j: (0, j)),
            pl.BlockSpec((tm, tn), lambda i, j: (i, j)),
        ],
        out_specs=pl.BlockSpec((tm, tn), lambda i, j: (i, j)),
        out_shape=jax.ShapeDtypeStruct((t, d), f32),
        scratch_shapes=[pltpu.VMEM((tm, wa + wc), bf16)],
        compiler_params=_cparams(("arbitrary", "arbitrary")),
        name="outproj",
    )(att, conv, g_a.reshape(1, wa), g_c.reshape(1, wc), w_o, x2)


def _router_kernel(x_ref, g_ref, wrt_ref, br_ref, lg_ref):
    h_hi, h_lo, _ = _split3(_rms(x_ref[...], g_ref[...]))
    w_hi, w_lo, _ = _split3(wrt_ref[...])
    lg_ref[...] = _dot_nt(w_hi, h_hi) + (_dot_nt(w_lo, h_hi) + _dot_nt(w_hi, h_lo)) + br_ref[...]


def _router(x1, g_ffn, w_router, b_router):
    t, d = x1.shape
    ne = w_router.shape[1]
    tm = 512
    return pl.pallas_call(
        _router_kernel,
        grid=(t // tm,),
        in_specs=[
            pl.BlockSpec((tm, d), lambda i: (i, 0)),
            pl.BlockSpec((1, d), lambda i: (0, 0)),
            pl.BlockSpec((ne, d), lambda i: (0, 0)),
            pl.BlockSpec((ne, 1), lambda i: (0, 0)),
        ],
        out_specs=pl.BlockSpec((ne, tm), lambda i: (0, i)),
        out_shape=jax.ShapeDtypeStruct((ne, t), f32),
        compiler_params=_cparams(("arbitrary",)),
        name="router",
    )(x1, g_ffn.reshape(1, d), w_router.T, b_router.reshape(ne, 1))


def _cumsum_sublanes(x):
    n = x.shape[0]
    row = lax.broadcasted_iota(i32, x.shape, 0)
    d = 1
    while d < n:
        x = x + jnp.where(row >= d, pltpu.roll(x, d, 0), 0.0)
        d *= 2
    return x


def _stack_rows(rows, n):
    width = rows[0].shape[1]
    sub = lax.broadcasted_iota(i32, (n, width), 0)
    out = jnp.zeros((n, width), rows[0].dtype)
    for k, r in enumerate(rows):
        out = jnp.where(sub == k, r, out)
    return out


def _route_kernel(lg_ref, pos_ref, gcol_ref, meta_ref, esel, rsel, *, chunk):
    ne, t = lg_ref.shape
    e_iota = lax.broadcasted_iota(i32, (ne, chunk), 0).astype(f32)
    a = lax.broadcasted_iota(i32, (chunk, chunk), 0)
    b = lax.broadcasted_iota(i32, (chunk, chunk), 1)
    before = (a < b).astype(bf16)
    counts = jnp.zeros((ne, 1), f32)
    for c in range(t // chunk):
        sl = slice(c * chunk, (c + 1) * chunk)
        vals = lg_ref[:, sl]
        tops, idxs, hots = [], [], []
        for _ in range(TOP_K):
            m = jnp.max(vals, axis=0, keepdims=True)
            idx = jnp.min(jnp.where(vals == m, e_iota, ne), axis=0, keepdims=True)
            hot = e_iota == idx
            vals = jnp.where(hot, -jnp.inf, vals)
            tops.append(m)
            idxs.append(idx)
            hots.append(hot)
        ex = [jnp.exp(v - tops[0]) for v in tops]
        den = ex[0] + ex[1] + ex[2] + ex[3]
        gates = [e / den for e in ex]
        member = sum(h.astype(f32) for h in hots)
        rank = jnp.dot(member.astype(bf16), before, preferred_element_type=f32) + counts
        counts = counts + jnp.sum(member, axis=1, keepdims=True)
        for k in range(TOP_K):
            rsel[k:k + 1, sl] = jnp.sum(jnp.where(hots[k], rank, 0.0), axis=0, keepdims=True)
            esel[k:k + 1, sl] = idxs[k]
        g8 = _stack_rows(gates, SUBLANES)
        gpad = jnp.concatenate([g8, jnp.zeros((LANES - SUBLANES, chunk), f32)], axis=0)
        gcol_ref[sl, :] = gpad.T

    cnt = jnp.broadcast_to(counts, (ne, LANES))
    pcnt = jnp.ceil(cnt * (1.0 / ROW_PAD)) * ROW_PAD
    pend = _cumsum_sublanes(pcnt)
    pstart = pend - pcnt
    npass = jnp.floor((pcnt + (EXPERT_CAP - ROW_PAD)) / EXPERT_CAP + 0.5 * ROW_PAD / EXPERT_CAP)
    iend = _cumsum_sublanes(npass)
    istart = iend - npass
    n_items = jnp.max(iend, axis=0, keepdims=True)

    pstart_col = pstart[:, 0:1]
    for c in range(t // chunk):
        sl = slice(c * chunk, (c + 1) * chunk)
        for k in range(TOP_K):
            hot = e_iota == esel[k:k + 1, sl]
            ps = jnp.sum(jnp.where(hot, pstart_col, 0.0), axis=0, keepdims=True)
            pos_ref[k:k + 1, sl] = (rsel[k:k + 1, sl] + ps).astype(i32)

    e_sub = lax.broadcasted_iota(i32, (ne, LANES), 0)
    lane = lax.broadcasted_iota(i32, (ne, LANES), 1)
    lane_f = lane.astype(f32)
    item = lane_f[0:1, :]
    e_of = jnp.minimum(jnp.sum((iend <= lane_f).astype(f32), axis=0, keepdims=True), ne - 1.0)
    hot = e_sub.astype(f32) == e_of
    pick = lambda v: jnp.sum(jnp.where(hot, v, 0.0), axis=0, keepdims=True)
    p_i = item - pick(istart)
    row0 = pick(pstart) + p_i * EXPERT_CAP
    nrows = jnp.clip(pick(pcnt) - p_i * EXPERT_CAP, 0.0, float(EXPERT_CAP))
    nsub = nrows * (1.0 / ROW_PAD)
    diag = e_sub == lane
    zrow = jnp.sum(jnp.where(diag, pend - ROW_PAD, 0.0), axis=0, keepdims=True)
    zval = jnp.sum(jnp.where(diag, (cnt > 0).astype(f32), 0.0), axis=0, keepdims=True)
    total = jnp.max(pend, axis=0, keepdims=True)
    rows = [e_of, row0, nsub, n_items, zrow, zval, total]
    meta_ref[...] = _stack_rows([r.astype(i32) for r in rows], SUBLANES)


def _route(logits_t):
    ne, t = logits_t.shape
    chunk = 512
    return pl.pallas_call(
        functools.partial(_route_kernel, chunk=chunk),
        out_shape=[
            jax.ShapeDtypeStruct((TOP_K, t), i32),
            jax.ShapeDtypeStruct((t, LANES), f32),
            jax.ShapeDtypeStruct((SUBLANES, LANES), i32),
        ],
        scratch_shapes=[pltpu.VMEM((TOP_K, t), f32), pltpu.VMEM((TOP_K, t), f32)],
        compiler_params=pltpu.CompilerParams(vmem_limit_bytes=VMEM_LIMIT_BYTES),
        name="route",
    )(logits_t)


_M_EXPERT, _M_ROW0, _M_NSUB, _M_NITEMS, _M_ZROW, _M_ZVALID, _M_TOTAL = range(7)


def _zero_tail(meta, zbuf, dst_ref, sem):
    total = pl.multiple_of(meta[_M_TOTAL, 0], ROW_PAD)
    n = (dst_ref.shape[0] - total) // ROW_PAD

    def copy(r):
        return pltpu.make_async_copy(zbuf, dst_ref.at[pl.ds(total + r * ROW_PAD, ROW_PAD), :], sem)

    def start(r, c):
        copy(r).start()
        return c

    def wait(r, c):
        copy(r).wait()
        return c

    lax.fori_loop(0, n, start, 0)
    lax.fori_loop(0, n, wait, 0)


def _dispatch_kernel(meta, pos_ref, x_ref, g_ref, xs_ref, h_scr, zbuf, sem, zsem, *, tmd, ne):
    i = pl.program_id(0)
    h_scr[...] = _rms(x_ref[...], g_ref[...])

    def zero_copy(e):
        start = pl.multiple_of(meta[_M_ZROW, e], ROW_PAD)
        return pltpu.make_async_copy(zbuf, xs_ref.at[pl.ds(start, ROW_PAD), :], zsem)

    @pl.when(i == 0)
    def _():
        zbuf[...] = jnp.zeros_like(zbuf)

        def zstart(e, c):
            @pl.when(meta[_M_ZVALID, e] > 0)
            def _():
                zero_copy(e).start()
            return c

        def zwait(e, c):
            @pl.when(meta[_M_ZVALID, e] > 0)
            def _():
                zero_copy(e).wait()
            return c

        lax.fori_loop(0, ne, zstart, 0)
        lax.fori_loop(0, ne, zwait, 0)
        _zero_tail(meta, zbuf, xs_ref, zsem)

    def row_copy(r, k):
        return pltpu.make_async_copy(h_scr.at[pl.ds(r, 1), :],
                                     xs_ref.at[pl.ds(pos_ref[k, r], 1), :], sem)

    def issue(r, c):
        for k in range(TOP_K):
            row_copy(r, k).start()
        return c

    lax.fori_loop(0, tmd, issue, 0)
    for _ in range(TOP_K):
        pltpu.make_async_copy(h_scr, xs_ref.at[pl.ds(0, tmd), :], sem).wait()


def _dispatch(meta, pos_t, x1, g_ffn, n_rows, ne):
    t, d = x1.shape
    tmd = 256
    return pl.pallas_call(
        functools.partial(_dispatch_kernel, tmd=tmd, ne=ne),
        grid_spec=pltpu.PrefetchScalarGridSpec(
            num_scalar_prefetch=1,
            grid=(t // tmd,),
            in_specs=[
                pl.BlockSpec((TOP_K, tmd), lambda i, m: (0, i), memory_space=pltpu.SMEM),
                pl.BlockSpec((tmd, d), lambda i, m: (i, 0)),
                pl.BlockSpec((1, d), lambda i, m: (0, 0)),
            ],
            out_specs=pl.BlockSpec(memory_space=pl.ANY),
            scratch_shapes=[pltpu.VMEM((tmd, d), f32), pltpu.VMEM((ROW_PAD, d), f32),
                            pltpu.SemaphoreType.DMA, pltpu.SemaphoreType.DMA],
        ),
        out_shape=jax.ShapeDtypeStruct((n_rows, d), f32),
        compiler_params=_cparams(("arbitrary",)),
        name="dispatch",
    )(meta, pos_t, x1, g_ffn.reshape(1, d))


def _expert_kernel(meta, xs_ref, wgu_hbm, wd_hbm, bgu_ref, bd_ref, ys_ref,
                   x_scr, h_scr, wbuf, stage, y_scr, w_sem, ld_sem, st_sem,
                   *, nf, nd, tf, td):
    it = pl.program_id(0)
    n_items = meta[_M_NITEMS, 0]
    f_dim = wd_hbm.shape[1]

    def up_copies(e, f, slot):
        cols = pl.ds(pl.multiple_of(f * tf, tf), tf)
        cols_up = pl.ds(pl.multiple_of(f_dim + f * tf, tf), tf)
        return (pltpu.make_async_copy(wgu_hbm.at[e, :, cols], wbuf.at[slot, :, pl.ds(0, tf)], w_sem.at[2 * slot]),
                pltpu.make_async_copy(wgu_hbm.at[e, :, cols_up], wbuf.at[slot, :, pl.ds(tf, tf)],
                                      w_sem.at[2 * slot + 1]))

    def down_rows(dcol):
        return pl.ds(pl.multiple_of((dcol % 2) * f_dim, f_dim), f_dim)

    def down_copy(e, dcol):
        cols = pl.ds(pl.multiple_of(dcol * td, td), td)
        return pltpu.make_async_copy(wd_hbm.at[e, :, cols], wbuf.at[(dcol // 2) % 2, down_rows(dcol), :],
                                     w_sem.at[dcol % 4])

    def load_copy(first_row, r, slot):
        return pltpu.make_async_copy(xs_ref.at[pl.ds(first_row + r * ROW_PAD, ROW_PAD), :],
                                     stage.at[slot], ld_sem.at[slot])

    def rows_of(r):
        return pl.ds(pl.multiple_of(r * ROW_PAD, ROW_PAD), ROW_PAD)

    @pl.when(it < n_items)
    def _():
        e = meta[_M_EXPERT, it]
        row0 = pl.multiple_of(meta[_M_ROW0, it], ROW_PAD)
        nsub = meta[_M_NSUB, it]
        has_next = it + 1 < n_items
        nxt = jnp.minimum(it + 1, n_items - 1)
        next_row0 = pl.multiple_of(meta[_M_ROW0, nxt], ROW_PAD)
        next_nsub = meta[_M_NSUB, nxt]
        xcur = it % 2

        @pl.when(it == 0)
        def _():
            for c in up_copies(e, 0, 0):
                c.start()
            stage[0] = jnp.zeros(stage.shape[1:], f32)
            _zero_tail(meta, stage.at[0], ys_ref, ld_sem.at[0])
            load_copy(row0, 0, 0).start()

            def body(r, c):
                slot = r % 2
                load_copy(row0, r, slot).wait()

                @pl.when(r + 1 < nsub)
                def _():
                    load_copy(row0, r + 1, 1 - slot).start()

                x_scr[0, rows_of(r), :] = stage[slot].astype(bf16)
                return c

            lax.fori_loop(0, nsub, body, 0)

        def for_row_tiles(fn):
            per = COMPUTE_ROWS // ROW_PAD
            n_full = nsub // per

            def pair(r, c):
                first = pl.multiple_of(r * (2 * COMPUTE_ROWS), COMPUTE_ROWS)
                fn(first, COMPUTE_ROWS)
                fn(first + COMPUTE_ROWS, COMPUTE_ROWS)
                return c

            lax.fori_loop(0, n_full // 2, pair, 0)

            @pl.when(n_full % 2 == 1)
            def _():
                fn(pl.multiple_of((n_full - 1) * COMPUTE_ROWS, COMPUTE_ROWS), COMPUTE_ROWS)

            for tail in range(1, per):
                @pl.when(nsub % per == tail)
                def _():
                    fn(pl.multiple_of(n_full * COMPUTE_ROWS, COMPUTE_ROWS), tail * ROW_PAD)

        def up_chunk(f, carry):
            slot = f % 2
            for c in up_copies(e, f, slot):
                c.wait()

            @pl.when(f + 1 < nf)
            def _():
                for c in up_copies(e, f + 1, 1 - slot):
                    c.start()

            @pl.when(f + 1 == nf)
            def _():
                down_copy(e, 0).start()
                down_copy(e, 1).start()

            bg = bgu_ref[f]
            bu = bgu_ref[nf + f]

            for j in range(stage.shape[0]):
                @pl.when(has_next & (f + j * nf < next_nsub))
                def _():
                    load_copy(next_row0, f + j * nf, j).start()

            def up_tile(first, n):
                rows = pl.ds(first, n)
                gu = jnp.dot(x_scr[xcur, rows, :], wbuf[slot].astype(bf16), preferred_element_type=f32)
                g = jnp.minimum(gu[:, :tf] + bg, SWIGLU_LIMIT)
                u = jnp.clip(gu[:, tf:] + bu, -SWIGLU_LIMIT, SWIGLU_LIMIT)
                act = (u + 1.0) * (g * jax.nn.sigmoid(SWIGLU_ALPHA * g))
                h_scr[f, rows, :] = act.astype(bf16)

            for_row_tiles(up_tile)

            for j in range(stage.shape[0]):
                @pl.when(has_next & (f + j * nf < next_nsub))
                def _():
                    load_copy(next_row0, f + j * nf, j).wait()
                    x_scr[1 - xcur, rows_of(f + j * nf), :] = stage[j].astype(bf16)

            return carry

        lax.fori_loop(0, nf, up_chunk, 0)

        def store_copy(first, dcol, sl):
            dst = ys_ref.at[pl.ds(row0 + first, ROW_PAD), pl.ds(pl.multiple_of(dcol * td, td), td)]
            return pltpu.make_async_copy(y_scr.at[sl, pl.ds(first, ROW_PAD), :], dst, st_sem.at[sl])

        def wait_stores(sl, count):
            def w(r, c):
                store_copy(0, 0, sl).wait()
                return c
            lax.fori_loop(0, count, w, 0)

        def down_chunk(dcol, carry):
            slot = (dcol // 2) % 2
            yslot = dcol % 2
            down_copy(e, dcol).wait()

            @pl.when(dcol == 0)
            def _():
                down_copy(e, 2).start()
                down_copy(e, 3).start()

            @pl.when((dcol >= 1) & (dcol + 3 < nd))
            def _():
                down_copy(e, dcol + 3).start()

            @pl.when((dcol + 2 == nd) & has_next)
            def _():
                for c in up_copies(meta[_M_EXPERT, nxt], 0, 0):
                    c.start()

            bd = bd_ref[dcol]

            @pl.when(dcol >= 2)
            def _():
                wait_stores(yslot, nsub)

            @pl.when((dcol < 2) & (it > 0))
            def _():
                wait_stores(yslot, meta[_M_NSUB, jnp.maximum(it - 1, 0)])

            def down_tile(first, n):
                rows = pl.ds(first, n)
                hidden = jnp.concatenate([h_scr[cf, rows, :] for cf in range(nf)], axis=1)
                y_scr[yslot, rows, :] = bd + jnp.dot(hidden, wbuf[slot, down_rows(dcol), :].astype(bf16),
                                                     preferred_element_type=f32)
                for part in range(n // ROW_PAD):
                    store_copy(first + part * ROW_PAD, dcol, yslot).start()

            for_row_tiles(down_tile)

            return carry

        lax.fori_loop(0, nd, down_chunk, 0)

        @pl.when(jnp.logical_not(has_next))
        def _():
            wait_stores(0, nsub)
            wait_stores(1, nsub)


def _experts(meta, xs, w_gate_up, b_gate_up, w_down, b_down):
    n_rows, d = xs.shape
    ne, _, f2 = w_gate_up.shape
    f = f2 // 2
    tf, td = 256, 512
    nf, nd = f // tf, d // td
    max_items = (n_rows + ne * (EXPERT_CAP - ROW_PAD)) // EXPERT_CAP
    stage_slots = max(2, pl.cdiv(EXPERT_CAP // ROW_PAD, nf))
    assert max_items <= MAX_ITEMS and td == 2 * tf and f <= d
    assert nf % 2 == 0 and nd % 4 == 0 and 2 * f == d

    def expert(i, m):
        return m[_M_EXPERT, jnp.minimum(i, m[_M_NITEMS, 0] - 1)]

    return pl.pallas_call(
        functools.partial(_expert_kernel, nf=nf, nd=nd, tf=tf, td=td),
        grid_spec=pltpu.PrefetchScalarGridSpec(
            num_scalar_prefetch=1,
            grid=(max_items,),
            in_specs=[
                pl.BlockSpec(memory_space=pl.ANY),
                pl.BlockSpec(memory_space=pl.ANY),
                pl.BlockSpec(memory_space=pl.ANY),
                pl.BlockSpec((None, 2 * nf, 1, tf), lambda i, m: (expert(i, m), 0, 0, 0)),
                pl.BlockSpec((None, nd, 1, td), lambda i, m: (expert(i, m), 0, 0, 0)),
            ],
            out_specs=pl.BlockSpec(memory_space=pl.ANY),
            scratch_shapes=[
                pltpu.VMEM((2, EXPERT_CAP, d), bf16),
                pltpu.VMEM((nf, EXPERT_CAP, tf), bf16),
                pltpu.VMEM((2, d, 2 * tf), f32),
                pltpu.VMEM((stage_slots, ROW_PAD, d), f32),
                pltpu.VMEM((2, EXPERT_CAP, td), f32),
                pltpu.SemaphoreType.DMA((4,)),
                pltpu.SemaphoreType.DMA((stage_slots,)),
                pltpu.SemaphoreType.DMA((2,)),
            ],
        ),
        out_shape=jax.ShapeDtypeStruct((n_rows, d), f32),
        compiler_params=_cparams(("arbitrary",)),
        name="experts",
    )(meta, xs, w_gate_up, w_down, b_gate_up.reshape(ne, 2 * nf, 1, tf), b_down.reshape(ne, nd, 1, td))


def _combine_kernel(pos_cur, pos_nxt, gcol_ref, x_ref, ys_ref, o_ref, gbuf, sem, *, tmc, n_tiles):
    i = pl.program_id(0)
    slot = i % 2

    def row_copy(pos_ref, sl, r, k):
        return pltpu.make_async_copy(ys_ref.at[pl.ds(pos_ref[k, r], 1), :],
                                     gbuf.at[sl, k, pl.ds(r, 1), :], sem.at[sl])

    def issue(pos_ref, sl):
        def body(r, c):
            for k in range(TOP_K):
                row_copy(pos_ref, sl, r, k).start()
            return c
        lax.fori_loop(0, tmc, body, 0)

    @pl.when(i == 0)
    def _():
        issue(pos_cur, 0)

    @pl.when(i + 1 < n_tiles)
    def _():
        issue(pos_nxt, 1 - slot)

    for k in range(TOP_K):
        pltpu.make_async_copy(ys_ref.at[pl.ds(0, tmc), :], gbuf.at[slot, k], sem.at[slot]).wait()
    acc = x_ref[...]
    for k in range(TOP_K):
        acc = acc + gcol_ref[:, k:k + 1] * gbuf[slot, k]
    o_ref[...] = acc


def _combine(pos_t, gcol, x1, ys):
    t, d = x1.shape
    tmc = 128
    n_tiles = t // tmc
    return pl.pallas_call(
        functools.partial(_combine_kernel, tmc=tmc, n_tiles=n_tiles),
        grid=(n_tiles,),
        in_specs=[
            pl.BlockSpec((TOP_K, tmc), lambda i: (0, i), memory_space=pltpu.SMEM),
            pl.BlockSpec((TOP_K, tmc), lambda i: (0, jnp.minimum(i + 1, n_tiles - 1)), memory_space=pltpu.SMEM),
            pl.BlockSpec((tmc, LANES), lambda i: (i, 0)),
            pl.BlockSpec((tmc, d), lambda i: (i, 0)),
            pl.BlockSpec(memory_space=pl.ANY),
        ],
        out_specs=pl.BlockSpec((tmc, d), lambda i: (i, 0)),
        out_shape=jax.ShapeDtypeStruct((t, d), f32),
        scratch_shapes=[pltpu.VMEM((2, TOP_K, tmc, d), f32), pltpu.SemaphoreType.DMA((2,))],
        compiler_params=_cparams(("arbitrary",)),
        name="combine",
    )(pos_t, pos_t, gcol, x1, ys)


def _layer(x2, batch, seq, g_mix, w_in, b_forget, g_q, g_k, w_conv, g_attn_out, g_conv_out, w_out,
           g_ffn, w_router, b_router, w_gate_up, b_gate_up, w_down, b_down):
    t, d = x2.shape
    nheads = b_forget.shape[0]
    att_width = nheads * HEAD_DIM
    ne = w_router.shape[1]
    c0 = 3 * att_width
    w_t = w_in.T
    conv_width = (w_t.shape[0] - c0 - nheads) // 3

    h, c = _prenorm(x2, g_mix, w_t, c0, b_forget, seq)
    qkv = _qkv(h, w_t, g_q, g_k, att_width)
    conv = _conv(h, w_t, c0 + nheads, conv_width, w_conv, seq)
    att = _attention(qkv, c, batch, seq, nheads)
    x1 = _outproj(att, conv, g_attn_out, g_conv_out, w_out.astype(bf16), x2)

    pos_t, gcol, meta = _route(_router(x1, g_ffn, w_router, b_router))
    n_rows = t * TOP_K + ne * ROW_PAD
    xs = _dispatch(meta, pos_t, x1, g_ffn, n_rows, ne)
    ys = _experts(meta, xs, w_gate_up, b_gate_up, w_down, b_down)
    return _combine(pos_t, gcol, x1, ys)


def kernel(x, g_mix, w_in, b_forget, g_q, g_k, w_conv, g_attn_out, g_conv_out, w_out, g_ffn, w_router,
           b_router, w_gate_up, b_gate_up, w_down, b_down):
    b, s, d = x.shape
    x2 = x.reshape(b * s, d)
    for l in range(g_mix.shape[0]):
        x2 = _layer(x2, b, s, g_mix[l], w_in[l], b_forget[l], g_q[l], g_k[l], w_conv[l], g_attn_out[l],
                    g_conv_out[l], w_out[l], g_ffn[l], w_router[l], b_router[l], w_gate_up[l],
                    b_gate_up[l], w_down[l], b_down[l])
    return x2.reshape(b, s, d)
```

```python
import functools

import jax
import jax.numpy as jnp
import numpy as np
from jax import lax
from jax.experimental import pallas as pl
from jax.experimental.pallas import tpu as pltpu

f32 = jnp.float32
bf16 = jnp.bfloat16
i32 = jnp.int32

HEAD_DIM = 128
LANES = 128
SUBLANES = 8
TOP_K = 4
RMS_EPS = 1e-6
SWIGLU_ALPHA = 1.702
SWIGLU_LIMIT = 7.0
VMEM_LIMIT_BYTES = 56 * 1024 * 1024

ROW_PAD = 128
EXPERT_CAP = 1152
COMPUTE_ROWS = 512
MAX_ITEMS = 128


def _cparams(sem):
    return pltpu.CompilerParams(dimension_semantics=sem, vmem_limit_bytes=VMEM_LIMIT_BYTES)


def _log_sigmoid(z):
    return jnp.minimum(z, 0.0) - jnp.log1p(jnp.exp(-jnp.abs(z)))


def _split3(a):
    hi = a.astype(bf16)
    r1 = a - hi.astype(f32)
    mid = r1.astype(bf16)
    lo = (r1 - mid.astype(f32)).astype(bf16)
    return hi, mid, lo


def _dot_nt(a, b):
    return lax.dot_general(a, b, (((1,), (1,)), ((), ())), preferred_element_type=f32)


def _rms(y, g):
    ms = jnp.mean(y * y, axis=-1, keepdims=True)
    return y * lax.rsqrt(ms + RMS_EPS) * g


def _prenorm_kernel(x_ref, g_ref, wf_ref, bf_ref, h_ref, c_ref, carry, *, tiles_per_seq):
    i = pl.program_id(0)
    hb = _rms(x_ref[...], g_ref[...]).astype(bf16)
    h_ref[...] = hb
    logf = _log_sigmoid(_dot_nt(hb, wf_ref[...].astype(bf16)) + bf_ref[...])
    tm = hb.shape[0]
    a = lax.broadcasted_iota(i32, (tm, tm), 0)
    b = lax.broadcasted_iota(i32, (tm, tm), 1)
    lower = (b <= a).astype(bf16)
    c = sum(jnp.dot(lower, p, preferred_element_type=f32) for p in _split3(logf))

    @pl.when(i % tiles_per_seq == 0)
    def _():
        carry[...] = jnp.zeros_like(carry)

    c = c + carry[...]
    c_ref[...] = c
    carry[...] = c[tm - 1:tm, :]


def _prenorm(x2, g_mix, w_t, f_row0, b_f, seq):
    t, d = x2.shape
    nh = b_f.shape[0]
    tm = 512
    assert f_row0 % LANES == 0 and nh <= LANES
    bfr = jnp.pad(b_f, (0, LANES - nh)).reshape(1, LANES)
    return pl.pallas_call(
        functools.partial(_prenorm_kernel, tiles_per_seq=seq // tm),
        grid=(t // tm,),
        in_specs=[
            pl.BlockSpec((tm, d), lambda i: (i, 0)),
            pl.BlockSpec((1, d), lambda i: (0, 0)),
            pl.BlockSpec((LANES, d), lambda i: (f_row0 // LANES, 0)),
            pl.BlockSpec((1, LANES), lambda i: (0, 0)),
        ],
        out_specs=[
            pl.BlockSpec((tm, d), lambda i: (i, 0)),
            pl.BlockSpec((tm, LANES), lambda i: (i, 0)),
        ],
        out_shape=[
            jax.ShapeDtypeStruct((t, d), bf16),
            jax.ShapeDtypeStruct((t, LANES), f32),
        ],
        scratch_shapes=[pltpu.VMEM((1, LANES), f32)],
        compiler_params=_cparams(("arbitrary",)),
        name="prenorm",
    )(x2, g_mix.reshape(1, d), w_t, bfr)


def _qkv_kernel(h_ref, w_ref, g_ref, post_ref, o_ref, *, n_norm_blocks):
    normed = pl.program_id(0) < n_norm_blocks
    tm, tn = o_ref.shape
    for half in range(2):
        rows = slice(half * (tm // 2), (half + 1) * (tm // 2))
        y = _dot_nt(h_ref[rows, :], w_ref[...].astype(bf16))
        for hh in range(tn // HEAD_DIM):
            sl = slice(hh * HEAD_DIM, (hh + 1) * HEAD_DIM)
            yh = y[:, sl]
            inv = lax.rsqrt(jnp.mean(yh * yh, axis=-1, keepdims=True) + RMS_EPS)
            o_ref[rows, sl] = (yh * jnp.where(normed, inv, 1.0) * g_ref[:, sl] * post_ref[:, sl]).astype(bf16)


def _qkv(h, w_t, g_q, g_k, att_width):
    t, d = h.shape
    n = 3 * att_width
    tm, tn = 1024, 512
    nheads = att_width // HEAD_DIM
    ones = jnp.ones((att_width,), f32)
    gain = jnp.concatenate([jnp.tile(g_q, nheads), jnp.tile(g_k, nheads), ones]).reshape(1, n)
    post = jnp.concatenate([ones * (1.0 / np.sqrt(HEAD_DIM)), ones, ones]).reshape(1, n)
    return pl.pallas_call(
        functools.partial(_qkv_kernel, n_norm_blocks=2 * att_width // tn),
        grid=(n // tn, t // tm),
        in_specs=[
            pl.BlockSpec((tm, d), lambda j, i: (i, 0)),
            pl.BlockSpec((tn, d), lambda j, i: (j, 0)),
            pl.BlockSpec((1, tn), lambda j, i: (0, j)),
            pl.BlockSpec((1, tn), lambda j, i: (0, j)),
        ],
        out_specs=pl.BlockSpec((tm, tn), lambda j, i: (i, j)),
        out_shape=jax.ShapeDtypeStruct((t, n), bf16),
        compiler_params=_cparams(("arbitrary", "arbitrary")),
        name="qkv",
    )(h, w_t, gain, post)


def _conv_kernel(h_ref, wb_ref, wc_ref, wx_ref, wk_ref, o_ref, carry, *, tiles_per_seq):
    i = pl.program_id(1)
    tm, tn = o_ref.shape
    hm = tm // 2
    prev = jnp.where(i % tiles_per_seq == 0, 0.0, carry[...])
    p2 = prev[SUBLANES - 2:SUBLANES - 1, :]
    p1 = prev[SUBLANES - 1:SUBLANES, :]
    row = lax.broadcasted_iota(i32, (hm, tn), 0)
    for half in range(2):
        rows = slice(half * hm, (half + 1) * hm)
        h = h_ref[rows, :]
        gb = _dot_nt(h, wb_ref[...].astype(bf16))
        gc = _dot_nt(h, wc_ref[...].astype(bf16))
        xc = _dot_nt(h, wx_ref[...].astype(bf16))
        u = gc * xc
        u1 = jnp.where(row == 0, p1, pltpu.roll(u, 1, 0))
        u2 = jnp.where(row == 0, p2, jnp.where(row == 1, p1, pltpu.roll(u, 2, 0)))
        o_ref[rows, :] = gb * (wk_ref[0:1, :] * u2 + wk_ref[1:2, :] * u1 + wk_ref[2:3, :] * u)
        p2 = u[hm - 2:hm - 1, :]
        p1 = u[hm - 1:hm, :]
    carry[...] = u[hm - SUBLANES:hm, :]


def _conv(h, w_t, row0, width, w_conv, seq):
    t, d = h.shape
    tm, tn = 1024, 256
    assert row0 % SUBLANES == 0 and width % tn == 0

    def w_spec(part):
        return pl.BlockSpec((pl.Element(tn), pl.Element(d)),
                            lambda j, i: (pl.multiple_of(row0 + part * width + j * tn, SUBLANES), 0))

    return pl.pallas_call(
        functools.partial(_conv_kernel, tiles_per_seq=seq // tm),
        grid=(width // tn, t // tm),
        in_specs=[
            pl.BlockSpec((tm, d), lambda j, i: (i, 0)),
            w_spec(0), w_spec(1), w_spec(2),
            pl.BlockSpec((3, tn), lambda j, i: (0, j)),
        ],
        out_specs=pl.BlockSpec((tm, tn), lambda j, i: (i, j)),
        out_shape=jax.ShapeDtypeStruct((t, width), f32),
        scratch_shapes=[pltpu.VMEM((SUBLANES, tn), f32)],
        compiler_params=_cparams(("arbitrary", "arbitrary")),
        name="conv",
    )(h, w_t, w_t, w_t, w_conv)


def _bias_lanes(c, ones_first):
    n = c.shape[0]
    lane = lax.broadcasted_iota(i32, (n, LANES), 1)
    out = jnp.zeros((n, LANES), f32)
    base_c, base_1 = (3, 0) if ones_first else (0, 3)
    for k, piece in enumerate(_split3(c)):
        out = jnp.where(lane == base_c + k, piece.astype(f32), out)
        out = jnp.where(lane == base_1 + k, 1.0, out)
    return out.astype(bf16)


def _attn_kernel(q_ref, k_ref, v_ref, cq_ref, ck_ref, o_ref, k_aug, *, tq):
    hd = pl.program_id(1)
    qi = pl.program_id(2)

    def head_column(c_ref):
        lane = lax.broadcasted_iota(i32, c_ref.shape, 1)
        return jnp.sum(jnp.where(lane == hd, c_ref[...], 0.0), axis=1, keepdims=True)

    @pl.when(qi == 0)
    def _():
        k_aug[:, :HEAD_DIM] = k_ref[...]
        k_aug[:, HEAD_DIM:] = _bias_lanes(-head_column(ck_ref), ones_first=True)

    q = jnp.concatenate([q_ref[...], _bias_lanes(head_column(cq_ref), ones_first=False)], axis=1)
    def rows_of(j):
        return pl.ds(pl.multiple_of(j * tq, tq), tq)

    def scores(j):
        return _dot_nt(q, k_aug[rows_of(j), :])

    def update(s, vj, state):
        m, l, acc = state
        m_new = jnp.maximum(m, jnp.max(s, axis=1, keepdims=True))
        alpha = jnp.exp(m - m_new)
        p = jnp.exp(s - m_new)
        l = alpha * l + jnp.sum(p, axis=1, keepdims=True)
        acc = alpha * acc + jnp.dot(p.astype(bf16), vj, preferred_element_type=f32)
        return m_new, l, acc

    def body(j, state):
        return update(scores(j), v_ref[rows_of(j), :], state)

    init = (jnp.full((tq, 1), -jnp.inf, f32), jnp.zeros((tq, 1), f32), jnp.zeros((tq, HEAD_DIM), f32))
    state = lax.fori_loop(0, qi, body, init)
    r = lax.broadcasted_iota(i32, (tq, tq), 0)
    c = lax.broadcasted_iota(i32, (tq, tq), 1)
    _, l, acc = update(jnp.where(c <= r, scores(qi), -jnp.inf), v_ref[rows_of(qi), :], state)
    o_ref[...] = acc / l


def _attention(qkv, c, batch, seq, nheads):
    t = qkv.shape[0]
    tq = 512
    nq = seq // tq
    return pl.pallas_call(
        functools.partial(_attn_kernel, tq=tq),
        grid=(batch, nheads, nq),
        in_specs=[
            pl.BlockSpec((tq, HEAD_DIM), lambda b, h, qi: (b * nq + qi, h)),
            pl.BlockSpec((seq, HEAD_DIM), lambda b, h, qi: (b, nheads + h)),
            pl.BlockSpec((seq, HEAD_DIM), lambda b, h, qi: (b, 2 * nheads + h)),
            pl.BlockSpec((tq, LANES), lambda b, h, qi: (b * nq + qi, 0)),
            pl.BlockSpec((seq, LANES), lambda b, h, qi: (b, 0)),
        ],
        out_specs=pl.BlockSpec((tq, HEAD_DIM), lambda b, h, qi: (b * nq + qi, h)),
        out_shape=jax.ShapeDtypeStruct((t, nheads * HEAD_DIM), f32),
        scratch_shapes=[pltpu.VMEM((seq, 2 * HEAD_DIM), bf16)],
        compiler_params=_cparams(("arbitrary", "arbitrary", "arbitrary")),
        name="attention",
    )(qkv, qkv, qkv, c, c)


def _outproj_kernel(att_ref, conv_ref, ga_ref, gc_ref, w_ref, x_ref, o_ref, mixed):
    j = pl.program_id(1)
    wa = att_ref.shape[1]

    @pl.when(j == 0)
    def _():
        mixed[:, :wa] = _rms(att_ref[...], ga_ref[...]).astype(bf16)
        mixed[:, wa:] = _rms(conv_ref[...], gc_ref[...]).astype(bf16)

    o_ref[...] = x_ref[...] + jnp.dot(mixed[...], w_ref[...], preferred_element_type=f32)


def _outproj(att, conv, g_a, g_c, w_o, x2):
    t, d = x2.shape
    wa, wc = att.shape[1], conv.shape[1]
    tm, tn = 512, 1024
    return pl.pallas_call(
        _outproj_kernel,
        grid=(t // tm, d // tn),
        in_specs=[
            pl.BlockSpec((tm, wa), lambda i, j: (i, 0)),
            pl.BlockSpec((tm, wc), lambda i, j: (i, 0)),
            pl.BlockSpec((1, wa), lambda i, j: (0, 0)),
            pl.BlockSpec((1, wc), lambda i, j: (0, 0)),
            pl.BlockSpec((wa + wc, tn), lambda i, j: (0, j)),
            pl.BlockSpec((tm, tn), lambda i, j: (i, j)),
        ],
        out_specs=pl.BlockSpec((tm, tn), lambda i, j: (i, j)),
        out_shape=jax.ShapeDtypeStruct((t, d), f32),
        scratch_shapes=[pltpu.VMEM((tm, wa + wc), bf16)],
        compiler_params=_cparams(("arbitrary", "arbitrary")),
        name="outproj",
    )(att, conv, g_a.reshape(1, wa), g_c.reshape(1, wc), w_o, x2)


def _router_kernel(x_ref, g_ref, wrt_ref, br_ref, lg_ref):
    h_hi, h_lo, _ = _split3(_rms(x_ref[...], g_ref[...]))
    w_hi, w_lo, _ = _split3(wrt_ref[...])
    lg_ref[...] = _dot_nt(w_hi, h_hi) + (_dot_nt(w_lo, h_hi) + _dot_nt(w_hi, h_lo)) + br_ref[...]


def _router(x1, g_ffn, w_router, b_router):
    t, d = x1.shape
    ne = w_router.shape[1]
    tm = 512
    return pl.pallas_call(
        _router_kernel,
        grid=(t // tm,),
        in_specs=[
            pl.BlockSpec((tm, d), lambda i: (i, 0)),
            pl.BlockSpec((1, d), lambda i: (0, 0)),
            pl.BlockSpec((ne, d), lambda i: (0, 0)),
            pl.BlockSpec((ne, 1), lambda i: (0, 0)),
        ],
        out_specs=pl.BlockSpec((ne, tm), lambda i: (0, i)),
        out_shape=jax.ShapeDtypeStruct((ne, t), f32),
        compiler_params=_cparams(("arbitrary",)),
        name="router",
    )(x1, g_ffn.reshape(1, d), w_router.T, b_router.reshape(ne, 1))


def _cumsum_sublanes(x):
    n = x.shape[0]
    row = lax.broadcasted_iota(i32, x.shape, 0)
    d = 1
    while d < n:
        x = x + jnp.where(row >= d, pltpu.roll(x, d, 0), 0.0)
        d *= 2
    return x


def _stack_rows(rows, n):
    width = rows[0].shape[1]
    sub = lax.broadcasted_iota(i32, (n, width), 0)
    out = jnp.zeros((n, width), rows[0].dtype)
    for k, r in enumerate(rows):
        out = jnp.where(sub == k, r, out)
    return out


def _route_kernel(lg_ref, pos_ref, gcol_ref, meta_ref, esel, rsel, *, chunk):
    ne, t = lg_ref.shape
    e_iota = lax.broadcasted_iota(i32, (ne, chunk), 0).astype(f32)
    a = lax.broadcasted_iota(i32, (chunk, chunk), 0)
    b = lax.broadcasted_iota(i32, (chunk, chunk), 1)
    before = (a < b).astype(bf16)
    counts = jnp.zeros((ne, 1), f32)
    for c in range(t // chunk):
        sl = slice(c * chunk, (c + 1) * chunk)
        vals = lg_ref[:, sl]
        tops, idxs, hots = [], [], []
        for _ in range(TOP_K):
            m = jnp.max(vals, axis=0, keepdims=True)
            idx = jnp.min(jnp.where(vals == m, e_iota, ne), axis=0, keepdims=True)
            hot = e_iota == idx
            vals = jnp.where(hot, -jnp.inf, vals)
            tops.append(m)
            idxs.append(idx)
            hots.append(hot)
        ex = [jnp.exp(v - tops[0]) for v in tops]
        den = ex[0] + ex[1] + ex[2] + ex[3]
        gates = [e / den for e in ex]
        member = sum(h.astype(f32) for h in hots)
        rank = jnp.dot(member.astype(bf16), before, preferred_element_type=f32) + counts
        counts = counts + jnp.sum(member, axis=1, keepdims=True)
        for k in range(TOP_K):
            rsel[k:k + 1, sl] = jnp.sum(jnp.where(hots[k], rank, 0.0), axis=0, keepdims=True)
            esel[k:k + 1, sl] = idxs[k]
        g8 = _stack_rows(gates, SUBLANES)
        gpad = jnp.concatenate([g8, jnp.zeros((LANES - SUBLANES, chunk), f32)], axis=0)
        gcol_ref[sl, :] = gpad.T

    cnt = jnp.broadcast_to(counts, (ne, LANES))
    pcnt = jnp.ceil(cnt * (1.0 / ROW_PAD)) * ROW_PAD
    pend = _cumsum_sublanes(pcnt)
    pstart = pend - pcnt
    npass = jnp.floor((pcnt + (EXPERT_CAP - ROW_PAD)) / EXPERT_CAP + 0.5 * ROW_PAD / EXPERT_CAP)
    iend = _cumsum_sublanes(npass)
    istart = iend - npass
    n_items = jnp.max(iend, axis=0, keepdims=True)

    pstart_col = pstart[:, 0:1]
    for c in range(t // chunk):
        sl = slice(c * chunk, (c + 1) * chunk)
        for k in range(TOP_K):
            hot = e_iota == esel[k:k + 1, sl]
            ps = jnp.sum(jnp.where(hot, pstart_col, 0.0), axis=0, keepdims=True)
            pos_ref[k:k + 1, sl] = (rsel[k:k + 1, sl] + ps).astype(i32)

    e_sub = lax.broadcasted_iota(i32, (ne, LANES), 0)
    lane = lax.broadcasted_iota(i32, (ne, LANES), 1)
    lane_f = lane.astype(f32)
    item = lane_f[0:1, :]
    e_of = jnp.minimum(jnp.sum((iend <= lane_f).astype(f32), axis=0, keepdims=True), ne - 1.0)
    hot = e_sub.astype(f32) == e_of
    pick = lambda v: jnp.sum(jnp.where(hot, v, 0.0), axis=0, keepdims=True)
    p_i = item - pick(istart)
    row0 = pick(pstart) + p_i * EXPERT_CAP
    nrows = jnp.clip(pick(pcnt) - p_i * EXPERT_CAP, 0.0, float(EXPERT_CAP))
    nsub = nrows * (1.0 / ROW_PAD)
    diag = e_sub == lane
    zrow = jnp.sum(jnp.where(diag, pend - ROW_PAD, 0.0), axis=0, keepdims=True)
    zval = jnp.sum(jnp.where(diag, (cnt > 0).astype(f32), 0.0), axis=0, keepdims=True)
    total = jnp.max(pend, axis=0, keepdims=True)
    rows = [e_of, row0, nsub, n_items, zrow, zval, total]
    meta_ref[...] = _stack_rows([r.astype(i32) for r in rows], SUBLANES)


def _route(logits_t):
    ne, t = logits_t.shape
    chunk = 512
    return pl.pallas_call(
        functools.partial(_route_kernel, chunk=chunk),
        out_shape=[
            jax.ShapeDtypeStruct((TOP_K, t), i32),
            jax.ShapeDtypeStruct((t, LANES), f32),
            jax.ShapeDtypeStruct((SUBLANES, LANES), i32),
        ],
        scratch_shapes=[pltpu.VMEM((TOP_K, t), f32), pltpu.VMEM((TOP_K, t), f32)],
        compiler_params=pltpu.CompilerParams(vmem_limit_bytes=VMEM_LIMIT_BYTES),
        name="route",
    )(logits_t)


_M_EXPERT, _M_ROW0, _M_NSUB, _M_NITEMS, _M_ZROW, _M_ZVALID, _M_TOTAL = range(7)


def _zero_tail(meta, zbuf, dst_ref, sem):
    total = pl.multiple_of(meta[_M_TOTAL, 0], ROW_PAD)
    n = (dst_ref.shape[0] - total) // ROW_PAD

    def copy(r):
        return pltpu.make_async_copy(zbuf, dst_ref.at[pl.ds(total + r * ROW_PAD, ROW_PAD), :], sem)

    def start(r, c):
        copy(r).start()
        return c

    def wait(r, c):
        copy(r).wait()
        return c

    lax.fori_loop(0, n, start, 0)
    lax.fori_loop(0, n, wait, 0)


def _dispatch_kernel(meta, pos_ref, x_ref, g_ref, xs_ref, h_scr, zbuf, sem, zsem, *, tmd, ne):
    i = pl.program_id(0)
    h_scr[...] = _rms(x_ref[...], g_ref[...])

    def zero_copy(e):
        start = pl.multiple_of(meta[_M_ZROW, e], ROW_PAD)
        return pltpu.make_async_copy(zbuf, xs_ref.at[pl.ds(start, ROW_PAD), :], zsem)

    @pl.when(i == 0)
    def _():
        zbuf[...] = jnp.zeros_like(zbuf)

        def zstart(e, c):
            @pl.when(meta[_M_ZVALID, e] > 0)
            def _():
                zero_copy(e).start()
            return c

        def zwait(e, c):
            @pl.when(meta[_M_ZVALID, e] > 0)
            def _():
                zero_copy(e).wait()
            return c

        lax.fori_loop(0, ne, zstart, 0)
        lax.fori_loop(0, ne, zwait, 0)
        _zero_tail(meta, zbuf, xs_ref, zsem)

    def row_copy(r, k):
        return pltpu.make_async_copy(h_scr.at[pl.ds(r, 1), :],
                                     xs_ref.at[pl.ds(pos_ref[k, r], 1), :], sem)

    def issue(r, c):
        for k in range(TOP_K):
            row_copy(r, k).start()
        return c

    lax.fori_loop(0, tmd, issue, 0)
    for _ in range(TOP_K):
        pltpu.make_async_copy(h_scr, xs_ref.at[pl.ds(0, tmd), :], sem).wait()


def _dispatch(meta, pos_t, x1, g_ffn, n_rows, ne):
    t, d = x1.shape
    tmd = 256
    return pl.pallas_call(
        functools.partial(_dispatch_kernel, tmd=tmd, ne=ne),
        grid_spec=pltpu.PrefetchScalarGridSpec(
            num_scalar_prefetch=1,
            grid=(t // tmd,),
            in_specs=[
                pl.BlockSpec((TOP_K, tmd), lambda i, m: (0, i), memory_space=pltpu.SMEM),
                pl.BlockSpec((tmd, d), lambda i, m: (i, 0)),
                pl.BlockSpec((1, d), lambda i, m: (0, 0)),
            ],
            out_specs=pl.BlockSpec(memory_space=pl.ANY),
            scratch_shapes=[pltpu.VMEM((tmd, d), f32), pltpu.VMEM((ROW_PAD, d), f32),
                            pltpu.SemaphoreType.DMA, pltpu.SemaphoreType.DMA],
        ),
        out_shape=jax.ShapeDtypeStruct((n_rows, d), f32),
        compiler_params=_cparams(("arbitrary",)),
        name="dispatch",
    )(meta, pos_t, x1, g_ffn.reshape(1, d))


def _expert_kernel(meta, xs_ref, wgu_hbm, wd_hbm, bgu_ref, bd_ref, ys_ref,
                   x_scr, h_scr, wbuf, stage, y_scr, w_sem, ld_sem, st_sem,
                   *, nf, nd, tf, td):
    it = pl.program_id(0)
    n_items = meta[_M_NITEMS, 0]
    f_dim = wd_hbm.shape[1]

    def up_copies(e, f, slot):
        cols = pl.ds(pl.multiple_of(f * tf, tf), tf)
        cols_up = pl.ds(pl.multiple_of(f_dim + f * tf, tf), tf)
        return (pltpu.make_async_copy(wgu_hbm.at[e, :, cols], wbuf.at[slot, :, pl.ds(0, tf)], w_sem.at[2 * slot]),
                pltpu.make_async_copy(wgu_hbm.at[e, :, cols_up], wbuf.at[slot, :, pl.ds(tf, tf)],
                                      w_sem.at[2 * slot + 1]))

    def down_rows(dcol):
        return pl.ds(pl.multiple_of((dcol % 2) * f_dim, f_dim), f_dim)

    def down_copy(e, dcol):
        cols = pl.ds(pl.multiple_of(dcol * td, td), td)
        return pltpu.make_async_copy(wd_hbm.at[e, :, cols], wbuf.at[(dcol // 2) % 2, down_rows(dcol), :],
                                     w_sem.at[dcol % 4])

    def load_copy(first_row, r, slot):
        return pltpu.make_async_copy(xs_ref.at[pl.ds(first_row + r * ROW_PAD, ROW_PAD), :],
                                     stage.at[slot], ld_sem.at[slot])

    def rows_of(r):
        return pl.ds(pl.multiple_of(r * ROW_PAD, ROW_PAD), ROW_PAD)

    @pl.when(it < n_items)
    def _():
        e = meta[_M_EXPERT, it]
        row0 = pl.multiple_of(meta[_M_ROW0, it], ROW_PAD)
        nsub = meta[_M_NSUB, it]
        has_next = it + 1 < n_items
        nxt = jnp.minimum(it + 1, n_items - 1)
        next_row0 = pl.multiple_of(meta[_M_ROW0, nxt], ROW_PAD)
        next_nsub = meta[_M_NSUB, nxt]
        xcur = it % 2

        @pl.when(it == 0)
        def _():
            for c in up_copies(e, 0, 0):
                c.start()
            stage[0] = jnp.zeros(stage.shape[1:], f32)
            _zero_tail(meta, stage.at[0], ys_ref, ld_sem.at[0])
            load_copy(row0, 0, 0).start()

            def body(r, c):
                slot = r % 2
                load_copy(row0, r, slot).wait()

                @pl.when(r + 1 < nsub)
                def _():
                    load_copy(row0, r + 1, 1 - slot).start()

                x_scr[0, rows_of(r), :] = stage[slot].astype(bf16)
                return c

            lax.fori_loop(0, nsub, body, 0)

        def for_row_tiles(fn):
            per = COMPUTE_ROWS // ROW_PAD
            n_full = nsub // per

            def pair(r, c):
                first = pl.multiple_of(r * (2 * COMPUTE_ROWS), COMPUTE_ROWS)
                fn(first, COMPUTE_ROWS)
                fn(first + COMPUTE_ROWS, COMPUTE_ROWS)
                return c

            lax.fori_loop(0, n_full // 2, pair, 0)

            @pl.when(n_full % 2 == 1)
            def _():
                fn(pl.multiple_of((n_full - 1) * COMPUTE_ROWS, COMPUTE_ROWS), COMPUTE_ROWS)

            for tail in range(1, per):
                @pl.when(nsub % per == tail)
                def _():
                    fn(pl.multiple_of(n_full * COMPUTE_ROWS, COMPUTE_ROWS), tail * ROW_PAD)

        def up_chunk(f, carry):
            slot = f % 2
            for c in up_copies(e, f, slot):
                c.wait()

            @pl.when(f + 1 < nf)
            def _():
                for c in up_copies(e, f + 1, 1 - slot):
                    c.start()

            @pl.when(f + 1 == nf)
            def _():
                down_copy(e, 0).start()
                down_copy(e, 1).start()

            bg = bgu_ref[f]
            bu = bgu_ref[nf + f]

            for j in range(stage.shape[0]):
                @pl.when(has_next & (f + j * nf < next_nsub))
                def _():
                    load_copy(next_row0, f + j * nf, j).start()

            def up_tile(first, n):
                rows = pl.ds(first, n)
                gu = jnp.dot(x_scr[xcur, rows, :], wbuf[slot].astype(bf16), preferred_element_type=f32)
                g = jnp.minimum(gu[:, :tf] + bg, SWIGLU_LIMIT)
                u = jnp.clip(gu[:, tf:] + bu, -SWIGLU_LIMIT, SWIGLU_LIMIT)
                act = (u + 1.0) * (g * jax.nn.sigmoid(SWIGLU_ALPHA * g))
                h_scr[f, rows, :] = act.astype(bf16)

            for_row_tiles(up_tile)

            for j in range(stage.shape[0]):
                @pl.when(has_next & (f + j * nf < next_nsub))
                def _():
                    load_copy(next_row0, f + j * nf, j).wait()
                    x_scr[1 - xcur, rows_of(f + j * nf), :] = stage[j].astype(bf16)

            return carry

        lax.fori_loop(0, nf, up_chunk, 0)

        def store_copy(first, dcol, sl):
            dst = ys_ref.at[pl.ds(row0 + first, ROW_PAD), pl.ds(pl.multiple_of(dcol * td, td), td)]
            return pltpu.make_async_copy(y_scr.at[sl, pl.ds(first, ROW_PAD), :], dst, st_sem.at[sl])

        def wait_stores(sl, count):
            def w(r, c):
                store_copy(0, 0, sl).wait()
                return c
            lax.fori_loop(0, count, w, 0)

        def down_chunk(dcol, carry):
            slot = (dcol // 2) % 2
            yslot = dcol % 2
            down_copy(e, dcol).wait()

            @pl.when(dcol == 0)
            def _():
                down_copy(e, 2).start()
                down_copy(e, 3).start()

            @pl.when((dcol >= 1) & (dcol + 3 < nd))
            def _():
                down_copy(e, dcol + 3).start()

            @pl.when((dcol + 2 == nd) & has_next)
            def _():
                for c in up_copies(meta[_M_EXPERT, nxt], 0, 0):
                    c.start()

            bd = bd_ref[dcol]

            @pl.when(dcol >= 2)
            def _():
                wait_stores(yslot, nsub)

            @pl.when((dcol < 2) & (it > 0))
            def _():
                wait_stores(yslot, meta[_M_NSUB, jnp.maximum(it - 1, 0)])

            def down_tile(first, n):
                rows = pl.ds(first, n)
                hidden = jnp.concatenate([h_scr[cf, rows, :] for cf in range(nf)], axis=1)
                y_scr[yslot, rows, :] = bd + jnp.dot(hidden, wbuf[slot, down_rows(dcol), :].astype(bf16),
                                                     preferred_element_type=f32)
                for part in range(n // ROW_PAD):
                    store_copy(first + part * ROW_PAD, dcol, yslot).start()

            for_row_tiles(down_tile)

            return carry

        lax.fori_loop(0, nd, down_chunk, 0)

        @pl.when(jnp.logical_not(has_next))
        def _():
            wait_stores(0, nsub)
            wait_stores(1, nsub)


def _experts(meta, xs, w_gate_up, b_gate_up, w_down, b_down):
    n_rows, d = xs.shape
    ne, _, f2 = w_gate_up.shape
    f = f2 // 2
    tf, td = 256, 512
    nf, nd = f // tf, d // td
    max_items = (n_rows + ne * (EXPERT_CAP - ROW_PAD)) // EXPERT_CAP
    stage_slots = max(2, pl.cdiv(EXPERT_CAP // ROW_PAD, nf))
    assert max_items <= MAX_ITEMS and td == 2 * tf and f <= d
    assert nf % 2 == 0 and nd % 4 == 0 and 2 * f == d

    def expert(i, m):
        return m[_M_EXPERT, jnp.minimum(i, m[_M_NITEMS, 0] - 1)]

    return pl.pallas_call(
        functools.partial(_expert_kernel, nf=nf, nd=nd, tf=tf, td=td),
        grid_spec=pltpu.PrefetchScalarGridSpec(
            num_scalar_prefetch=1,
            grid=(max_items,),
            in_specs=[
                pl.BlockSpec(memory_space=pl.ANY),
                pl.BlockSpec(memory_space=pl.ANY),
                pl.BlockSpec(memory_space=pl.ANY),
                pl.BlockSpec((None, 2 * nf, 1, tf), lambda i, m: (expert(i, m), 0, 0, 0)),
                pl.BlockSpec((None, nd, 1, td), lambda i, m: (expert(i, m), 0, 0, 0)),
            ],
            out_specs=pl.BlockSpec(memory_space=pl.ANY),
            scratch_shapes=[
                pltpu.VMEM((2, EXPERT_CAP, d), bf16),
                pltpu.VMEM((nf, EXPERT_CAP, tf), bf16),
                pltpu.VMEM((2, d, 2 * tf), f32),
                pltpu.VMEM((stage_slots, ROW_PAD, d), f32),
                pltpu.VMEM((2, EXPERT_CAP, td), f32),
                pltpu.SemaphoreType.DMA((4,)),
                pltpu.SemaphoreType.DMA((stage_slots,)),
                pltpu.SemaphoreType.DMA((2,)),
            ],
        ),
        out_shape=jax.ShapeDtypeStruct((n_rows, d), f32),
        compiler_params=_cparams(("arbitrary",)),
        name="experts",
    )(meta, xs, w_gate_up, w_down, b_gate_up.reshape(ne, 2 * nf, 1, tf), b_down.reshape(ne, nd, 1, td))


def _combine_kernel(pos_cur, pos_nxt, gcol_ref, x_ref, ys_ref, o_ref, gbuf, sem, *, tmc, n_tiles):
    i = pl.program_id(0)
    slot = i % 2

    def row_copy(pos_ref, sl, r, k):
        return pltpu.make_async_copy(ys_ref.at[pl.ds(pos_ref[k, r], 1), :],
                                     gbuf.at[sl, k, pl.ds(r, 1), :], sem.at[sl])

    def issue(pos_ref, sl):
        def body(r, c):
            for k in range(TOP_K):
                row_copy(pos_ref, sl, r, k).start()
            return c
        lax.fori_loop(0, tmc, body, 0)

    @pl.when(i == 0)
    def _():
        issue(pos_cur, 0)

    @pl.when(i + 1 < n_tiles)
    def _():
        issue(pos_nxt, 1 - slot)

    for k in range(TOP_K):
        pltpu.make_async_copy(ys_ref.at[pl.ds(0, tmc), :], gbuf.at[slot, k], sem.at[slot]).wait()
    acc = x_ref[...]
    for k in range(TOP_K):
        acc = acc + gcol_ref[:, k:k + 1] * gbuf[slot, k]
    o_ref[...] = acc


def _combine(pos_t, gcol, x1, ys):
    t, d = x1.shape
    tmc = 128
    n_tiles = t // tmc
    return pl.pallas_call(
        functools.partial(_combine_kernel, tmc=tmc, n_tiles=n_tiles),
        grid=(n_tiles,),
        in_specs=[
            pl.BlockSpec((TOP_K, tmc), lambda i: (0, i), memory_space=pltpu.SMEM),
            pl.BlockSpec((TOP_K, tmc), lambda i: (0, jnp.minimum(i + 1, n_tiles - 1)), memory_space=pltpu.SMEM),
            pl.BlockSpec((tmc, LANES), lambda i: (i, 0)),
            pl.BlockSpec((tmc, d), lambda i: (i, 0)),
            pl.BlockSpec(memory_space=pl.ANY),
        ],
        out_specs=pl.BlockSpec((tmc, d), lambda i: (i, 0)),
        out_shape=jax.ShapeDtypeStruct((t, d), f32),
        scratch_shapes=[pltpu.VMEM((2, TOP_K, tmc, d), f32), pltpu.SemaphoreType.DMA((2,))],
        compiler_params=_cparams(("arbitrary",)),
        name="combine",
    )(pos_t, pos_t, gcol, x1, ys)


def _layer(x2, batch, seq, g_mix, w_in, b_forget, g_q, g_k, w_conv, g_attn_out, g_conv_out, w_out,
           g_ffn, w_router, b_router, w_gate_up, b_gate_up, w_down, b_down):
    t, d = x2.shape
    nheads = b_forget.shape[0]
    att_width = nheads * HEAD_DIM
    ne = w_router.shape[1]
    c0 = 3 * att_width
    w_t = w_in.T
    conv_width = (w_t.shape[0] - c0 - nheads) // 3

    h, c = _prenorm(x2, g_mix, w_t, c0, b_forget, seq)
    qkv = _qkv(h, w_t, g_q, g_k, att_width)
    conv = _conv(h, w_t, c0 + nheads, conv_width, w_conv, seq)
    att = _attention(qkv, c, batch, seq, nheads)
    x1 = _outproj(att, conv, g_attn_out, g_conv_out, w_out.astype(bf16), x2)

    pos_t, gcol, meta = _route(_router(x1, g_ffn, w_router, b_router))
    n_rows = t * TOP_K + ne * ROW_PAD
    xs = _dispatch(meta, pos_t, x1, g_ffn, n_rows, ne)
    ys = _experts(meta, xs, w_gate_up, b_gate_up, w_down, b_down)
    return _combine(pos_t, gcol, x1, ys)


def kernel(x, g_mix, w_in, b_forget, g_q, g_k, w_conv, g_attn_out, g_conv_out, w_out, g_ffn, w_router,
           b_router, w_gate_up, b_gate_up, w_down, b_down):
    b, s, d = x.shape
    x2 = x.reshape(b * s, d)
    for l in range(g_mix.shape[0]):
        x2 = _layer(x2, b, s, g_mix[l], w_in[l], b_forget[l], g_q[l], g_k[l], w_conv[l], g_attn_out[l],
                    g_conv_out[l], w_out[l], g_ffn[l], w_router[l], b_router[l], w_gate_up[l],
                    b_gate_up[l], w_down[l], b_down[l])
    return x2.reshape(b, s, d)
```

```python
import functools

import jax
import jax.numpy as jnp
import numpy as np
from jax import lax
from jax.experimental import pallas as pl
from jax.experimental.pallas import tpu as pltpu

f32 = jnp.float32
bf16 = jnp.bfloat16
i32 = jnp.int32

HEAD_DIM = 128
LANES = 128
SUBLANES = 8
TOP_K = 4
RMS_EPS = 1e-6
SWIGLU_ALPHA = 1.702
SWIGLU_LIMIT = 7.0
VMEM_LIMIT_BYTES = 56 * 1024 * 1024

ROW_PAD = 128
EXPERT_CAP = 1152
COMPUTE_ROWS = 512
MAX_ITEMS = 128


def _cparams(sem):
    return pltpu.CompilerParams(dimension_semantics=sem, vmem_limit_bytes=VMEM_LIMIT_BYTES)


def _log_sigmoid(z):
    return jnp.minimum(z, 0.0) - jnp.log1p(jnp.exp(-jnp.abs(z)))


def _split3(a):
    hi = a.astype(bf16)
    r1 = a - hi.astype(f32)
    mid = r1.astype(bf16)
    lo = (r1 - mid.astype(f32)).astype(bf16)
    return hi, mid, lo


def _dot_nt(a, b):
    return lax.dot_general(a, b, (((1,), (1,)), ((), ())), preferred_element_type=f32)


def _rms(y, g):
    ms = jnp.mean(y * y, axis=-1, keepdims=True)
    return y * lax.rsqrt(ms + RMS_EPS) * g


def _prenorm_kernel(x_ref, g_ref, wf_ref, bf_ref, h_ref, c_ref, carry, *, tiles_per_seq):
    i = pl.program_id(0)
    hb = _rms(x_ref[...], g_ref[...]).astype(bf16)
    h_ref[...] = hb
    logf = _log_sigmoid(_dot_nt(hb, wf_ref[...].astype(bf16)) + bf_ref[...])
    tm = hb.shape[0]
    a = lax.broadcasted_iota(i32, (tm, tm), 0)
    b = lax.broadcasted_iota(i32, (tm, tm), 1)
    lower = (b <= a).astype(bf16)
    c = sum(jnp.dot(lower, p, preferred_element_type=f32) for p in _split3(logf))

    @pl.when(i % tiles_per_seq == 0)
    def _():
        carry[...] = jnp.zeros_like(carry)

    c = c + carry[...]
    c_ref[...] = c
    carry[...] = c[tm - 1:tm, :]


def _prenorm(x2, g_mix, w_t, f_row0, b_f, seq):
    t, d = x2.shape
    nh = b_f.shape[0]
    tm = 512
    assert f_row0 % LANES == 0 and nh <= LANES
    bfr = jnp.pad(b_f, (0, LANES - nh)).reshape(1, LANES)
    return pl.pallas_call(
        functools.partial(_prenorm_kernel, tiles_per_seq=seq // tm),
        grid=(t // tm,),
        in_specs=[
            pl.BlockSpec((tm, d), lambda i: (i, 0)),
            pl.BlockSpec((1, d), lambda i: (0, 0)),
            pl.BlockSpec((LANES, d), lambda i: (f_row0 // LANES, 0)),
            pl.BlockSpec((1, LANES), lambda i: (0, 0)),
        ],
        out_specs=[
            pl.BlockSpec((tm, d), lambda i: (i, 0)),
            pl.BlockSpec((tm, LANES), lambda i: (i, 0)),
        ],
        out_shape=[
            jax.ShapeDtypeStruct((t, d), bf16),
            jax.ShapeDtypeStruct((t, LANES), f32),
        ],
        scratch_shapes=[pltpu.VMEM((1, LANES), f32)],
        compiler_params=_cparams(("arbitrary",)),
        name="prenorm",
    )(x2, g_mix.reshape(1, d), w_t, bfr)


def _qkv_kernel(h_ref, w_ref, g_ref, post_ref, o_ref, *, n_norm_blocks):
    normed = pl.program_id(0) < n_norm_blocks
    tm, tn = o_ref.shape
    for half in range(2):
        rows = slice(half * (tm // 2), (half + 1) * (tm // 2))
        y = _dot_nt(h_ref[rows, :], w_ref[...].astype(bf16))
        for hh in range(tn // HEAD_DIM):
            sl = slice(hh * HEAD_DIM, (hh + 1) * HEAD_DIM)
            yh = y[:, sl]
            inv = lax.rsqrt(jnp.mean(yh * yh, axis=-1, keepdims=True) + RMS_EPS)
            o_ref[rows, sl] = (yh * jnp.where(normed, inv, 1.0) * g_ref[:, sl] * post_ref[:, sl]).astype(bf16)


def _qkv(h, w_t, g_q, g_k, att_width):
    t, d = h.shape
    n = 3 * att_width
    tm, tn = 1024, 512
    nheads = att_width // HEAD_DIM
    ones = jnp.ones((att_width,), f32)
    gain = jnp.concatenate([jnp.tile(g_q, nheads), jnp.tile(g_k, nheads), ones]).reshape(1, n)
    post = jnp.concatenate([ones * (1.0 / np.sqrt(HEAD_DIM)), ones, ones]).reshape(1, n)
    return pl.pallas_call(
        functools.partial(_qkv_kernel, n_norm_blocks=2 * att_width // tn),
        grid=(n // tn, t // tm),
        in_specs=[
            pl.BlockSpec((tm, d), lambda j, i: (i, 0)),
            pl.BlockSpec((tn, d), lambda j, i: (j, 0)),
            pl.BlockSpec((1, tn), lambda j, i: (0, j)),
            pl.BlockSpec((1, tn), lambda j, i: (0, j)),
        ],
        out_specs=pl.BlockSpec((tm, tn), lambda j, i: (i, j)),
        out_shape=jax.ShapeDtypeStruct((t, n), bf16),
        compiler_params=_cparams(("arbitrary", "arbitrary")),
        name="qkv",
    )(h, w_t, gain, post)


def _conv_kernel(h_ref, wb_ref, wc_ref, wx_ref, wk_ref, o_ref, carry, *, tiles_per_seq):
    i = pl.program_id(1)
    tm, tn = o_ref.shape
    hm = tm // 2
    prev = jnp.where(i % tiles_per_seq == 0, 0.0, carry[...])
    p2 = prev[SUBLANES - 2:SUBLANES - 1, :]
    p1 = prev[SUBLANES - 1:SUBLANES, :]
    row = lax.broadcasted_iota(i32, (hm, tn), 0)
    for half in range(2):
        rows = slice(half * hm, (half + 1) * hm)
        h = h_ref[rows, :]
        gb = _dot_nt(h, wb_ref[...].astype(bf16))
        gc = _dot_nt(h, wc_ref[...].astype(bf16))
        xc = _dot_nt(h, wx_ref[...].astype(bf16))
        u = gc * xc
        u1 = jnp.where(row == 0, p1, pltpu.roll(u, 1, 0))
        u2 = jnp.where(row == 0, p2, jnp.where(row == 1, p1, pltpu.roll(u, 2, 0)))
        o_ref[rows, :] = gb * (wk_ref[0:1, :] * u2 + wk_ref[1:2, :] * u1 + wk_ref[2:3, :] * u)
        p2 = u[hm - 2:hm - 1, :]
        p1 = u[hm - 1:hm, :]
    carry[...] = u[hm - SUBLANES:hm, :]


def _conv(h, w_t, row0, width, w_conv, seq):
    t, d = h.shape
    tm, tn = 1024, 256
    assert row0 % SUBLANES == 0 and width % tn == 0

    def w_spec(part):
        return pl.BlockSpec((pl.Element(tn), pl.Element(d)),
                            lambda j, i: (pl.multiple_of(row0 + part * width + j * tn, SUBLANES), 0))

    return pl.pallas_call(
        functools.partial(_conv_kernel, tiles_per_seq=seq // tm),
        grid=(width // tn, t // tm),
        in_specs=[
            pl.BlockSpec((tm, d), lambda j, i: (i, 0)),
            w_spec(0), w_spec(1), w_spec(2),
            pl.BlockSpec((3, tn), lambda j, i: (0, j)),
        ],
        out_specs=pl.BlockSpec((tm, tn), lambda j, i: (i, j)),
        out_shape=jax.ShapeDtypeStruct((t, width), f32),
        scratch_shapes=[pltpu.VMEM((SUBLANES, tn), f32)],
        compiler_params=_cparams(("arbitrary", "arbitrary")),
        name="conv",
    )(h, w_t, w_t, w_t, w_conv)


def _bias_lanes(c, ones_first):
    n = c.shape[0]
    lane = lax.broadcasted_iota(i32, (n, LANES), 1)
    out = jnp.zeros((n, LANES), f32)
    base_c, base_1 = (3, 0) if ones_first else (0, 3)
    for k, piece in enumerate(_split3(c)):
        out = jnp.where(lane == base_c + k, piece.astype(f32), out)
        out = jnp.where(lane == base_1 + k, 1.0, out)
    return out.astype(bf16)


def _attn_kernel(q_ref, k_ref, v_ref, cq_ref, ck_ref, o_ref, k_aug, *, tq):
    hd = pl.program_id(1)
    qi = pl.program_id(2)

    def head_column(c_ref):
        lane = lax.broadcasted_iota(i32, c_ref.shape, 1)
        return jnp.sum(jnp.where(lane == hd, c_ref[...], 0.0), axis=1, keepdims=True)

    @pl.when(qi == 0)
    def _():
        k_aug[:, :HEAD_DIM] = k_ref[...]
        k_aug[:, HEAD_DIM:] = _bias_lanes(-head_column(ck_ref), ones_first=True)

    q = jnp.concatenate([q_ref[...], _bias_lanes(head_column(cq_ref), ones_first=False)], axis=1)
    def rows_of(j):
        return pl.ds(pl.multiple_of(j * tq, tq), tq)

    def scores(j):
        return _dot_nt(q, k_aug[rows_of(j), :])

    def update(s, vj, state):
        m, l, acc = state
        m_new = jnp.maximum(m, jnp.max(s, axis=1, keepdims=True))
        alpha = jnp.exp(m - m_new)
        p = jnp.exp(s - m_new)
        l = alpha * l + jnp.sum(p, axis=1, keepdims=True)
        acc = alpha * acc + jnp.dot(p.astype(bf16), vj, preferred_element_type=f32)
        return m_new, l, acc

    def body(j, state):
        return update(scores(j), v_ref[rows_of(j), :], state)

    init = (jnp.full((tq, 1), -jnp.inf, f32), jnp.zeros((tq, 1), f32), jnp.zeros((tq, HEAD_DIM), f32))
    state = lax.fori_loop(0, qi, body, init)
    r = lax.broadcasted_iota(i32, (tq, tq), 0)
    c = lax.broadcasted_iota(i32, (tq, tq), 1)
    _, l, acc = update(jnp.where(c <= r, scores(qi), -jnp.inf), v_ref[rows_of(qi), :], state)
    o_ref[...] = acc / l


def _attention(qkv, c, batch, seq, nheads):
    t = qkv.shape[0]
    tq = 512
    nq = seq // tq
    return pl.pallas_call(
        functools.partial(_attn_kernel, tq=tq),
        grid=(batch, nheads, nq),
        in_specs=[
            pl.BlockSpec((tq, HEAD_DIM), lambda b, h, qi: (b * nq + qi, h)),
            pl.BlockSpec((seq, HEAD_DIM), lambda b, h, qi: (b, nheads + h)),
            pl.BlockSpec((seq, HEAD_DIM), lambda b, h, qi: (b, 2 * nheads + h)),
            pl.BlockSpec((tq, LANES), lambda b, h, qi: (b * nq + qi, 0)),
            pl.BlockSpec((seq, LANES), lambda b, h, qi: (b, 0)),
        ],
        out_specs=pl.BlockSpec((tq, HEAD_DIM), lambda b, h, qi: (b * nq + qi, h)),
        out_shape=jax.ShapeDtypeStruct((t, nheads * HEAD_DIM), f32),
        scratch_shapes=[pltpu.VMEM((seq, 2 * HEAD_DIM), bf16)],
        compiler_params=_cparams(("arbitrary", "arbitrary", "arbitrary")),
        name="attention",
    )(qkv, qkv, qkv, c, c)


def _outproj_kernel(att_ref, conv_ref, ga_ref, gc_ref, w_ref, x_ref, o_ref, mixed):
    j = pl.program_id(1)
    wa = att_ref.shape[1]

    @pl.when(j == 0)
    def _():
        mixed[:, :wa] = _rms(att_ref[...], ga_ref[...]).astype(bf16)
        mixed[:, wa:] = _rms(conv_ref[...], gc_ref[...]).astype(bf16)

    o_ref[...] = x_ref[...] + jnp.dot(mixed[...], w_ref[...], preferred_element_type=f32)


def _outproj(att, conv, g_a, g_c, w_o, x2):
    t, d = x2.shape
    wa, wc = att.shape[1], conv.shape[1]
    tm, tn = 512, 1024
    return pl.pallas_call(
        _outproj_kernel,
        grid=(t // tm, d // tn),
        in_specs=[
            pl.BlockSpec((tm, wa), lambda i, j: (i, 0)),
            pl.BlockSpec((tm, wc), lambda i, j: (i, 0)),
            pl.BlockSpec((1, wa), lambda i, j: (0, 0)),
            pl.BlockSpec((1, wc), lambda i, j: (0, 0)),
            pl.BlockSpec((wa + wc, tn), lambda i, j: (0, j)),
            pl.BlockSpec((tm, tn), lambda i, j: (i, j)),
        ],
        out_specs=pl.BlockSpec((tm, tn), lambda i, j: (i, j)),
        out_shape=jax.ShapeDtypeStruct((t, d), f32),
        scratch_shapes=[pltpu.VMEM((tm, wa + wc), bf16)],
        compiler_params=_cparams(("arbitrary", "arbitrary")),
        name="outproj",
    )(att, conv, g_a.reshape(1, wa), g_c.reshape(1, wc), w_o, x2)


def _router_kernel(x_ref, g_ref, wrt_ref, br_ref, lg_ref):
    h_hi, h_lo, _ = _split3(_rms(x_ref[...], g_ref[...]))
    w_hi, w_lo, _ = _split3(wrt_ref[...])
    lg_ref[...] = _dot_nt(w_hi, h_hi) + (_dot_nt(w_lo, h_hi) + _dot_nt(w_hi, h_lo)) + br_ref[...]


def _router(x1, g_ffn, w_router, b_router):
    t, d = x1.shape
    ne = w_router.shape[1]
    tm = 512
    return pl.pallas_call(
        _router_kernel,
        grid=(t // tm,),
        in_specs=[
            pl.BlockSpec((tm, d), lambda i: (i, 0)),
            pl.BlockSpec((1, d), lambda i: (0, 0)),
            pl.BlockSpec((ne, d), lambda i: (0, 0)),
            pl.BlockSpec((ne, 1), lambda i: (0, 0)),
        ],
        out_specs=pl.BlockSpec((ne, tm), lambda i: (0, i)),
        out_shape=jax.ShapeDtypeStruct((ne, t), f32),
        compiler_params=_cparams(("arbitrary",)),
        name="router",
    )(x1, g_ffn.reshape(1, d), w_router.T, b_router.reshape(ne, 1))


def _cumsum_sublanes(x):
    n = x.shape[0]
    row = lax.broadcasted_iota(i32, x.shape, 0)
    d = 1
    while d < n:
        x = x + jnp.where(row >= d, pltpu.roll(x, d, 0), 0.0)
        d *= 2
    return x


def _stack_rows(rows, n):
    width = rows[0].shape[1]
    sub = lax.broadcasted_iota(i32, (n, width), 0)
    out = jnp.zeros((n, width), rows[0].dtype)
    for k, r in enumerate(rows):
        out = jnp.where(sub == k, r, out)
    return out


def _route_kernel(lg_ref, pos_ref, gcol_ref, meta_ref, esel, rsel, *, chunk):
    ne, t = lg_ref.shape
    e_iota = lax.broadcasted_iota(i32, (ne, chunk), 0).astype(f32)
    a = lax.broadcasted_iota(i32, (chunk, chunk), 0)
    b = lax.broadcasted_iota(i32, (chunk, chunk), 1)
    before = (a < b).astype(bf16)
    counts = jnp.zeros((ne, 1), f32)
    for c in range(t // chunk):
        sl = slice(c * chunk, (c + 1) * chunk)
        vals = lg_ref[:, sl]
        tops, idxs, hots = [], [], []
        for _ in range(TOP_K):
            m = jnp.max(vals, axis=0, keepdims=True)
            idx = jnp.min(jnp.where(vals == m, e_iota, ne), axis=0, keepdims=True)
            hot = e_iota == idx
            vals = jnp.where(hot, -jnp.inf, vals)
            tops.append(m)
            idxs.append(idx)
            hots.append(hot)
        ex = [jnp.exp(v - tops[0]) for v in tops]
        den = ex[0] + ex[1] + ex[2] + ex[3]
        gates = [e / den for e in ex]
        member = sum(h.astype(f32) for h in hots)
        rank = jnp.dot(member.astype(bf16), before, preferred_element_type=f32) + counts
        counts = counts + jnp.sum(member, axis=1, keepdims=True)
        for k in range(TOP_K):
            rsel[k:k + 1, sl] = jnp.sum(jnp.where(hots[k], rank, 0.0), axis=0, keepdims=True)
            esel[k:k + 1, sl] = idxs[k]
        g8 = _stack_rows(gates, SUBLANES)
        gpad = jnp.concatenate([g8, jnp.zeros((LANES - SUBLANES, chunk), f32)], axis=0)
        gcol_ref[sl, :] = gpad.T

    cnt = jnp.broadcast_to(counts, (ne, LANES))
    pcnt = jnp.ceil(cnt * (1.0 / ROW_PAD)) * ROW_PAD
    pend = _cumsum_sublanes(pcnt)
    pstart = pend - pcnt
    npass = jnp.floor((pcnt + (EXPERT_CAP - ROW_PAD)) / EXPERT_CAP + 0.5 * ROW_PAD / EXPERT_CAP)
    iend = _cumsum_sublanes(npass)
    istart = iend - npass
    n_items = jnp.max(iend, axis=0, keepdims=True)

    pstart_col = pstart[:, 0:1]
    for c in range(t // chunk):
        sl = slice(c * chunk, (c + 1) * chunk)
        for k in range(TOP_K):
            hot = e_iota == esel[k:k + 1, sl]
            ps = jnp.sum(jnp.where(hot, pstart_col, 0.0), axis=0, keepdims=True)
            pos_ref[k:k + 1, sl] = (rsel[k:k + 1, sl] + ps).astype(i32)

    e_sub = lax.broadcasted_iota(i32, (ne, LANES), 0)
    lane = lax.broadcasted_iota(i32, (ne, LANES), 1)
    lane_f = lane.astype(f32)
    item = lane_f[0:1, :]
    e_of = jnp.minimum(jnp.sum((iend <= lane_f).astype(f32), axis=0, keepdims=True), ne - 1.0)
    hot = e_sub.astype(f32) == e_of
    pick = lambda v: jnp.sum(jnp.where(hot, v, 0.0), axis=0, keepdims=True)
    p_i = item - pick(istart)
    row0 = pick(pstart) + p_i * EXPERT_CAP
    nrows = jnp.clip(pick(pcnt) - p_i * EXPERT_CAP, 0.0, float(EXPERT_CAP))
    nsub = nrows * (1.0 / ROW_PAD)
    diag = e_sub == lane
    zrow = jnp.sum(jnp.where(diag, pend - ROW_PAD, 0.0), axis=0, keepdims=True)
    zval = jnp.sum(jnp.where(diag, (cnt > 0).astype(f32), 0.0), axis=0, keepdims=True)
    total = jnp.max(pend, axis=0, keepdims=True)
    rows = [e_of, row0, nsub, n_items, zrow, zval, total]
    meta_ref[...] = _stack_rows([r.astype(i32) for r in rows], SUBLANES)


def _route(logits_t):
    ne, t = logits_t.shape
    chunk = 512
    return pl.pallas_call(
        functools.partial(_route_kernel, chunk=chunk),
        out_shape=[
            jax.ShapeDtypeStruct((TOP_K, t), i32),
            jax.ShapeDtypeStruct((t, LANES), f32),
            jax.ShapeDtypeStruct((SUBLANES, LANES), i32),
        ],
        scratch_shapes=[pltpu.VMEM((TOP_K, t), f32), pltpu.VMEM((TOP_K, t), f32)],
        compiler_params=pltpu.CompilerParams(vmem_limit_bytes=VMEM_LIMIT_BYTES),
        name="route",
    )(logits_t)


_M_EXPERT, _M_ROW0, _M_NSUB, _M_NITEMS, _M_ZROW, _M_ZVALID, _M_TOTAL = range(7)


def _zero_tail(meta, zbuf, dst_ref, sem):
    total = pl.multiple_of(meta[_M_TOTAL, 0], ROW_PAD)
    n = (dst_ref.shape[0] - total) // ROW_PAD

    def copy(r):
        return pltpu.make_async_copy(zbuf, dst_ref.at[pl.ds(total + r * ROW_PAD, ROW_PAD), :], sem)

    def start(r, c):
        copy(r).start()
        return c

    def wait(r, c):
        copy(r).wait()
        return c

    lax.fori_loop(0, n, start, 0)
    lax.fori_loop(0, n, wait, 0)


def _dispatch_kernel(meta, pos_ref, x_ref, g_ref, xs_ref, h_scr, zbuf, sem, zsem, *, tmd, ne):
    i = pl.program_id(0)
    h_scr[...] = _rms(x_ref[...], g_ref[...])

    def zero_copy(e):
        start = pl.multiple_of(meta[_M_ZROW, e], ROW_PAD)
        return pltpu.make_async_copy(zbuf, xs_ref.at[pl.ds(start, ROW_PAD), :], zsem)

    @pl.when(i == 0)
    def _():
        zbuf[...] = jnp.zeros_like(zbuf)

        def zstart(e, c):
            @pl.when(meta[_M_ZVALID, e] > 0)
            def _():
                zero_copy(e).start()
            return c

        def zwait(e, c):
            @pl.when(meta[_M_ZVALID, e] > 0)
            def _():
                zero_copy(e).wait()
            return c

        lax.fori_loop(0, ne, zstart, 0)
        lax.fori_loop(0, ne, zwait, 0)
        _zero_tail(meta, zbuf, xs_ref, zsem)

    def row_copy(r, k):
        return pltpu.make_async_copy(h_scr.at[pl.ds(r, 1), :],
                                     xs_ref.at[pl.ds(pos_ref[k, r], 1), :], sem)

    def issue(r, c):
        for k in range(TOP_K):
            row_copy(r, k).start()
        return c

    lax.fori_loop(0, tmd, issue, 0)
    for _ in range(TOP_K):
        pltpu.make_async_copy(h_scr, xs_ref.at[pl.ds(0, tmd), :], sem).wait()


def _dispatch(meta, pos_t, x1, g_ffn, n_rows, ne):
    t, d = x1.shape
    tmd = 256
    return pl.pallas_call(
        functools.partial(_dispatch_kernel, tmd=tmd, ne=ne),
        grid_spec=pltpu.PrefetchScalarGridSpec(
            num_scalar_prefetch=1,
            grid=(t // tmd,),
            in_specs=[
                pl.BlockSpec((TOP_K, tmd), lambda i, m: (0, i), memory_space=pltpu.SMEM),
                pl.BlockSpec((tmd, d), lambda i, m: (i, 0)),
                pl.BlockSpec((1, d), lambda i, m: (0, 0)),
            ],
            out_specs=pl.BlockSpec(memory_space=pl.ANY),
            scratch_shapes=[pltpu.VMEM((tmd, d), f32), pltpu.VMEM((ROW_PAD, d), f32),
                            pltpu.SemaphoreType.DMA, pltpu.SemaphoreType.DMA],
        ),
        out_shape=jax.ShapeDtypeStruct((n_rows, d), f32),
        compiler_params=_cparams(("arbitrary",)),
        name="dispatch",
    )(meta, pos_t, x1, g_ffn.reshape(1, d))


def _expert_kernel(meta, xs_ref, wgu_hbm, wd_hbm, bgu_ref, bd_ref, ys_ref,
                   x_scr, h_scr, wbuf, stage, y_scr, w_sem, ld_sem, st_sem,
                   *, nf, nd, tf, td):
    it = pl.program_id(0)
    n_items = meta[_M_NITEMS, 0]
    f_dim = wd_hbm.shape[1]

    def up_copies(e, f, slot):
        cols = pl.ds(pl.multiple_of(f * tf, tf), tf)
        cols_up = pl.ds(pl.multiple_of(f_dim + f * tf, tf), tf)
        return (pltpu.make_async_copy(wgu_hbm.at[e, :, cols], wbuf.at[slot, :, pl.ds(0, tf)], w_sem.at[2 * slot]),
                pltpu.make_async_copy(wgu_hbm.at[e, :, cols_up], wbuf.at[slot, :, pl.ds(tf, tf)],
                                      w_sem.at[2 * slot + 1]))

    def down_rows(dcol):
        return pl.ds(pl.multiple_of((dcol % 2) * f_dim, f_dim), f_dim)

    def down_copy(e, dcol):
        cols = pl.ds(pl.multiple_of(dcol * td, td), td)
        return pltpu.make_async_copy(wd_hbm.at[e, :, cols], wbuf.at[(dcol // 2) % 2, down_rows(dcol), :],
                                     w_sem.at[dcol % 4])

    def load_copy(first_row, r, slot):
        return pltpu.make_async_copy(xs_ref.at[pl.ds(first_row + r * ROW_PAD, ROW_PAD), :],
                                     stage.at[slot], ld_sem.at[slot])

    def rows_of(r):
        return pl.ds(pl.multiple_of(r * ROW_PAD, ROW_PAD), ROW_PAD)

    @pl.when(it < n_items)
    def _():
        e = meta[_M_EXPERT, it]
        row0 = pl.multiple_of(meta[_M_ROW0, it], ROW_PAD)
        nsub = meta[_M_NSUB, it]
        has_next = it + 1 < n_items
        nxt = jnp.minimum(it + 1, n_items - 1)
        next_row0 = pl.multiple_of(meta[_M_ROW0, nxt], ROW_PAD)
        next_nsub = meta[_M_NSUB, nxt]
        xcur = it % 2

        @pl.when(it == 0)
        def _():
            for c in up_copies(e, 0, 0):
                c.start()
            stage[0] = jnp.zeros(stage.shape[1:], f32)
            _zero_tail(meta, stage.at[0], ys_ref, ld_sem.at[0])
            load_copy(row0, 0, 0).start()

            def body(r, c):
                slot = r % 2
                load_copy(row0, r, slot).wait()

                @pl.when(r + 1 < nsub)
                def _():
                    load_copy(row0, r + 1, 1 - slot).start()

                x_scr[0, rows_of(r), :] = stage[slot].astype(bf16)
                return c

            lax.fori_loop(0, nsub, body, 0)

        def for_row_tiles(fn):
            per = COMPUTE_ROWS // ROW_PAD
            n_full = nsub // per
            tail = nsub % per
            fused = (tail == 1) & (n_full >= 2) & (n_full % 2 == 0)
            n_pairs = n_full // 2 - fused.astype(i32)

            def pair(r, c):
                first = pl.multiple_of(r * (2 * COMPUTE_ROWS), COMPUTE_ROWS)
                fn(first, COMPUTE_ROWS)
                fn(first + COMPUTE_ROWS, COMPUTE_ROWS)
                return c

            lax.fori_loop(0, n_pairs, pair, 0)

            @pl.when(fused)
            def _():
                first = pl.multiple_of(n_pairs * (2 * COMPUTE_ROWS), COMPUTE_ROWS)
                fn(first, COMPUTE_ROWS)
                fn(first + COMPUTE_ROWS, COMPUTE_ROWS)
                fn(first + 2 * COMPUTE_ROWS, ROW_PAD)

            @pl.when(n_full % 2 == 1)
            def _():
                fn(pl.multiple_of((n_full - 1) * COMPUTE_ROWS, COMPUTE_ROWS), COMPUTE_ROWS)

            for t in range(1, per):
                @pl.when((tail == t) & jnp.logical_not(fused))
                def _():
                    fn(pl.multiple_of(n_full * COMPUTE_ROWS, COMPUTE_ROWS), t * ROW_PAD)

        def up_chunk(f, carry):
            slot = f % 2
            for c in up_copies(e, f, slot):
                c.wait()

            @pl.when(f + 1 < nf)
            def _():
                for c in up_copies(e, f + 1, 1 - slot):
                    c.start()

            @pl.when(f + 1 == nf)
            def _():
                down_copy(e, 0).start()
                down_copy(e, 1).start()

            bg = bgu_ref[f]
            bu = bgu_ref[nf + f]

            for j in range(stage.shape[0]):
                @pl.when(has_next & (f + j * nf < next_nsub))
                def _():
                    load_copy(next_row0, f + j * nf, j).start()

            def up_tile(first, n):
                rows = pl.ds(first, n)
                gu = jnp.dot(x_scr[xcur, rows, :], wbuf[slot].astype(bf16), preferred_element_type=f32)
                g = jnp.minimum(gu[:, :tf] + bg, SWIGLU_LIMIT)
                u = jnp.clip(gu[:, tf:] + bu, -SWIGLU_LIMIT, SWIGLU_LIMIT)
                act = (u + 1.0) * (g * jax.nn.sigmoid(SWIGLU_ALPHA * g))
                h_scr[f, rows, :] = act.astype(bf16)

            for_row_tiles(up_tile)

            for j in range(stage.shape[0]):
                @pl.when(has_next & (f + j * nf < next_nsub))
                def _():
                    load_copy(next_row0, f + j * nf, j).wait()
                    x_scr[1 - xcur, rows_of(f + j * nf), :] = stage[j].astype(bf16)

            return carry

        lax.fori_loop(0, nf, up_chunk, 0)

        def store_copy(first, dcol, sl):
            dst = ys_ref.at[pl.ds(row0 + first, ROW_PAD), pl.ds(pl.multiple_of(dcol * td, td), td)]
            return pltpu.make_async_copy(y_scr.at[sl, pl.ds(first, ROW_PAD), :], dst, st_sem.at[sl])

        def wait_stores(sl, count):
            def w(r, c):
                store_copy(0, 0, sl).wait()
                return c
            lax.fori_loop(0, count, w, 0)

        def down_chunk(dcol, carry):
            slot = (dcol // 2) % 2
            yslot = dcol % 2
            down_copy(e, dcol).wait()

            @pl.when(dcol == 0)
            def _():
                down_copy(e, 2).start()
                down_copy(e, 3).start()

            @pl.when((dcol >= 1) & (dcol + 3 < nd))
            def _():
                down_copy(e, dcol + 3).start()

            @pl.when((dcol + 2 == nd) & has_next)
            def _():
                for c in up_copies(meta[_M_EXPERT, nxt], 0, 0):
                    c.start()

            bd = bd_ref[dcol]

            @pl.when(dcol >= 2)
            def _():
                wait_stores(yslot, nsub)

            @pl.when((dcol < 2) & (it > 0))
            def _():
                wait_stores(yslot, meta[_M_NSUB, jnp.maximum(it - 1, 0)])

            def down_tile(first, n):
                rows = pl.ds(first, n)
                hidden = jnp.concatenate([h_scr[cf, rows, :] for cf in range(nf)], axis=1)
                y_scr[yslot, rows, :] = bd + jnp.dot(hidden, wbuf[slot, down_rows(dcol), :].astype(bf16),
                                                     preferred_element_type=f32)
                for part in range(n // ROW_PAD):
                    store_copy(first + part * ROW_PAD, dcol, yslot).start()

            for_row_tiles(down_tile)

            return carry

        lax.fori_loop(0, nd, down_chunk, 0)

        @pl.when(jnp.logical_not(has_next))
        def _():
            wait_stores(0, nsub)
            wait_stores(1, nsub)


def _experts(meta, xs, w_gate_up, b_gate_up, w_down, b_down):
    n_rows, d = xs.shape
    ne, _, f2 = w_gate_up.shape
    f = f2 // 2
    tf, td = 256, 512
    nf, nd = f // tf, d // td
    max_items = (n_rows + ne * (EXPERT_CAP - ROW_PAD)) // EXPERT_CAP
    stage_slots = max(2, pl.cdiv(EXPERT_CAP // ROW_PAD, nf))
    assert max_items <= MAX_ITEMS and td == 2 * tf and f <= d
    assert nf % 2 == 0 and nd % 4 == 0 and 2 * f == d

    def expert(i, m):
        return m[_M_EXPERT, jnp.minimum(i, m[_M_NITEMS, 0] - 1)]

    return pl.pallas_call(
        functools.partial(_expert_kernel, nf=nf, nd=nd, tf=tf, td=td),
        grid_spec=pltpu.PrefetchScalarGridSpec(
            num_scalar_prefetch=1,
            grid=(max_items,),
            in_specs=[
                pl.BlockSpec(memory_space=pl.ANY),
                pl.BlockSpec(memory_space=pl.ANY),
                pl.BlockSpec(memory_space=pl.ANY),
                pl.BlockSpec((None, 2 * nf, 1, tf), lambda i, m: (expert(i, m), 0, 0, 0)),
                pl.BlockSpec((None, nd, 1, td), lambda i, m: (expert(i, m), 0, 0, 0)),
            ],
            out_specs=pl.BlockSpec(memory_space=pl.ANY),
            scratch_shapes=[
                pltpu.VMEM((2, EXPERT_CAP, d), bf16),
                pltpu.VMEM((nf, EXPERT_CAP, tf), bf16),
                pltpu.VMEM((2, d, 2 * tf), f32),
                pltpu.VMEM((stage_slots, ROW_PAD, d), f32),
                pltpu.VMEM((2, EXPERT_CAP, td), f32),
                pltpu.SemaphoreType.DMA((4,)),
                pltpu.SemaphoreType.DMA((stage_slots,)),
                pltpu.SemaphoreType.DMA((2,)),
            ],
        ),
        out_shape=jax.ShapeDtypeStruct((n_rows, d), f32),
        compiler_params=_cparams(("arbitrary",)),
        name="experts",
    )(meta, xs, w_gate_up, w_down, b_gate_up.reshape(ne, 2 * nf, 1, tf), b_down.reshape(ne, nd, 1, td))


def _combine_kernel(pos_cur, pos_nxt, gcol_ref, x_ref, ys_ref, o_ref, gbuf, sem, *, tmc, n_tiles):
    i = pl.program_id(0)
    slot = i % 2

    def row_copy(pos_ref, sl, r, k):
        return pltpu.make_async_copy(ys_ref.at[pl.ds(pos_ref[k, r], 1), :],
                                     gbuf.at[sl, k, pl.ds(r, 1), :], sem.at[sl])

    def issue(pos_ref, sl):
        def body(r, c):
            for k in range(TOP_K):
                row_copy(pos_ref, sl, r, k).start()
            return c
        lax.fori_loop(0, tmc, body, 0)

    @pl.when(i == 0)
    def _():
        issue(pos_cur, 0)

    @pl.when(i + 1 < n_tiles)
    def _():
        issue(pos_nxt, 1 - slot)

    for k in range(TOP_K):
        pltpu.make_async_copy(ys_ref.at[pl.ds(0, tmc), :], gbuf.at[slot, k], sem.at[slot]).wait()
    acc = x_ref[...]
    for k in range(TOP_K):
        acc = acc + gcol_ref[:, k:k + 1] * gbuf[slot, k]
    o_ref[...] = acc


def _combine(pos_t, gcol, x1, ys):
    t, d = x1.shape
    tmc = 128
    n_tiles = t // tmc
    return pl.pallas_call(
        functools.partial(_combine_kernel, tmc=tmc, n_tiles=n_tiles),
        grid=(n_tiles,),
        in_specs=[
            pl.BlockSpec((TOP_K, tmc), lambda i: (0, i), memory_space=pltpu.SMEM),
            pl.BlockSpec((TOP_K, tmc), lambda i: (0, jnp.minimum(i + 1, n_tiles - 1)), memory_space=pltpu.SMEM),
            pl.BlockSpec((tmc, LANES), lambda i: (i, 0)),
            pl.BlockSpec((tmc, d), lambda i: (i, 0)),
            pl.BlockSpec(memory_space=pl.ANY),
        ],
        out_specs=pl.BlockSpec((tmc, d), lambda i: (i, 0)),
        out_shape=jax.ShapeDtypeStruct((t, d), f32),
        scratch_shapes=[pltpu.VMEM((2, TOP_K, tmc, d), f32), pltpu.SemaphoreType.DMA((2,))],
        compiler_params=_cparams(("arbitrary",)),
        name="combine",
    )(pos_t, pos_t, gcol, x1, ys)


def _layer(x2, batch, seq, g_mix, w_in, b_forget, g_q, g_k, w_conv, g_attn_out, g_conv_out, w_out,
           g_ffn, w_router, b_router, w_gate_up, b_gate_up, w_down, b_down):
    t, d = x2.shape
    nheads = b_forget.shape[0]
    att_width = nheads * HEAD_DIM
    ne = w_router.shape[1]
    c0 = 3 * att_width
    w_t = w_in.T
    conv_width = (w_t.shape[0] - c0 - nheads) // 3

    h, c = _prenorm(x2, g_mix, w_t, c0, b_forget, seq)
    qkv = _qkv(h, w_t, g_q, g_k, att_width)
    conv = _conv(h, w_t, c0 + nheads, conv_width, w_conv, seq)
    att = _attention(qkv, c, batch, seq, nheads)
    x1 = _outproj(att, conv, g_attn_out, g_conv_out, w_out.astype(bf16), x2)

    pos_t, gcol, meta = _route(_router(x1, g_ffn, w_router, b_router))
    n_rows = t * TOP_K + ne * ROW_PAD
    xs = _dispatch(meta, pos_t, x1, g_ffn, n_rows, ne)
    ys = _experts(meta, xs, w_gate_up, b_gate_up, w_down, b_down)
    return _combine(pos_t, gcol, x1, ys)


def kernel(x, g_mix, w_in, b_forget, g_q, g_k, w_conv, g_attn_out, g_conv_out, w_out, g_ffn, w_router,
           b_router, w_gate_up, b_gate_up, w_down, b_down):
    b, s, d = x.shape
    x2 = x.reshape(b * s, d)
    for l in range(g_mix.shape[0]):
        x2 = _layer(x2, b, s, g_mix[l], w_in[l], b_forget[l], g_q[l], g_k[l], w_conv[l], g_attn_out[l],
                    g_conv_out[l], w_out[l], g_ffn[l], w_router[l], b_router[l], w_gate_up[l],
                    b_gate_up[l], w_down[l], b_down[l])
    return x2.reshape(b, s, d)
```

```python
import functools

import jax
import jax.numpy as jnp
import numpy as np
from jax import lax
from jax.experimental import pallas as pl
from jax.experimental.pallas import tpu as pltpu

f32 = jnp.float32
bf16 = jnp.bfloat16
i32 = jnp.int32

HEAD_DIM = 128
LANES = 128
SUBLANES = 8
TOP_K = 4
RMS_EPS = 1e-6
SWIGLU_ALPHA = 1.702
SWIGLU_LIMIT = 7.0
VMEM_LIMIT_BYTES = 56 * 1024 * 1024

ROW_PAD = 128
EXPERT_CAP = 1152
COMPUTE_ROWS = 512
MAX_ITEMS = 128


def _cparams(sem):
    return pltpu.CompilerParams(dimension_semantics=sem, vmem_limit_bytes=VMEM_LIMIT_BYTES)


def _log_sigmoid(z):
    return jnp.minimum(z, 0.0) - jnp.log1p(jnp.exp(-jnp.abs(z)))


def _split3(a):
    hi = a.astype(bf16)
    r1 = a - hi.astype(f32)
    mid = r1.astype(bf16)
    lo = (r1 - mid.astype(f32)).astype(bf16)
    return hi, mid, lo


def _dot_nt(a, b):
    return lax.dot_general(a, b, (((1,), (1,)), ((), ())), preferred_element_type=f32)


def _rms(y, g):
    ms = jnp.mean(y * y, axis=-1, keepdims=True)
    return y * lax.rsqrt(ms + RMS_EPS) * g


def _prenorm_kernel(x_ref, g_ref, wf_ref, bf_ref, h_ref, c_ref, carry, *, tiles_per_seq):
    i = pl.program_id(0)
    hb = _rms(x_ref[...], g_ref[...]).astype(bf16)
    h_ref[...] = hb
    logf = _log_sigmoid(_dot_nt(hb, wf_ref[...].astype(bf16)) + bf_ref[...])
    tm = hb.shape[0]
    a = lax.broadcasted_iota(i32, (tm, tm), 0)
    b = lax.broadcasted_iota(i32, (tm, tm), 1)
    lower = (b <= a).astype(bf16)
    c = sum(jnp.dot(lower, p, preferred_element_type=f32) for p in _split3(logf))

    @pl.when(i % tiles_per_seq == 0)
    def _():
        carry[...] = jnp.zeros_like(carry)

    c = c + carry[...]
    c_ref[...] = c
    carry[...] = c[tm - 1:tm, :]


def _prenorm(x2, g_mix, w_t, f_row0, b_f, seq):
    t, d = x2.shape
    nh = b_f.shape[0]
    tm = 512
    assert f_row0 % LANES == 0 and nh <= LANES
    bfr = jnp.pad(b_f, (0, LANES - nh)).reshape(1, LANES)
    return pl.pallas_call(
        functools.partial(_prenorm_kernel, tiles_per_seq=seq // tm),
        grid=(t // tm,),
        in_specs=[
            pl.BlockSpec((tm, d), lambda i: (i, 0)),
            pl.BlockSpec((1, d), lambda i: (0, 0)),
            pl.BlockSpec((LANES, d), lambda i: (f_row0 // LANES, 0)),
            pl.BlockSpec((1, LANES), lambda i: (0, 0)),
        ],
        out_specs=[
            pl.BlockSpec((tm, d), lambda i: (i, 0)),
            pl.BlockSpec((tm, LANES), lambda i: (i, 0)),
        ],
        out_shape=[
            jax.ShapeDtypeStruct((t, d), bf16),
            jax.ShapeDtypeStruct((t, LANES), f32),
        ],
        scratch_shapes=[pltpu.VMEM((1, LANES), f32)],
        compiler_params=_cparams(("arbitrary",)),
        name="prenorm",
    )(x2, g_mix.reshape(1, d), w_t, bfr)


def _qkv_kernel(h_ref, w_ref, g_ref, post_ref, o_ref, *, n_norm_blocks):
    normed = pl.program_id(0) < n_norm_blocks
    tm, tn = o_ref.shape
    for half in range(2):
        rows = slice(half * (tm // 2), (half + 1) * (tm // 2))
        y = _dot_nt(h_ref[rows, :], w_ref[...].astype(bf16))
        for hh in range(tn // HEAD_DIM):
            sl = slice(hh * HEAD_DIM, (hh + 1) * HEAD_DIM)
            yh = y[:, sl]
            inv = lax.rsqrt(jnp.mean(yh * yh, axis=-1, keepdims=True) + RMS_EPS)
            o_ref[rows, sl] = (yh * jnp.where(normed, inv, 1.0) * g_ref[:, sl] * post_ref[:, sl]).astype(bf16)


def _qkv(h, w_t, g_q, g_k, att_width):
    t, d = h.shape
    n = 3 * att_width
    tm, tn = 1024, 512
    nheads = att_width // HEAD_DIM
    ones = jnp.ones((att_width,), f32)
    gain = jnp.concatenate([jnp.tile(g_q, nheads), jnp.tile(g_k, nheads), ones]).reshape(1, n)
    post = jnp.concatenate([ones * (1.0 / np.sqrt(HEAD_DIM)), ones, ones]).reshape(1, n)
    return pl.pallas_call(
        functools.partial(_qkv_kernel, n_norm_blocks=2 * att_width // tn),
        grid=(n // tn, t // tm),
        in_specs=[
            pl.BlockSpec((tm, d), lambda j, i: (i, 0)),
            pl.BlockSpec((tn, d), lambda j, i: (j, 0)),
            pl.BlockSpec((1, tn), lambda j, i: (0, j)),
            pl.BlockSpec((1, tn), lambda j, i: (0, j)),
        ],
        out_specs=pl.BlockSpec((tm, tn), lambda j, i: (i, j)),
        out_shape=jax.ShapeDtypeStruct((t, n), bf16),
        compiler_params=_cparams(("arbitrary", "arbitrary")),
        name="qkv",
    )(h, w_t, gain, post)


def _conv_kernel(h_ref, wb_ref, wc_ref, wx_ref, wk_ref, o_ref, carry, *, tiles_per_seq):
    i = pl.program_id(1)
    tm, tn = o_ref.shape
    hm = tm // 2
    prev = jnp.where(i % tiles_per_seq == 0, 0.0, carry[...])
    p2 = prev[SUBLANES - 2:SUBLANES - 1, :]
    p1 = prev[SUBLANES - 1:SUBLANES, :]
    row = lax.broadcasted_iota(i32, (hm, tn), 0)
    for half in range(2):
        rows = slice(half * hm, (half + 1) * hm)
        h = h_ref[rows, :]
        gb = _dot_nt(h, wb_ref[...].astype(bf16))
        gc = _dot_nt(h, wc_ref[...].astype(bf16))
        xc = _dot_nt(h, wx_ref[...].astype(bf16))
        u = gc * xc
        u1 = jnp.where(row == 0, p1, pltpu.roll(u, 1, 0))
        u2 = jnp.where(row == 0, p2, jnp.where(row == 1, p1, pltpu.roll(u, 2, 0)))
        o_ref[rows, :] = gb * (wk_ref[0:1, :] * u2 + wk_ref[1:2, :] * u1 + wk_ref[2:3, :] * u)
        p2 = u[hm - 2:hm - 1, :]
        p1 = u[hm - 1:hm, :]
    carry[...] = u[hm - SUBLANES:hm, :]


def _conv(h, w_t, row0, width, w_conv, seq):
    t, d = h.shape
    tm, tn = 1024, 256
    assert row0 % SUBLANES == 0 and width % tn == 0

    def w_spec(part):
        return pl.BlockSpec((pl.Element(tn), pl.Element(d)),
                            lambda j, i: (pl.multiple_of(row0 + part * width + j * tn, SUBLANES), 0))

    return pl.pallas_call(
        functools.partial(_conv_kernel, tiles_per_seq=seq // tm),
        grid=(width // tn, t // tm),
        in_specs=[
            pl.BlockSpec((tm, d), lambda j, i: (i, 0)),
            w_spec(0), w_spec(1), w_spec(2),
            pl.BlockSpec((3, tn), lambda j, i: (0, j)),
        ],
        out_specs=pl.BlockSpec((tm, tn), lambda j, i: (i, j)),
        out_shape=jax.ShapeDtypeStruct((t, width), f32),
        scratch_shapes=[pltpu.VMEM((SUBLANES, tn), f32)],
        compiler_params=_cparams(("arbitrary", "arbitrary")),
        name="conv",
    )(h, w_t, w_t, w_t, w_conv)


def _bias_lanes(c, ones_first):
    n = c.shape[0]
    lane = lax.broadcasted_iota(i32, (n, LANES), 1)
    out = jnp.zeros((n, LANES), f32)
    base_c, base_1 = (3, 0) if ones_first else (0, 3)
    for k, piece in enumerate(_split3(c)):
        out = jnp.where(lane == base_c + k, piece.astype(f32), out)
        out = jnp.where(lane == base_1 + k, 1.0, out)
    return out.astype(bf16)


def _attn_kernel(q_ref, k_ref, v_ref, cq_ref, ck_ref, o_ref, k_aug, *, tq):
    hd = pl.program_id(1)
    qi = pl.program_id(2)

    def head_column(c_ref):
        lane = lax.broadcasted_iota(i32, c_ref.shape, 1)
        return jnp.sum(jnp.where(lane == hd, c_ref[...], 0.0), axis=1, keepdims=True)

    @pl.when(qi == 0)
    def _():
        k_aug[:, :HEAD_DIM] = k_ref[...]
        k_aug[:, HEAD_DIM:] = _bias_lanes(-head_column(ck_ref), ones_first=True)

    q = jnp.concatenate([q_ref[...], _bias_lanes(head_column(cq_ref), ones_first=False)], axis=1)
    def rows_of(j):
        return pl.ds(pl.multiple_of(j * tq, tq), tq)

    def scores(j):
        return _dot_nt(q, k_aug[rows_of(j), :])

    def update(s, vj, state):
        m, l, acc = state
        m_new = jnp.maximum(m, jnp.max(s, axis=1, keepdims=True))
        alpha = jnp.exp(m - m_new)
        p = jnp.exp(s - m_new)
        l = alpha * l + jnp.sum(p, axis=1, keepdims=True)
        acc = alpha * acc + jnp.dot(p.astype(bf16), vj, preferred_element_type=f32)
        return m_new, l, acc

    def body(j, state):
        return update(scores(j), v_ref[rows_of(j), :], state)

    init = (jnp.full((tq, 1), -jnp.inf, f32), jnp.zeros((tq, 1), f32), jnp.zeros((tq, HEAD_DIM), f32))
    state = lax.fori_loop(0, qi, body, init)
    r = lax.broadcasted_iota(i32, (tq, tq), 0)
    c = lax.broadcasted_iota(i32, (tq, tq), 1)
    _, l, acc = update(jnp.where(c <= r, scores(qi), -jnp.inf), v_ref[rows_of(qi), :], state)
    o_ref[...] = acc / l


def _attention(qkv, c, batch, seq, nheads):
    t = qkv.shape[0]
    tq = 512
    nq = seq // tq
    return pl.pallas_call(
        functools.partial(_attn_kernel, tq=tq),
        grid=(batch, nheads, nq),
        in_specs=[
            pl.BlockSpec((tq, HEAD_DIM), lambda b, h, qi: (b * nq + qi, h)),
            pl.BlockSpec((seq, HEAD_DIM), lambda b, h, qi: (b, nheads + h)),
            pl.BlockSpec((seq, HEAD_DIM), lambda b, h, qi: (b, 2 * nheads + h)),
            pl.BlockSpec((tq, LANES), lambda b, h, qi: (b * nq + qi, 0)),
            pl.BlockSpec((seq, LANES), lambda b, h, qi: (b, 0)),
        ],
        out_specs=pl.BlockSpec((tq, HEAD_DIM), lambda b, h, qi: (b * nq + qi, h)),
        out_shape=jax.ShapeDtypeStruct((t, nheads * HEAD_DIM), f32),
        scratch_shapes=[pltpu.VMEM((seq, 2 * HEAD_DIM), bf16)],
        compiler_params=_cparams(("arbitrary", "arbitrary", "arbitrary")),
        name="attention",
    )(qkv, qkv, qkv, c, c)


def _outproj_kernel(att_ref, conv_ref, ga_ref, gc_ref, w_ref, x_ref, o_ref, mixed):
    j = pl.program_id(1)
    wa = att_ref.shape[1]

    @pl.when(j == 0)
    def _():
        mixed[:, :wa] = _rms(att_ref[...], ga_ref[...]).astype(bf16)
        mixed[:, wa:] = _rms(conv_ref[...], gc_ref[...]).astype(bf16)

    o_ref[...] = x_ref[...] + jnp.dot(mixed[...], w_ref[...], preferred_element_type=f32)


def _outproj(att, conv, g_a, g_c, w_o, x2):
    t, d = x2.shape
    wa, wc = att.shape[1], conv.shape[1]
    tm, tn = 512, 1024
    return pl.pallas_call(
        _outproj_kernel,
        grid=(t // tm, d // tn),
        in_specs=[
            pl.BlockSpec((tm, wa), lambda i, j: (i, 0)),
            pl.BlockSpec((tm, wc), lambda i, j: (i, 0)),
            pl.BlockSpec((1, wa), lambda i, j: (0, 0)),
            pl.BlockSpec((1, wc), lambda i, j: (0, 0)),
            pl.BlockSpec((wa + wc, tn), lambda i, j: (0, j)),
            pl.BlockSpec((tm, tn), lambda i, j: (i, j)),
        ],
        out_specs=pl.BlockSpec((tm, tn), lambda i, j: (i, j)),
        out_shape=jax.ShapeDtypeStruct((t, d), f32),
        scratch_shapes=[pltpu.VMEM((tm, wa + wc), bf16)],
        compiler_params=_cparams(("arbitrary", "arbitrary")),
        name="outproj",
    )(att, conv, g_a.reshape(1, wa), g_c.reshape(1, wc), w_o, x2)


def _router_kernel(x_ref, g_ref, wrt_ref, br_ref, lg_ref):
    h_hi, h_lo, _ = _split3(_rms(x_ref[...], g_ref[...]))
    w_hi, w_lo, _ = _split3(wrt_ref[...])
    lg_ref[...] = _dot_nt(w_hi, h_hi) + (_dot_nt(w_lo, h_hi) + _dot_nt(w_hi, h_lo)) + br_ref[...]


def _router(x1, g_ffn, w_router, b_router):
    t, d = x1.shape
    ne = w_router.shape[1]
    tm = 512
    return pl.pallas_call(
        _router_kernel,
        grid=(t // tm,),
        in_specs=[
            pl.BlockSpec((tm, d), lambda i: (i, 0)),
            pl.BlockSpec((1, d), lambda i: (0, 0)),
            pl.BlockSpec((ne, d), lambda i: (0, 0)),
            pl.BlockSpec((ne, 1), lambda i: (0, 0)),
        ],
        out_specs=pl.BlockSpec((ne, tm), lambda i: (0, i)),
        out_shape=jax.ShapeDtypeStruct((ne, t), f32),
        compiler_params=_cparams(("arbitrary",)),
        name="router",
    )(x1, g_ffn.reshape(1, d), w_router.T, b_router.reshape(ne, 1))


def _cumsum_sublanes(x):
    n = x.shape[0]
    row = lax.broadcasted_iota(i32, x.shape, 0)
    d = 1
    while d < n:
        x = x + jnp.where(row >= d, pltpu.roll(x, d, 0), 0.0)
        d *= 2
    return x


def _stack_rows(rows, n):
    width = rows[0].shape[1]
    sub = lax.broadcasted_iota(i32, (n, width), 0)
    out = jnp.zeros((n, width), rows[0].dtype)
    for k, r in enumerate(rows):
        out = jnp.where(sub == k, r, out)
    return out


def _route_kernel(lg_ref, pos_ref, gcol_ref, meta_ref, esel, rsel, *, chunk):
    ne, t = lg_ref.shape
    e_iota = lax.broadcasted_iota(i32, (ne, chunk), 0).astype(f32)
    a = lax.broadcasted_iota(i32, (chunk, chunk), 0)
    b = lax.broadcasted_iota(i32, (chunk, chunk), 1)
    before = (a < b).astype(bf16)
    counts = jnp.zeros((ne, 1), f32)
    for c in range(t // chunk):
        sl = slice(c * chunk, (c + 1) * chunk)
        vals = lg_ref[:, sl]
        tops, idxs, hots = [], [], []
        for _ in range(TOP_K):
            m = jnp.max(vals, axis=0, keepdims=True)
            idx = jnp.min(jnp.where(vals == m, e_iota, ne), axis=0, keepdims=True)
            hot = e_iota == idx
            vals = jnp.where(hot, -jnp.inf, vals)
            tops.append(m)
            idxs.append(idx)
            hots.append(hot)
        ex = [jnp.exp(v - tops[0]) for v in tops]
        den = ex[0] + ex[1] + ex[2] + ex[3]
        gates = [e / den for e in ex]
        member = sum(h.astype(f32) for h in hots)
        rank = jnp.dot(member.astype(bf16), before, preferred_element_type=f32) + counts
        counts = counts + jnp.sum(member, axis=1, keepdims=True)
        for k in range(TOP_K):
            rsel[k:k + 1, sl] = jnp.sum(jnp.where(hots[k], rank, 0.0), axis=0, keepdims=True)
            esel[k:k + 1, sl] = idxs[k]
        g8 = _stack_rows(gates, SUBLANES)
        gpad = jnp.concatenate([g8, jnp.zeros((LANES - SUBLANES, chunk), f32)], axis=0)
        gcol_ref[sl, :] = gpad.T

    cnt = jnp.broadcast_to(counts, (ne, LANES))
    pcnt = jnp.ceil(cnt * (1.0 / ROW_PAD)) * ROW_PAD
    pend = _cumsum_sublanes(pcnt)
    pstart = pend - pcnt
    npass = jnp.floor((pcnt + (EXPERT_CAP - ROW_PAD)) / EXPERT_CAP + 0.5 * ROW_PAD / EXPERT_CAP)
    iend = _cumsum_sublanes(npass)
    istart = iend - npass
    n_items = jnp.max(iend, axis=0, keepdims=True)

    pstart_col = pstart[:, 0:1]
    for c in range(t // chunk):
        sl = slice(c * chunk, (c + 1) * chunk)
        for k in range(TOP_K):
            hot = e_iota == esel[k:k + 1, sl]
            ps = jnp.sum(jnp.where(hot, pstart_col, 0.0), axis=0, keepdims=True)
            pos_ref[k:k + 1, sl] = (rsel[k:k + 1, sl] + ps).astype(i32)

    e_sub = lax.broadcasted_iota(i32, (ne, LANES), 0)
    lane = lax.broadcasted_iota(i32, (ne, LANES), 1)
    lane_f = lane.astype(f32)
    item = lane_f[0:1, :]
    e_of = jnp.minimum(jnp.sum((iend <= lane_f).astype(f32), axis=0, keepdims=True), ne - 1.0)
    hot = e_sub.astype(f32) == e_of
    pick = lambda v: jnp.sum(jnp.where(hot, v, 0.0), axis=0, keepdims=True)
    p_i = item - pick(istart)
    row0 = pick(pstart) + p_i * EXPERT_CAP
    nrows = jnp.clip(pick(pcnt) - p_i * EXPERT_CAP, 0.0, float(EXPERT_CAP))
    nsub = nrows * (1.0 / ROW_PAD)
    diag = e_sub == lane
    zrow = jnp.sum(jnp.where(diag, pend - ROW_PAD, 0.0), axis=0, keepdims=True)
    zval = jnp.sum(jnp.where(diag, (cnt > 0).astype(f32), 0.0), axis=0, keepdims=True)
    total = jnp.max(pend, axis=0, keepdims=True)
    rows = [e_of, row0, nsub, n_items, zrow, zval, total]
    meta_ref[...] = _stack_rows([r.astype(i32) for r in rows], SUBLANES)


def _route(logits_t):
    ne, t = logits_t.shape
    chunk = 512
    return pl.pallas_call(
        functools.partial(_route_kernel, chunk=chunk),
        out_shape=[
            jax.ShapeDtypeStruct((TOP_K, t), i32),
            jax.ShapeDtypeStruct((t, LANES), f32),
            jax.ShapeDtypeStruct((SUBLANES, LANES), i32),
        ],
        scratch_shapes=[pltpu.VMEM((TOP_K, t), f32), pltpu.VMEM((TOP_K, t), f32)],
        compiler_params=pltpu.CompilerParams(vmem_limit_bytes=VMEM_LIMIT_BYTES),
        name="route",
    )(logits_t)


_M_EXPERT, _M_ROW0, _M_NSUB, _M_NITEMS, _M_ZROW, _M_ZVALID, _M_TOTAL = range(7)


def _zero_tail(meta, zbuf, dst_ref, sem):
    total = pl.multiple_of(meta[_M_TOTAL, 0], ROW_PAD)
    n = (dst_ref.shape[0] - total) // ROW_PAD

    def copy(r):
        return pltpu.make_async_copy(zbuf, dst_ref.at[pl.ds(total + r * ROW_PAD, ROW_PAD), :], sem)

    def start(r, c):
        copy(r).start()
        return c

    def wait(r, c):
        copy(r).wait()
        return c

    lax.fori_loop(0, n, start, 0)
    lax.fori_loop(0, n, wait, 0)


def _dispatch_kernel(meta, pos_ref, x_ref, g_ref, xs_ref, h_scr, zbuf, sem, zsem, *, tmd, ne):
    i = pl.program_id(0)
    h_scr[...] = _rms(x_ref[...], g_ref[...])

    def zero_copy(e):
        start = pl.multiple_of(meta[_M_ZROW, e], ROW_PAD)
        return pltpu.make_async_copy(zbuf, xs_ref.at[pl.ds(start, ROW_PAD), :], zsem)

    @pl.when(i == 0)
    def _():
        zbuf[...] = jnp.zeros_like(zbuf)

        def zstart(e, c):
            @pl.when(meta[_M_ZVALID, e] > 0)
            def _():
                zero_copy(e).start()
            return c

        def zwait(e, c):
            @pl.when(meta[_M_ZVALID, e] > 0)
            def _():
                zero_copy(e).wait()
            return c

        lax.fori_loop(0, ne, zstart, 0)
        lax.fori_loop(0, ne, zwait, 0)
        _zero_tail(meta, zbuf, xs_ref, zsem)

    def row_copy(r, k):
        return pltpu.make_async_copy(h_scr.at[pl.ds(r, 1), :],
                                     xs_ref.at[pl.ds(pos_ref[k, r], 1), :], sem)

    def issue(r, c):
        for k in range(TOP_K):
            row_copy(r, k).start()
        return c

    lax.fori_loop(0, tmd, issue, 0, unroll=2)
    for _ in range(TOP_K):
        pltpu.make_async_copy(h_scr, xs_ref.at[pl.ds(0, tmd), :], sem).wait()


def _dispatch(meta, pos_t, x1, g_ffn, n_rows, ne):
    t, d = x1.shape
    tmd = 256
    return pl.pallas_call(
        functools.partial(_dispatch_kernel, tmd=tmd, ne=ne),
        grid_spec=pltpu.PrefetchScalarGridSpec(
            num_scalar_prefetch=1,
            grid=(t // tmd,),
            in_specs=[
                pl.BlockSpec((TOP_K, tmd), lambda i, m: (0, i), memory_space=pltpu.SMEM),
                pl.BlockSpec((tmd, d), lambda i, m: (i, 0)),
                pl.BlockSpec((1, d), lambda i, m: (0, 0)),
            ],
            out_specs=pl.BlockSpec(memory_space=pl.ANY),
            scratch_shapes=[pltpu.VMEM((tmd, d), f32), pltpu.VMEM((ROW_PAD, d), f32),
                            pltpu.SemaphoreType.DMA, pltpu.SemaphoreType.DMA],
        ),
        out_shape=jax.ShapeDtypeStruct((n_rows, d), f32),
        compiler_params=_cparams(("arbitrary",)),
        name="dispatch",
    )(meta, pos_t, x1, g_ffn.reshape(1, d))


def _expert_kernel(meta, xs_ref, wgu_hbm, wd_hbm, bgu_ref, bd_ref, ys_ref,
                   x_scr, h_scr, wbuf, stage, y_scr, w_sem, ld_sem, st_sem,
                   *, nf, nd, tf, td):
    it = pl.program_id(0)
    n_items = meta[_M_NITEMS, 0]
    f_dim = wd_hbm.shape[1]

    def up_copies(e, f, slot):
        cols = pl.ds(pl.multiple_of(f * tf, tf), tf)
        cols_up = pl.ds(pl.multiple_of(f_dim + f * tf, tf), tf)
        return (pltpu.make_async_copy(wgu_hbm.at[e, :, cols], wbuf.at[slot, :, pl.ds(0, tf)], w_sem.at[2 * slot]),
                pltpu.make_async_copy(wgu_hbm.at[e, :, cols_up], wbuf.at[slot, :, pl.ds(tf, tf)],
                                      w_sem.at[2 * slot + 1]))

    def down_rows(dcol):
        return pl.ds(pl.multiple_of((dcol % 2) * f_dim, f_dim), f_dim)

    def down_copy(e, dcol):
        cols = pl.ds(pl.multiple_of(dcol * td, td), td)
        return pltpu.make_async_copy(wd_hbm.at[e, :, cols], wbuf.at[(dcol // 2) % 2, down_rows(dcol), :],
                                     w_sem.at[dcol % 4])

    def load_copy(first_row, r, slot):
        return pltpu.make_async_copy(xs_ref.at[pl.ds(first_row + r * ROW_PAD, ROW_PAD), :],
                                     stage.at[slot], ld_sem.at[slot])

    def rows_of(r):
        return pl.ds(pl.multiple_of(r * ROW_PAD, ROW_PAD), ROW_PAD)

    @pl.when(it < n_items)
    def _():
        e = meta[_M_EXPERT, it]
        row0 = pl.multiple_of(meta[_M_ROW0, it], ROW_PAD)
        nsub = meta[_M_NSUB, it]
        has_next = it + 1 < n_items
        nxt = jnp.minimum(it + 1, n_items - 1)
        next_row0 = pl.multiple_of(meta[_M_ROW0, nxt], ROW_PAD)
        next_nsub = meta[_M_NSUB, nxt]
        xcur = it % 2

        @pl.when(it == 0)
        def _():
            for c in up_copies(e, 0, 0):
                c.start()
            stage[0] = jnp.zeros(stage.shape[1:], f32)
            _zero_tail(meta, stage.at[0], ys_ref, ld_sem.at[0])
            load_copy(row0, 0, 0).start()

            def body(r, c):
                slot = r % 2
                load_copy(row0, r, slot).wait()

                @pl.when(r + 1 < nsub)
                def _():
                    load_copy(row0, r + 1, 1 - slot).start()

                x_scr[0, rows_of(r), :] = stage[slot].astype(bf16)
                return c

            lax.fori_loop(0, nsub, body, 0)

        def for_row_tiles(fn):
            per = COMPUTE_ROWS // ROW_PAD
            n_full = nsub // per
            tail = nsub % per
            fused = (tail == 1) & (n_full >= 2) & (n_full % 2 == 0)
            n_pairs = n_full // 2 - fused.astype(i32)

            def pair(r, c):
                first = pl.multiple_of(r * (2 * COMPUTE_ROWS), COMPUTE_ROWS)
                fn(first, COMPUTE_ROWS)
                fn(first + COMPUTE_ROWS, COMPUTE_ROWS)
                return c

            lax.fori_loop(0, n_pairs, pair, 0)

            @pl.when(fused)
            def _():
                first = pl.multiple_of(n_pairs * (2 * COMPUTE_ROWS), COMPUTE_ROWS)
                fn(first, COMPUTE_ROWS)
                fn(first + COMPUTE_ROWS, COMPUTE_ROWS)
                fn(first + 2 * COMPUTE_ROWS, ROW_PAD)

            @pl.when(n_full % 2 == 1)
            def _():
                fn(pl.multiple_of((n_full - 1) * COMPUTE_ROWS, COMPUTE_ROWS), COMPUTE_ROWS)

            for t in range(1, per):
                @pl.when((tail == t) & jnp.logical_not(fused))
                def _():
                    fn(pl.multiple_of(n_full * COMPUTE_ROWS, COMPUTE_ROWS), t * ROW_PAD)

        def up_chunk(f, carry):
            slot = f % 2
            for c in up_copies(e, f, slot):
                c.wait()

            @pl.when(f + 1 < nf)
            def _():
                for c in up_copies(e, f + 1, 1 - slot):
                    c.start()

            @pl.when(f + 1 == nf)
            def _():
                down_copy(e, 0).start()
                down_copy(e, 1).start()

            bg = bgu_ref[f]
            bu = bgu_ref[nf + f]

            for j in range(stage.shape[0]):
                @pl.when(has_next & (f + j * nf < next_nsub))
                def _():
                    load_copy(next_row0, f + j * nf, j).start()

            def up_tile(first, n):
                rows = pl.ds(first, n)
                gu = jnp.dot(x_scr[xcur, rows, :], wbuf[slot].astype(bf16), preferred_element_type=f32)
                g = jnp.minimum(gu[:, :tf] + bg, SWIGLU_LIMIT)
                u = jnp.clip(gu[:, tf:] + bu, -SWIGLU_LIMIT, SWIGLU_LIMIT)
                act = (u + 1.0) * (g * jax.nn.sigmoid(SWIGLU_ALPHA * g))
                h_scr[f, rows, :] = act.astype(bf16)

            for_row_tiles(up_tile)

            for j in range(stage.shape[0]):
                @pl.when(has_next & (f + j * nf < next_nsub))
                def _():
                    load_copy(next_row0, f + j * nf, j).wait()
                    x_scr[1 - xcur, rows_of(f + j * nf), :] = stage[j].astype(bf16)

            return carry

        lax.fori_loop(0, nf, up_chunk, 0)

        def store_copy(first, dcol, sl):
            dst = ys_ref.at[pl.ds(row0 + first, ROW_PAD), pl.ds(pl.multiple_of(dcol * td, td), td)]
            return pltpu.make_async_copy(y_scr.at[sl, pl.ds(first, ROW_PAD), :], dst, st_sem.at[sl])

        def wait_stores(sl, count):
            def w(r, c):
                store_copy(0, 0, sl).wait()
                return c
            lax.fori_loop(0, count, w, 0)

        def down_chunk(dcol, carry):
            slot = (dcol // 2) % 2
            yslot = dcol % 2
            down_copy(e, dcol).wait()

            @pl.when(dcol == 0)
            def _():
                down_copy(e, 2).start()
                down_copy(e, 3).start()

            @pl.when((dcol >= 1) & (dcol + 3 < nd))
            def _():
                down_copy(e, dcol + 3).start()

            @pl.when((dcol + 2 == nd) & has_next)
            def _():
                for c in up_copies(meta[_M_EXPERT, nxt], 0, 0):
                    c.start()

            bd = bd_ref[dcol]

            @pl.when(dcol >= 2)
            def _():
                wait_stores(yslot, nsub)

            @pl.when((dcol < 2) & (it > 0))
            def _():
                wait_stores(yslot, meta[_M_NSUB, jnp.maximum(it - 1, 0)])

            def down_tile(first, n):
                rows = pl.ds(first, n)
                hidden = jnp.concatenate([h_scr[cf, rows, :] for cf in range(nf)], axis=1)
                y_scr[yslot, rows, :] = bd + jnp.dot(hidden, wbuf[slot, down_rows(dcol), :].astype(bf16),
                                                     preferred_element_type=f32)
                for part in range(n // ROW_PAD):
                    store_copy(first + part * ROW_PAD, dcol, yslot).start()

            for_row_tiles(down_tile)

            return carry

        lax.fori_loop(0, nd, down_chunk, 0)

        @pl.when(jnp.logical_not(has_next))
        def _():
            wait_stores(0, nsub)
            wait_stores(1, nsub)


def _experts(meta, xs, w_gate_up, b_gate_up, w_down, b_down):
    n_rows, d = xs.shape
    ne, _, f2 = w_gate_up.shape
    f = f2 // 2
    tf, td = 256, 512
    nf, nd = f // tf, d // td
    max_items = (n_rows + ne * (EXPERT_CAP - ROW_PAD)) // EXPERT_CAP
    stage_slots = max(2, pl.cdiv(EXPERT_CAP // ROW_PAD, nf))
    assert max_items <= MAX_ITEMS and td == 2 * tf and f <= d
    assert nf % 2 == 0 and nd % 4 == 0 and 2 * f == d

    def expert(i, m):
        return m[_M_EXPERT, jnp.minimum(i, m[_M_NITEMS, 0] - 1)]

    return pl.pallas_call(
        functools.partial(_expert_kernel, nf=nf, nd=nd, tf=tf, td=td),
        grid_spec=pltpu.PrefetchScalarGridSpec(
            num_scalar_prefetch=1,
            grid=(max_items,),
            in_specs=[
                pl.BlockSpec(memory_space=pl.ANY),
                pl.BlockSpec(memory_space=pl.ANY),
                pl.BlockSpec(memory_space=pl.ANY),
                pl.BlockSpec((None, 2 * nf, 1, tf), lambda i, m: (expert(i, m), 0, 0, 0)),
                pl.BlockSpec((None, nd, 1, td), lambda i, m: (expert(i, m), 0, 0, 0)),
            ],
            out_specs=pl.BlockSpec(memory_space=pl.ANY),
            scratch_shapes=[
                pltpu.VMEM((2, EXPERT_CAP, d), bf16),
                pltpu.VMEM((nf, EXPERT_CAP, tf), bf16),
                pltpu.VMEM((2, d, 2 * tf), f32),
                pltpu.VMEM((stage_slots, ROW_PAD, d), f32),
                pltpu.VMEM((2, EXPERT_CAP, td), f32),
                pltpu.SemaphoreType.DMA((4,)),
                pltpu.SemaphoreType.DMA((stage_slots,)),
                pltpu.SemaphoreType.DMA((2,)),
            ],
        ),
        out_shape=jax.ShapeDtypeStruct((n_rows, d), f32),
        compiler_params=_cparams(("arbitrary",)),
        name="experts",
    )(meta, xs, w_gate_up, w_down, b_gate_up.reshape(ne, 2 * nf, 1, tf), b_down.reshape(ne, nd, 1, td))


def _combine_kernel(pos_cur, pos_nxt, gcol_ref, x_ref, ys_ref, o_ref, gbuf, sem, *, tmc, n_tiles):
    i = pl.program_id(0)
    slot = i % 2

    def row_copy(pos_ref, sl, r, k):
        return pltpu.make_async_copy(ys_ref.at[pl.ds(pos_ref[k, r], 1), :],
                                     gbuf.at[sl, k, pl.ds(r, 1), :], sem.at[sl])

    def issue(pos_ref, sl):
        def body(r, c):
            for k in range(TOP_K):
                row_copy(pos_ref, sl, r, k).start()
            return c
        lax.fori_loop(0, tmc, body, 0)

    @pl.when(i == 0)
    def _():
        issue(pos_cur, 0)

    @pl.when(i + 1 < n_tiles)
    def _():
        issue(pos_nxt, 1 - slot)

    for k in range(TOP_K):
        pltpu.make_async_copy(ys_ref.at[pl.ds(0, tmc), :], gbuf.at[slot, k], sem.at[slot]).wait()
    acc = x_ref[...]
    for k in range(TOP_K):
        acc = acc + gcol_ref[:, k:k + 1] * gbuf[slot, k]
    o_ref[...] = acc


def _combine(pos_t, gcol, x1, ys):
    t, d = x1.shape
    tmc = 128
    n_tiles = t // tmc
    return pl.pallas_call(
        functools.partial(_combine_kernel, tmc=tmc, n_tiles=n_tiles),
        grid=(n_tiles,),
        in_specs=[
            pl.BlockSpec((TOP_K, tmc), lambda i: (0, i), memory_space=pltpu.SMEM),
            pl.BlockSpec((TOP_K, tmc), lambda i: (0, jnp.minimum(i + 1, n_tiles - 1)), memory_space=pltpu.SMEM),
            pl.BlockSpec((tmc, LANES), lambda i: (i, 0)),
            pl.BlockSpec((tmc, d), lambda i: (i, 0)),
            pl.BlockSpec(memory_space=pl.ANY),
        ],
        out_specs=pl.BlockSpec((tmc, d), lambda i: (i, 0)),
        out_shape=jax.ShapeDtypeStruct((t, d), f32),
        scratch_shapes=[pltpu.VMEM((2, TOP_K, tmc, d), f32), pltpu.SemaphoreType.DMA((2,))],
        compiler_params=_cparams(("arbitrary",)),
        name="combine",
    )(pos_t, pos_t, gcol, x1, ys)


def _layer(x2, batch, seq, g_mix, w_in, b_forget, g_q, g_k, w_conv, g_attn_out, g_conv_out, w_out,
           g_ffn, w_router, b_router, w_gate_up, b_gate_up, w_down, b_down):
    t, d = x2.shape
    nheads = b_forget.shape[0]
    att_width = nheads * HEAD_DIM
    ne = w_router.shape[1]
    c0 = 3 * att_width
    w_t = w_in.T
    conv_width = (w_t.shape[0] - c0 - nheads) // 3

    h, c = _prenorm(x2, g_mix, w_t, c0, b_forget, seq)
    qkv = _qkv(h, w_t, g_q, g_k, att_width)
    conv = _conv(h, w_t, c0 + nheads, conv_width, w_conv, seq)
    att = _attention(qkv, c, batch, seq, nheads)
    x1 = _outproj(att, conv, g_attn_out, g_conv_out, w_out.astype(bf16), x2)

    pos_t, gcol, meta = _route(_router(x1, g_ffn, w_router, b_router))
    n_rows = t * TOP_K + ne * ROW_PAD
    xs = _dispatch(meta, pos_t, x1, g_ffn, n_rows, ne)
    ys = _experts(meta, xs, w_gate_up, b_gate_up, w_down, b_down)
    return _combine(pos_t, gcol, x1, ys)


def kernel(x, g_mix, w_in, b_forget, g_q, g_k, w_conv, g_attn_out, g_conv_out, w_out, g_ffn, w_router,
           b_router, w_gate_up, b_gate_up, w_down, b_down):
    b, s, d = x.shape
    x2 = x.reshape(b * s, d)
    for l in range(g_mix.shape[0]):
        x2 = _layer(x2, b, s, g_mix[l], w_in[l], b_forget[l], g_q[l], g_k[l], w_conv[l], g_attn_out[l],
                    g_conv_out[l], w_out[l], g_ffn[l], w_router[l], b_router[l], w_gate_up[l],
                    b_gate_up[l], w_down[l], b_down[l])
    return x2.reshape(b, s, d)
```
